```python
import math
import jax, jax.numpy as jnp
from jax import lax
import numpy as np

D_MODEL = 1024
BATCH = 4
SEQ = 8192
DEPTH = 2

GRID_W = 64
CTX_LEN = 256
MIX_W = D_MODEL
FOURIER_GROUPS = 4
FOURIER_W = MIX_W // 4
POOL_WINDOWS = (2, 4, 8, 16)
POOL_W = MIX_W // 4
POOL_GROUP = POOL_W // len(POOL_WINDOWS)
HEAD_DIM = 64
ATT_HEADS = (MIX_W // 4) // HEAD_DIM
ATT_KV_HEADS = 2
WINDOW = 128
ATT_BLOCK = 128
ROPE_BASE = 10000.0
RET_HEADS = 4
RET_DK = (MIX_W // 4) // RET_HEADS
RET_DV = RET_DK
RET_CHUNK = 128
N_EXPERTS = 32
TOP_K = 4
D_EXPERT = D_MODEL
SWIGLU_LIMIT = 7.0
SWIGLU_ALPHA = 1.702
MOE_BLOCK = 128
DEEPNORM_ALPHA = (2 * DEPTH) ** 0.25
DEEPNORM_BETA = (8 * DEPTH) ** -0.25
LN_EPS = 1e-6
NEG_INF = -1e30
IN_SIZES = (FOURIER_W, POOL_W, ATT_HEADS * HEAD_DIM, ATT_KV_HEADS * HEAD_DIM, ATT_KV_HEADS * HEAD_DIM,
            RET_HEADS * RET_DK, RET_HEADS * RET_DK, RET_HEADS * RET_DV, RET_HEADS * RET_DV)
IN_W = sum(IN_SIZES)

kernel_name = 'hybrid_parallel_groups_moe_dit'


def layer_norm(x, g=None, b=None):
    xf = x.astype(jnp.float32)
    mu = jnp.mean(xf, axis=-1, keepdims=True)
    var = jnp.mean(jnp.square(xf - mu), axis=-1, keepdims=True)
    y = (xf - mu) * lax.rsqrt(var + LN_EPS)
    if g is not None:
        y = y * g.astype(jnp.float32) + b.astype(jnp.float32)
    return y.astype(x.dtype)


def axial_rope(x, row, col):
    half = x.shape[-1] // 2
    inv = 1.0 / (ROPE_BASE ** (jnp.arange(0, half, 2, dtype=jnp.float32) / half))

    def rot(xp, pos):
        ang = pos.astype(jnp.float32)[:, None] * inv[None, :]
        cos = jnp.cos(ang)[None, :, None, :].astype(xp.dtype)
        sin = jnp.sin(ang)[None, :, None, :].astype(xp.dtype)
        x1, x2 = jnp.split(xp, 2, axis=-1)
        return jnp.concatenate([x1 * cos - x2 * sin, x2 * cos + x1 * sin], axis=-1)

    return jnp.concatenate([rot(x[..., :half], row), rot(x[..., half:], col)], axis=-1)


def fourier_mix(a):
    B, N, _ = a.shape
    af = a.reshape(B, N, FOURIER_GROUPS, FOURIER_W // FOURIER_GROUPS).astype(jnp.float32)
    y = jnp.fft.fftn(af, axes=(1, 3), norm='ortho').real
    return y.astype(a.dtype).reshape(B, N, FOURIER_W)


def centred_window_mean(x, w):
    N = x.shape[1]
    cs = jnp.concatenate([jnp.zeros_like(x[:, :1]), jnp.cumsum(x, axis=1)], axis=1)
    t = jnp.arange(N)
    lo = jnp.clip(t - w // 2, 0, N)
    hi = jnp.clip(t + w // 2, 0, N)
    cnt = (hi - lo).astype(x.dtype)
    return (cs[:, hi] - cs[:, lo]) / cnt[None, :, None]


def pool_mix(b, pool_w, pool_scale):
    B, N, _ = b.shape
    bg = b.reshape(B, N, len(POOL_WINDOWS), POOL_GROUP).astype(jnp.float32)
    pooled = jnp.stack([centred_window_mean(bg[:, :, i], w) - bg[:, :, i] for i, w in enumerate(POOL_WINDOWS)], axis=2)
    y = jnp.einsum('bngc,gcd->bngd', pooled.astype(b.dtype), pool_w)
    return y.reshape(B, N, POOL_W) * pool_scale


def banded_attention(q, k, v, k_ctx, v_ctx, sinks):
    B, N, _, _ = q.shape
    L = k_ctx.shape[1]
    nb = N // ATT_BLOCK
    G = ATT_HEADS // ATT_KV_HEADS
    scale = HEAD_DIM ** -0.5
    qb = q.reshape(B, nb, ATT_BLOCK, ATT_KV_HEADS, G, HEAD_DIM)

    def band(t):
        tp = jnp.pad(t, ((0, 0), (ATT_BLOCK, ATT_BLOCK), (0, 0), (0, 0)))
        tp = tp.reshape(B, nb + 2, ATT_BLOCK, ATT_KV_HEADS, HEAD_DIM)
        return jnp.concatenate([tp[:, :-2], tp[:, 1:-1], tp[:, 2:]], axis=2)

    kb, vb = band(k), band(v)
    s_loc = jnp.einsum('bnqkgd,bnskd->bnkgqs', qb, kb).astype(jnp.float32) * scale
    s_ctx = jnp.einsum('bnqkgd,blkd->bnkgql', qb, k_ctx).astype(jnp.float32) * scale
    blk = jnp.arange(nb)[:, None]
    qpos = blk * ATT_BLOCK + jnp.arange(ATT_BLOCK)[None]
    kpos = (blk - 1) * ATT_BLOCK + jnp.arange(3 * ATT_BLOCK)[None]
    valid = ((jnp.abs(qpos[:, :, None] - kpos[:, None, :]) <= WINDOW)
             & (kpos >= 0)[:, None, :] & (kpos < N)[:, None, :])
    s_loc = jnp.where(valid[None, :, None, None], s_loc, NEG_INF)
    sink = jnp.broadcast_to(sinks.astype(jnp.float32).reshape(ATT_KV_HEADS, G)[None, None, :, :, None, None],
                            s_loc.shape[:-1] + (1,))
    p = jax.nn.softmax(jnp.concatenate([s_loc, s_ctx, sink], axis=-1), axis=-1).astype(v.dtype)
    nloc = 3 * ATT_BLOCK
    o = (jnp.einsum('bnkgqs,bnskd->bnqkgd', p[..., :nloc], vb)
         + jnp.einsum('bnkgql,blkd->bnqkgd', p[..., nloc:nloc + L], v_ctx))
    return o.reshape(B, N, ATT_HEADS * HEAD_DIM)


def context_attention(q, k, v, sinks):
    B, L, _, _ = q.shape
    G = ATT_HEADS // ATT_KV_HEADS
    qg = q.reshape(B, L, ATT_KV_HEADS, G, HEAD_DIM)
    s = jnp.einsum('blkgd,bmkd->bkglm', qg, k).astype(jnp.float32) * (HEAD_DIM ** -0.5)
    sink = jnp.broadcast_to(sinks.astype(jnp.float32).reshape(ATT_KV_HEADS, G)[None, :, :, None, None],
                            s.shape[:-1] + (1,))
    p = jax.nn.softmax(jnp.concatenate([s, sink], axis=-1), axis=-1).astype(v.dtype)
    o = jnp.einsum('bkglm,bmkd->blkgd', p[..., :L], v)
    return o.reshape(B, L, ATT_HEADS * HEAD_DIM)


def retention_chunkwise(q, k, v, log_g, state0):
    B, N, H, dk = q.shape
    dv = v.shape[-1]
    nc = N // RET_CHUNK
    qc = q.reshape(B, nc, RET_CHUNK, H, dk)
    kc = k.reshape(B, nc, RET_CHUNK, H, dk)
    vc = v.reshape(B, nc, RET_CHUNK, H, dv)
    i = jnp.arange(RET_CHUNK, dtype=jnp.float32)
    diff = i[:, None] - i[None, :]
    dmask = jnp.where(diff >= 0, jnp.exp(jnp.maximum(diff, 0.0)[None] * log_g[:, None, None]), 0.0)
    scores = jnp.einsum('bnqhd,bnshd->bnhqs', qc, kc) * dmask[None, None]
    o_inner = jnp.einsum('bnhqs,bnshv->bnqhv', scores, vc)
    zeta = jnp.exp((RET_CHUNK - 1.0 - i)[:, None] * log_g[None, :])
    kv = jnp.einsum('bnshd,sh,bnshv->bnhdv', kc, zeta, vc)
    chunk_decay = jnp.exp(RET_CHUNK * log_g)[None, :, None, None]

    def step(R, kv_n):
        return chunk_decay * R + kv_n, R

    R_final, R_prev = lax.scan(step, state0, jnp.moveaxis(kv, 1, 0))
    R_prev = jnp.moveaxis(R_prev, 0, 1)
    xi = jnp.exp((i + 1.0)[:, None] * log_g[None, :])
    o_cross = jnp.einsum('bnqhd,bnhdv->bnqhv', qc, R_prev) * xi[None, None, :, :, None]
    return (o_inner + o_cross).reshape(B, N, H, dv), R_final


def bidirectional_retention(q_l, k_l, v_l, q_c, k_c, v_c, ret_decay):
    def heads(t):
        return t.reshape(t.shape[0], t.shape[1], RET_HEADS, -1).astype(jnp.float32)

    q_l, k_l, v_l, q_c, k_c, v_c = [heads(t) for t in (q_l, k_l, v_l, q_c, k_c, v_c)]
    k_l = k_l * (RET_DK ** -0.5)
    k_c = k_c * (RET_DK ** -0.5)
    log_g = -jnp.exp(ret_decay.astype(jnp.float32))
    B = q_l.shape[0]
    state0 = jnp.zeros((B, RET_HEADS, RET_DK, RET_DV), jnp.float32)

    def flip(t):
        return jnp.flip(t, axis=1)

    o_cf, s_f = retention_chunkwise(q_c, k_c, v_c, log_g[0], state0)
    o_lf, _ = retention_chunkwise(q_l, k_l, v_l, log_g[0], s_f)
    o_cb, s_b = retention_chunkwise(flip(q_c), flip(k_c), flip(v_c), log_g[1], state0)
    o_lb, _ = retention_chunkwise(flip(q_l), flip(k_l), flip(v_l), log_g[1], s_b)
    return o_lf + flip(o_lb), o_cf + flip(o_cb)


def retention_out(o, g):
    B, N = o.shape[:2]
    gn = layer_norm(o).reshape(B, N, RET_HEADS * RET_DV).astype(g.dtype)
    return jax.nn.silu(g) * gn


def token_mixers(h_lat, h_ctx, w_in, pool_w, pool_scale, sinks, ret_decay, w_out, row, col, need_ctx):
    B, N, _ = h_lat.shape
    L = h_ctx.shape[1]
    split_pts = np.cumsum(IN_SIZES)[:-1].tolist()
    a_l, b_l, q_l, k_l, v_l, rq_l, rk_l, rv_l, rg_l = jnp.split(h_lat @ w_in, split_pts, axis=-1)
    a_c, b_c, q_c, k_c, v_c, rq_c, rk_c, rv_c, rg_c = jnp.split(h_ctx @ w_in, split_pts, axis=-1)
    q_l = axial_rope(q_l.reshape(B, N, ATT_HEADS, HEAD_DIM), row, col)
    k_l = axial_rope(k_l.reshape(B, N, ATT_KV_HEADS, HEAD_DIM), row, col)
    v_l = v_l.reshape(B, N, ATT_KV_HEADS, HEAD_DIM)
    k_c = k_c.reshape(B, L, ATT_KV_HEADS, HEAD_DIM)
    v_c = v_c.reshape(B, L, ATT_KV_HEADS, HEAD_DIM)
    att_l = banded_attention(q_l, k_l, v_l, k_c, v_c, sinks)
    ret_l, ret_c = bidirectional_retention(rq_l, rk_l, rv_l, rq_c, rk_c, rv_c, ret_decay)
    y_lat = jnp.concatenate([fourier_mix(a_l), pool_mix(b_l, pool_w, pool_scale), att_l,
                             retention_out(ret_l, rg_l)], axis=-1) @ w_out
    if not need_ctx:
        return y_lat, None
    att_c = context_attention(q_c.reshape(B, L, ATT_HEADS, HEAD_DIM), k_c, v_c, sinks)
    y_ctx = jnp.concatenate([fourier_mix(a_c), pool_mix(b_c, pool_w, pool_scale), att_c,
                             retention_out(ret_c, rg_c)], axis=-1) @ w_out
    return y_lat, y_ctx


def clamped_swiglu(u):
    glu = jnp.minimum(u[..., ::2], SWIGLU_LIMIT)
    lin = jnp.clip(u[..., 1::2], -SWIGLU_LIMIT, SWIGLU_LIMIT)
    return glu * jax.nn.sigmoid(SWIGLU_ALPHA * glu) * (lin + 1.0)


def moe_ffn(h, router_w, router_b, w1, b1, w2, b2):
    T, D = h.shape
    logits = (h @ router_w + router_b).astype(jnp.float32)
    top_v, top_i = lax.top_k(logits, TOP_K)
    gates = jax.nn.softmax(top_v, axis=-1)
    A = T * TOP_K
    flat_e = top_i.reshape(-1)
    flat_g = gates.reshape(-1).astype(h.dtype)
    order = jnp.argsort(flat_e)
    sorted_e = flat_e[order]
    counts = jnp.bincount(flat_e, length=N_EXPERTS)
    padded = (counts + MOE_BLOCK - 1) // MOE_BLOCK * MOE_BLOCK
    pad_end = jnp.cumsum(padded)
    pad_start = pad_end - padded
    grp_start = jnp.cumsum(counts) - counts
    dest = pad_start[sorted_e] + jnp.arange(A) - grp_start[sorted_e]
    nblk = -(-A // MOE_BLOCK) + N_EXPERTS
    P = nblk * MOE_BLOCK
    row_tok = jnp.full((P,), T, jnp.int32).at[dest].set((order // TOP_K).astype(jnp.int32))
    row_gate = jnp.zeros((P,), h.dtype).at[dest].set(flat_g[order])
    blk_e = jnp.minimum(jnp.searchsorted(pad_end, jnp.arange(nblk) * MOE_BLOCK, side='right'), N_EXPERTS - 1)
    h_pad = jnp.concatenate([h, jnp.zeros((1, D), h.dtype)], axis=0)

    def expert_block(args):
        rows, e = args
        u = h_pad[rows] @ w1[e] + b1[e]
        return clamped_swiglu(u) @ w2[e] + b2[e]

    y = lax.map(expert_block, (row_tok.reshape(nblk, MOE_BLOCK), blk_e))
    y = y.reshape(P, D) * row_gate[:, None]
    return jnp.zeros((T + 1, D), y.dtype).at[row_tok].add(y)[:T]


def setup_inputs(seed: int = 0) -> dict:
    key = jax.random.key(seed)
    ks = jax.random.split(key, 20)
    nrm = jax.random.normal
    f32 = jnp.float32
    base_decay = -(5.0 + jnp.arange(RET_HEADS, dtype=f32)) * math.log(2.0)
    return {
        'x': nrm(ks[0], (BATCH, SEQ, D_MODEL), f32),
        'c': nrm(ks[1], (BATCH, D_MODEL), f32),
        'ctx': nrm(ks[2], (BATCH, CTX_LEN, D_MODEL), f32),
        'c_ctx': nrm(ks[3], (D_MODEL,), f32),
        'w_mod': nrm(ks[4], (DEPTH, D_MODEL, 6 * D_MODEL), f32) * D_MODEL ** -0.5,
        'b_mod': 0.02 * nrm(ks[5], (DEPTH, 6 * D_MODEL), f32),
        'w_in': nrm(ks[6], (DEPTH, D_MODEL, IN_W), f32) * D_MODEL ** -0.5,
        'pool_w': nrm(ks[7], (DEPTH, len(POOL_WINDOWS), POOL_GROUP, POOL_GROUP), f32) * POOL_GROUP ** -0.5,
        'pool_scale': 1.0 + 0.1 * nrm(ks[8], (DEPTH, POOL_W), f32),
        'attn_sink': 0.5 * nrm(ks[9], (DEPTH, ATT_HEADS), f32),
        'ret_decay': jnp.log(-base_decay)[None, None, :] * 0.0 + jnp.log(jnp.exp(base_decay))[None, None, :] + 0.1 * nrm(ks[10], (DEPTH, 2, RET_HEADS), f32),
        'w_out': nrm(ks[11], (DEPTH, MIX_W, D_MODEL), f32) * MIX_W ** -0.5 * DEEPNORM_BETA,
        'ln_g': 1.0 + 0.02 * nrm(ks[12], (DEPTH, 2, D_MODEL), f32),
        'ln_b': 0.02 * nrm(ks[13], (DEPTH, 2, D_MODEL), f32),
        'router_w': nrm(ks[14], (DEPTH, D_MODEL, N_EXPERTS), f32) * D_MODEL ** -0.5,
        'router_b': 0.01 * nrm(ks[15], (DEPTH, N_EXPERTS), f32),
        'exp_w1': nrm(ks[16], (DEPTH, N_EXPERTS, D_MODEL, 2 * D_EXPERT), f32) * D_MODEL ** -0.5,
        'exp_b1': 0.02 * nrm(ks[17], (DEPTH, N_EXPERTS, 2 * D_EXPERT), f32),
        'exp_w2': nrm(ks[18], (DEPTH, N_EXPERTS, D_EXPERT, D_MODEL), f32) * D_EXPERT ** -0.5 * DEEPNORM_BETA,
        'exp_b2': 0.02 * nrm(ks[19], (DEPTH, N_EXPERTS, D_MODEL), f32),
    }


def reference(x, c, ctx, c_ctx, w_mod, b_mod, w_in, pool_w, pool_scale, attn_sink, ret_decay, w_out,
              ln_g, ln_b, router_w, router_b, exp_w1, exp_b1, exp_w2, exp_b2):
    B, N, D = x.shape
    L = ctx.shape[1]
    ROWS = N // GRID_W
    row = jnp.broadcast_to(jnp.arange(ROWS)[:, None], (ROWS, GRID_W)).reshape(-1)
    col = jnp.broadcast_to(jnp.arange(GRID_W)[None, :], (ROWS, GRID_W)).reshape(-1)
    x_lat, x_ctx = x, ctx
    for l in range(DEPTH):
        need_ctx = l < DEPTH - 1
        sh1, sc1, g1, sh2, sc2, g2 = [m[:, None, :] for m in
                                      jnp.split(jax.nn.silu(c) @ w_mod[l] + b_mod[l], 6, axis=-1)]
        csh1, csc1, cg1, csh2, csc2, cg2 = jnp.split(jax.nn.silu(c_ctx) @ w_mod[l] + b_mod[l], 6, axis=-1)
        h_lat = layer_norm(x_lat) * (1.0 + sc1) + sh1
        h_ctx = layer_norm(x_ctx) * (1.0 + csc1) + csh1
        y_lat, y_ctx = token_mixers(h_lat, h_ctx, w_in[l], pool_w[l], pool_scale[l], attn_sink[l],
                                    ret_decay[l], w_out[l], row, col, need_ctx)
        x_lat = layer_norm(DEEPNORM_ALPHA * x_lat + g1 * y_lat, ln_g[l, 0], ln_b[l, 0])
        h2_lat = (layer_norm(x_lat) * (1.0 + sc2) + sh2).reshape(B * N, D)
        if need_ctx:
            x_ctx = layer_norm(DEEPNORM_ALPHA * x_ctx + cg1 * y_ctx, ln_g[l, 0], ln_b[l, 0])
            h2_ctx = (layer_norm(x_ctx) * (1.0 + csc2) + csh2).reshape(B * L, D)
            f = moe_ffn(jnp.concatenate([h2_ctx, h2_lat], axis=0), router_w[l], router_b[l],
                        exp_w1[l], exp_b1[l], exp_w2[l], exp_b2[l])
            f_ctx = f[:B * L].reshape(B, L, D)
            f_lat = f[B * L:].reshape(B, N, D)
            x_ctx = layer_norm(DEEPNORM_ALPHA * x_ctx + cg2 * f_ctx, ln_g[l, 1], ln_b[l, 1])
        else:
            f_lat = moe_ffn(h2_lat, router_w[l], router_b[l], exp_w1[l], exp_b1[l],
                            exp_w2[l], exp_b2[l]).reshape(B, N, D)
        x_lat = layer_norm(DEEPNORM_ALPHA * x_lat + g2 * f_lat, ln_g[l, 1], ln_b[l, 1])
    return x_lat
```

```python
import functools

import numpy as np
import jax
import jax.numpy as jnp
from jax import lax
from jax.experimental import pallas as pl
from jax.experimental.pallas import tpu as pltpu

F32 = jnp.float32
BF16 = jnp.bfloat16
I32 = jnp.int32

D_MODEL = 1024
GRID_W = 64
FOURIER_GROUP_W = 64
POOL_WINDOWS = (2, 4, 8, 16)
HEAD_DIM = 64
WINDOW = 128
ROPE_BASE = 10000.0
RET_HEADS = 4
RET_CHUNK = 128
N_EXPERTS = 32
TOP_K = 4
SWIGLU_LIMIT = 7.0
SWIGLU_ALPHA = 1.702
LN_EPS = 1e-6
NEG_INF = -1e30
MIX_GROUP_W = 256
FFT_RADIX2 = 64

TILE = 256
MOE_BLK = 256
MOE_TOK_TILE = 512
VMEM_LIMIT = 48 * 1024 * 1024


def _dot(a, b):
    return jnp.dot(a, b, preferred_element_type=F32)


def _dot_nt(a, b):
    return lax.dot_general(a, b, (((1,), (1,)), ((), ())), preferred_element_type=F32)


def _dot_tn(a, b):
    return lax.dot_general(a, b, (((0,), (0,)), ((), ())), preferred_element_type=F32)


def _normalize(x):
    mu = jnp.mean(x, axis=-1, keepdims=True)
    xc = x - mu
    var = jnp.mean(xc * xc, axis=-1, keepdims=True)
    return xc * lax.rsqrt(var + LN_EPS)


def _cparams(sem=None, vmem=None):
    kw = {}
    if sem is not None:
        kw["dimension_semantics"] = sem
    if vmem is not None:
        kw["vmem_limit_bytes"] = vmem
    return pltpu.CompilerParams(**kw)


def _mod_kernel(c_ref, w_ref, b_ref, o_ref):
    c = c_ref[...]
    a = c * jax.nn.sigmoid(c)
    w = w_ref[...]
    a_hi = a.astype(BF16)
    a_lo = (a - a_hi.astype(F32)).astype(BF16)
    w_hi = w.astype(BF16)
    w_lo = (w - w_hi.astype(F32)).astype(BF16)
    o_ref[...] = _dot(a_hi, w_hi) + _dot(a_lo, w_hi) + _dot(a_hi, w_lo) + b_ref[...]


def _modulation(cvec, w_mod, b_mod):
    depth, d, six_d = w_mod.shape
    nb = 1536
    return pl.pallas_call(
        _mod_kernel,
        grid=(depth, six_d // nb),
        in_specs=[
            pl.BlockSpec((8, d), lambda l, j: (0, 0)),
            pl.BlockSpec((None, d, nb), lambda l, j: (l, 0, j)),
            pl.BlockSpec((None, 1, nb), lambda l, j: (l, 0, j)),
        ],
        out_specs=pl.BlockSpec((None, 8, nb), lambda l, j: (l, 0, j)),
        out_shape=jax.ShapeDtypeStruct((depth, 8, six_d), F32),
        compiler_params=_cparams(vmem=VMEM_LIMIT),
        name="modulation",
    )(cvec, w_mod, b_mod.reshape(depth, 1, six_d))


def _proj_kernel(*refs, split_input, n_lat_tiles):
    if split_input:
        x_ref, ctx_ref = refs[:2]
        refs = refs[2:]
        is_ctx = pl.program_id(1) == n_lat_tiles
        x = jnp.where(is_ctx, ctx_ref[...], x_ref[...])
    else:
        x = refs[0][...]
        refs = refs[1:]
    sc_ref, sh_ref, w_ref, cos_ref, sin_ref = refs[:5]
    a_ref, bp_ref, q_ref, k_ref, v_ref, rq_ref, rk_ref, rv_ref, rg_ref = refs[5:]
    h = _normalize(x) * (1.0 + sc_ref[...]) + sh_ref[...]
    p = _dot(h.astype(BF16), w_ref[...])
    g = MIX_GROUP_W
    a_ref[...] = p[:, 0:g].astype(BF16)
    bp_ref[...] = p[:, g:2 * g].astype(BF16)
    qk = p[:, 2 * g:4 * g]
    lane = lax.broadcasted_iota(I32, qk.shape, 1)
    first = (lane & 16) == 0
    partner = jnp.where(first, pltpu.roll(qk, 2 * g - 16, 1), pltpu.roll(qk, 16, 1))
    cos = jnp.concatenate([cos_ref[...]] * 4, axis=-1)
    sin = jnp.concatenate([sin_ref[...]] * 4, axis=-1)
    qk = qk * cos + partner * sin
    q_ref[...] = (qk[:, :g] * (HEAD_DIM ** -0.5)).astype(BF16)
    k_ref[...] = qk[:, g:].astype(BF16)
    v_ref[...] = p[:, 4 * g:5 * g].astype(BF16)
    rq_ref[...] = p[:, 5 * g:6 * g].astype(BF16)
    rk_ref[...] = p[:, 6 * g:7 * g].astype(BF16)
    rv_ref[...] = p[:, 7 * g:8 * g].astype(BF16)
    rg_ref[...] = p[:, 8 * g:9 * g].astype(BF16)


def _mod_spec(n_lat_tiles, ctx_row):
    return pl.BlockSpec((None, 1, D_MODEL), lambda b, i: (jnp.where(i == n_lat_tiles, ctx_row, b), 0, 0))


def _x_specs(split_input, n_lat_tiles):
    if split_input:
        return [
            pl.BlockSpec((None, TILE, D_MODEL), lambda b, i: (b, jnp.minimum(i, n_lat_tiles - 1), 0)),
            pl.BlockSpec((None, TILE, D_MODEL), lambda b, i: (b, 0, 0)),
        ]
    return [pl.BlockSpec((None, TILE, D_MODEL), lambda b, i: (b, i, 0))]


def _projection(xs, sc, sh, w_ext, cos, sin, *, B, S, n_lat_tiles):
    split_input = len(xs) == 2
    nt = S // TILE
    g = MIX_GROUP_W
    out_spec = pl.BlockSpec((None, TILE, g), lambda b, i: (b, i, 0))
    out_shape = jax.ShapeDtypeStruct((B, S, g), BF16)
    return pl.pallas_call(
        functools.partial(_proj_kernel, split_input=split_input, n_lat_tiles=n_lat_tiles),
        grid=(B, nt),
        in_specs=_x_specs(split_input, n_lat_tiles) + [
            _mod_spec(n_lat_tiles, B),
            _mod_spec(n_lat_tiles, B),
            pl.BlockSpec(w_ext.shape, lambda b, i: (0, 0)),
            pl.BlockSpec((TILE, 128), lambda b, i: (i, 0)),
            pl.BlockSpec((TILE, 128), lambda b, i: (i, 0)),
        ],
        out_specs=[out_spec] * 9,
        out_shape=[out_shape] * 9,
        compiler_params=_cparams(vmem=VMEM_LIMIT),
        name="ln_mod_in_proj",
    )(*xs, sc, sh, w_ext, cos, sin)


def _channel_dft_matrix():
    c = np.arange(FOURIER_GROUP_W)
    ang = 2.0 * np.pi * np.outer(c, c) / FOURIER_GROUP_W
    eye = np.eye(MIX_GROUP_W // FOURIER_GROUP_W)
    cs = np.concatenate([np.kron(eye, np.cos(ang)), -np.kron(eye, np.sin(ang))], axis=1)
    return jnp.asarray(cs, dtype=BF16)


def _seq_dft_stage_a(n1):
    n2 = FFT_RADIX2
    n = n1 * n2
    k1 = np.arange(n1)[None, :, None]
    i1 = np.arange(n1)[None, None, :]
    i2 = np.arange(n2)[:, None, None]
    th = 2.0 * np.pi * ((k1 * (n2 * i1 + i2)) % n) / n
    wr, wi = np.cos(th), -np.sin(th)
    wa = np.concatenate([np.concatenate([wr, -wi], axis=2), np.concatenate([wi, wr], axis=2)], axis=1)
    return jnp.asarray(wa, dtype=BF16)


def _seq_dft_stage_b():
    n2 = FFT_RADIX2
    k2 = np.arange(n2)
    ph = 2.0 * np.pi * (np.outer(k2, k2) % n2) / n2
    return jnp.asarray(np.concatenate([np.cos(ph), np.sin(ph)], axis=1), dtype=BF16)


def _dense_dft(length):
    k = np.arange(length)
    th = 2.0 * np.pi * (np.outer(k, k) % length) / length
    return jnp.asarray(np.concatenate([np.cos(th), np.sin(th)], axis=1), dtype=BF16)


def _fft_a_kernel(a_ref, cs_ref, wa_ref, t_ref, *, nb, n1):
    g = MIX_GROUP_W
    for b in range(nb):
        p = _dot(a_ref[b], cs_ref[...])
        pp = jnp.concatenate([p[:, :g], p[:, g:]], axis=0).astype(BF16)
        t = _dot(wa_ref[...], pp)
        t_ref[b] = jnp.concatenate([t[:n1], t[n1:]], axis=-1).astype(BF16)


def _fft_b_kernel(t_ref, vb_ref, y_ref, *, nb, scale):
    g = MIX_GROUP_W
    n2 = FFT_RADIX2
    for b in range(nb):
        t = t_ref[b]
        tt = jnp.concatenate([t[:, :g], t[:, g:]], axis=0)
        y = _dot(vb_ref[...], tt) * scale
        y_ref[b, 0:n2, :] = y.astype(BF16)
        y_ref[b, n2:, :] = jnp.zeros((y_ref.shape[1] - n2, g), BF16)


def _fft_ctx_kernel(a_ref, cs_ref, wc_ref, yin_ref, y_ref, *, scale):
    del yin_ref
    g = MIX_GROUP_W
    p = _dot(a_ref[...], cs_ref[...])
    pp = jnp.concatenate([p[:, :g], p[:, g:]], axis=0).astype(BF16)
    y_ref[...] = (_dot(wc_ref[...], pp) * scale).astype(BF16)


def _fourier_mix(a, *, B, N, L, with_ctx):
    S = N + L
    g = MIX_GROUP_W
    n2 = FFT_RADIX2
    n1 = N // n2
    cs = _channel_dft_matrix()
    a2 = a.reshape(B, S // n2, n2 * g)
    t = pl.pallas_call(
        functools.partial(_fft_a_kernel, nb=B, n1=n1),
        grid=(n2,),
        in_specs=[
            pl.BlockSpec((B, n1, g), lambda j: (0, 0, j)),
            pl.BlockSpec(cs.shape, lambda j: (0, 0)),
            pl.BlockSpec((None, 2 * n1, 2 * n1), lambda j: (j, 0, 0)),
        ],
        out_specs=pl.BlockSpec((B, n1, 2 * g), lambda j: (0, 0, j)),
        out_shape=jax.ShapeDtypeStruct((B, n1, n2 * 2 * g), BF16),
        name="fourier_seq_stage_a",
    )(a2, cs, _seq_dft_stage_a(n1))
    t = t.reshape(B, n1 * n2, 2 * g)
    rows = S // n1
    y = pl.pallas_call(
        functools.partial(_fft_b_kernel, nb=B, scale=float((N * FOURIER_GROUP_W) ** -0.5)),
        grid=(n1,),
        in_specs=[
            pl.BlockSpec((B, n2, 2 * g), lambda k: (0, k, 0)),
            pl.BlockSpec((n2, 2 * n2), lambda k: (0, 0)),
        ],
        out_specs=pl.BlockSpec((B, rows, g), lambda k: (0, 0, k)),
        out_shape=jax.ShapeDtypeStruct((B, rows, n1 * g), BF16),
        name="fourier_seq_stage_b",
    )(t, _seq_dft_stage_b())
    y = y.reshape(B, S, g)
    if not with_ctx:
        return y
    ctx_tile = N // TILE
    return pl.pallas_call(
        functools.partial(_fft_ctx_kernel, scale=float((L * FOURIER_GROUP_W) ** -0.5)),
        grid=(B,),
        in_specs=[
            pl.BlockSpec((None, TILE, g), lambda b: (b, ctx_tile, 0)),
            pl.BlockSpec(cs.shape, lambda b: (0, 0)),
            pl.BlockSpec((L, 2 * L), lambda b: (0, 0)),
            pl.BlockSpec(memory_space=pl.ANY),
        ],
        out_specs=pl.BlockSpec((None, TILE, g), lambda b: (b, ctx_tile, 0)),
        out_shape=jax.ShapeDtypeStruct((B, S, g), BF16),
        input_output_aliases={3: 0},
        name="fourier_ctx",
    )(a, cs, _dense_dft(L), y)


POOL_HALO = 16


def _pool_kernel(prev_ref, x_ref, next_ref, pw_ref, ps_ref, o_ref, *, n_lat, s_tot):
    i = pl.program_id(1)
    xm = x_ref[...]
    xc = jnp.concatenate([prev_ref[...], xm, next_ref[...]], axis=0)
    t0 = i * TILE
    is_ctx = t0 >= n_lat
    r0 = jnp.where(is_ctx, n_lat, 0)
    r1 = jnp.where(is_ctx, s_tot, n_lat)
    kw = TILE + 2 * POOL_HALO
    t_col = t0 + lax.broadcasted_iota(I32, (TILE, 1), 0)
    s_pos = t0 - POOL_HALO + lax.broadcasted_iota(I32, (TILE, kw), 1)
    group = lax.broadcasted_iota(I32, (TILE, MIX_GROUP_W), 1) // FOURIER_GROUP_W
    acc = jnp.zeros((TILE, MIX_GROUP_W), F32)
    for gi, w in enumerate(POOL_WINDOWS):
        lo = jnp.maximum(t_col - w // 2, r0)
        hi = jnp.minimum(t_col + w // 2, r1)
        band = jnp.where(s_pos >= lo, jnp.where(s_pos < hi, 1.0, 0.0), 0.0).astype(BF16)
        mean = _dot(band, xc) / (hi - lo).astype(F32)
        acc = jnp.where(group == gi, mean, acc)
    pooled = acc - xm.astype(F32)
    o_ref[...] = (_dot(pooled.astype(BF16), pw_ref[...]) * ps_ref[...]).astype(BF16)


def _pool_mix(bp, pw_bd, pscale, *, B, N, L):
    S = N + L
    nt = S // TILE
    g = MIX_GROUP_W
    hb = TILE // POOL_HALO
    last = S // POOL_HALO - 1
    return pl.pallas_call(
        functools.partial(_pool_kernel, n_lat=N, s_tot=S),
        grid=(B, nt),
        in_specs=[
            pl.BlockSpec((None, POOL_HALO, g), lambda b, i: (b, jnp.maximum(i * hb - 1, 0), 0)),
            pl.BlockSpec((None, TILE, g), lambda b, i: (b, i, 0)),
            pl.BlockSpec((None, POOL_HALO, g), lambda b, i: (b, jnp.minimum((i + 1) * hb, last), 0)),
            pl.BlockSpec((g, g), lambda b, i: (0, 0)),
            pl.BlockSpec((1, g), lambda b, i: (0, 0)),
        ],
        out_specs=pl.BlockSpec((None, TILE, g), lambda b, i: (b, i, 0)),
        out_shape=jax.ShapeDtypeStruct((B, S, g), BF16),
        name="pool_mix",
    )(bp, bp, bp, pw_bd, pscale)


def _attn_kernel(sink_ref, q_ref, kp_ref, kc_ref, kn_ref, kx_ref, vp_ref, vc_ref, vn_ref, vx_ref, o_ref,
                 *, n_lat):
    i = pl.program_id(1)
    t0 = i * TILE
    nband = TILE + 2 * WINDOW
    nkeys = nband + kx_ref.shape[0]
    row = lax.broadcasted_iota(I32, (2 * TILE, nkeys), 0)
    col = lax.broadcasted_iota(I32, (2 * TILE, nkeys), 1)
    qpos = t0 + jnp.where(row >= TILE, row - TILE, row)
    kpos = t0 - WINDOW + col
    in_band = (jnp.abs(qpos - kpos) <= WINDOW) & (kpos >= 0) & (kpos < n_lat) & (t0 < n_lat)
    valid = (col >= nband) | in_band
    lane = lax.broadcasted_iota(I32, (TILE, 2 * HEAD_DIM), 1)
    low = lane < HEAD_DIM
    keep_low = jnp.where(low, 1.0, 0.0).astype(BF16)
    keep_high = jnp.where(low, 0.0, 1.0).astype(BF16)
    row1 = lax.broadcasted_iota(I32, (2 * TILE, 1), 0)
    outs = []
    for h in range(2):
        sl = slice(2 * HEAD_DIM * h, 2 * HEAD_DIM * (h + 1))
        qh = q_ref[:, sl]
        qs = jnp.concatenate([qh * keep_low, qh * keep_high], axis=0)
        kh = jnp.concatenate([kp_ref[:, sl], kc_ref[:, sl], kn_ref[:, sl], kx_ref[:, sl]], axis=0)
        vh = jnp.concatenate([vp_ref[:, sl], vc_ref[:, sl], vn_ref[:, sl], vx_ref[:, sl]], axis=0)
        s = jnp.where(valid, _dot_nt(qs, kh), NEG_INF)
        sink = jnp.where(row1 < TILE, sink_ref[2 * h], sink_ref[2 * h + 1])
        m = jnp.maximum(jnp.max(s, axis=-1, keepdims=True), sink)
        p = jnp.exp(s - m)
        den = jnp.sum(p, axis=-1, keepdims=True) + jnp.exp(sink - m)
        o = _dot(p.astype(BF16), vh) / den
        outs.append(jnp.where(low, o[:TILE], o[TILE:]))
    o_ref[...] = jnp.concatenate(outs, axis=-1).astype(BF16)


def _attention(sinks, q, k2, v2, *, B, N, L):
    S = N + L
    nt = S // TILE
    g = MIX_GROUP_W
    hb = TILE // WINDOW
    last = N // WINDOW - 1
    ctx_tile = N // TILE
    prev = pl.BlockSpec((None, WINDOW, g), lambda b, i: (b, jnp.clip(i * hb - 1, 0, last), 0))
    cur = pl.BlockSpec((None, TILE, g), lambda b, i: (b, i, 0))
    nxt = pl.BlockSpec((None, WINDOW, g), lambda b, i: (b, jnp.minimum((i + 1) * hb, last), 0))
    ctx = pl.BlockSpec((None, L, g), lambda b, i: (b, ctx_tile, 0))
    return pl.pallas_call(
        functools.partial(_attn_kernel, n_lat=N),
        grid=(B, nt),
        in_specs=[pl.BlockSpec(memory_space=pltpu.SMEM), cur, prev, cur, nxt, ctx, prev, cur, nxt, ctx],
        out_specs=pl.BlockSpec((None, TILE, g), lambda b, i: (b, i, 0)),
        out_shape=jax.ShapeDtypeStruct((B, S, g), BF16),
        compiler_params=_cparams(vmem=VMEM_LIMIT),
        name="banded_attention",
    )(sinks, q, k2, k2, k2, k2, v2, v2, v2, v2)


def _ret_kernel(*refs, reverse, nb):
    if reverse:
        dec_ref, q_ref, k_ref, v_ref, of_ref, g_ref, o_ref, r_scr = refs
    else:
        dec_ref, q_ref, k_ref, v_ref, o_ref, r_scr = refs
    c = RET_CHUNK
    w = MIX_GROUP_W
    dh = w // RET_HEADS

    @pl.when(pl.program_id(0) == 0)
    def _():
        r_scr[...] = jnp.zeros(r_scr.shape, F32)

    lg = -jnp.exp(dec_ref[...])
    idx = lax.broadcasted_iota(I32, (c, 1), 0).astype(F32)
    jdx = lax.broadcasted_iota(I32, (1, c), 1).astype(F32)
    if reverse:
        idx, jdx = (c - 1.0) - idx, (c - 1.0) - jdx
    diff = idx - jdx
    dmask = [jnp.where(diff >= 0, jnp.exp(jnp.maximum(diff, 0.0) * lg[:, dh * h:dh * h + 1]), 0.0)
             for h in range(RET_HEADS)]
    xi = jnp.exp((idx + 1.0) * lg)
    zeta = jnp.exp((c - 1.0 - idx) * lg)
    chunk_decay = jnp.exp(float(c) * lg)
    lane_head = lax.broadcasted_iota(I32, (1, w), 1) // dh
    keep_head = [jnp.where(lane_head == h, 1.0, 0.0).astype(BF16) for h in range(RET_HEADS)]
    same_head =(lax.broadcasted_iota(I32, (w, w), 0) // dh) == (lax.broadcasted_iota(I32, (w, w), 1) // dh)

    for b in range(nb):
        q = q_ref[b]
        k = k_ref[b] * jnp.asarray(dh ** -0.5, BF16)
        v = v_ref[b]
        state = r_scr[b]
        o = _dot(q, state.astype(BF16)) * xi
        for h in range(RET_HEADS):
            mh = lane_head == h
            s = _dot_nt(q * keep_head[h], k) * dmask[h]
            o = o + jnp.where(mh, _dot(s.astype(BF16), v), 0.0)
        kv = _dot_tn(k, (v.astype(F32) * zeta).astype(BF16))
        r_scr[b] = state * chunk_decay + jnp.where(same_head, kv, 0.0)
        if not reverse:
            o_ref[b] = o
        else:
            tot = o + of_ref[b]
            mu = jnp.zeros_like(tot)
            for h in range(RET_HEADS):
                mh = lane_head == h
                mu = jnp.where(mh, jnp.sum(jnp.where(mh, tot, 0.0), axis=-1, keepdims=True) / dh, mu)
            cen = tot - mu
            var = jnp.zeros_like(tot)
            for h in range(RET_HEADS):
                mh = lane_head == h
                var = jnp.where(mh, jnp.sum(jnp.where(mh, cen * cen, 0.0), axis=-1, keepdims=True) / dh, var)
            gate = g_ref[b].astype(F32)
            o_ref[b] = (gate * jax.nn.sigmoid(gate) * (cen * lax.rsqrt(var + LN_EPS))).astype(BF16)


def _retention(dec, rq, rk, rv, rg, *, B, N, L):
    S = N + L
    c = RET_CHUNK
    w = MIX_GROUP_W
    nc_lat, nc_ctx = N // c, L // c
    nc = nc_lat + nc_ctx

    def fwd_blk(j):
        return jnp.where(j < nc_ctx, nc_lat + j, j - nc_ctx)

    def bwd_blk(j):
        return nc - 1 - j

    def call(blk, reverse, extra):
        spec = pl.BlockSpec((B, c, w), lambda j: (0, blk(j), 0))
        dspec = pl.BlockSpec((1, w), lambda j: (0, 0))
        return pl.pallas_call(
            functools.partial(_ret_kernel, reverse=reverse, nb=B),
            grid=(nc,),
            in_specs=[dspec] + [spec] * (3 + len(extra)),
            out_specs=spec,
            out_shape=jax.ShapeDtypeStruct((B, S, w), BF16 if reverse else F32),
            scratch_shapes=[pltpu.VMEM((B, w, w), F32)],
            compiler_params=_cparams(sem=("arbitrary",), vmem=VMEM_LIMIT),
            name="retention_bwd" if reverse else "retention_fwd",
        )(dec[1:2] if reverse else dec[0:1], rq, rk, rv, *extra)

    o_f = call(fwd_blk, False, ())
    return call(bwd_blk, True, (o_f, rg))


def _merge_kernel(*refs, split_input, n_lat_tiles, alpha):
    if split_input:
        x_ref, ctx_ref = refs[:2]
        refs = refs[2:]
        x = jnp.where(pl.program_id(1) == n_lat_tiles, ctx_ref[...], x_ref[...])
    else:
        x = refs[0][...]
        refs = refs[1:]
    (yf_ref, yp_ref, ya_ref, yr_ref, g1_ref, sc_ref, sh_ref, lg_ref, lb_ref, wo_ref, rw_ref, rb_ref,
     x1_ref, h2_ref, idx_ref, gate_ref) = refs
    ycat = jnp.concatenate([yf_ref[...], yp_ref[...], ya_ref[...], yr_ref[...]], axis=-1)
    y = _dot(ycat, wo_ref[...])
    x1 = _normalize(alpha * x + g1_ref[...] * y) * lg_ref[...] + lb_ref[...]
    x1_ref[...] = x1
    h2 = _normalize(x1) * (1.0 + sc_ref[...]) + sh_ref[...]
    h2_ref[...] = h2
    logits = _dot_nt(rw_ref[...], h2.astype(BF16)) + rb_ref[...]
    eid = lax.broadcasted_iota(I32, logits.shape, 0)
    orow = lax.broadcasted_iota(I32, (8, logits.shape[1]), 0)
    idx_out = jnp.zeros((8, logits.shape[1]), I32)
    val_out = jnp.zeros((8, logits.shape[1]), F32)
    top = None
    den = None
    for kk in range(TOP_K):
        m = jnp.max(logits, axis=0, keepdims=True)
        sel = jnp.min(jnp.where(logits == m, eid, N_EXPERTS), axis=0, keepdims=True)
        logits = jnp.where(eid == sel, -jnp.inf, logits)
        if kk == 0:
            top = m
        e = jnp.exp(m - top)
        den = e if kk == 0 else den + e
        idx_out = jnp.where(orow == kk, sel, idx_out)
        val_out = jnp.where(orow == kk, e, val_out)
    idx_ref[...] = idx_out
    gate_ref[...] = val_out / den


def _merge(xs, ys, g1, sc2, sh2, ln_g, ln_b, w_out, rw_t, rb, *, B, S, n_lat_tiles, alpha):
    split_input = len(xs) == 2
    nt = S // TILE
    g = MIX_GROUP_W
    yspec = pl.BlockSpec((None, TILE, g), lambda b, i: (b, i, 0))
    vec = pl.BlockSpec((1, D_MODEL), lambda b, i: (0, 0))
    tok = lambda b, i: (0, b * nt + i)
    return pl.pallas_call(
        functools.partial(_merge_kernel, split_input=split_input, n_lat_tiles=n_lat_tiles, alpha=alpha),
        grid=(B, nt),
        in_specs=_x_specs(split_input, n_lat_tiles) + [yspec] * 4 + [
            _mod_spec(n_lat_tiles, B), _mod_spec(n_lat_tiles, B), _mod_spec(n_lat_tiles, B), vec, vec,
            pl.BlockSpec((D_MODEL, D_MODEL), lambda b, i: (0, 0)),
            pl.BlockSpec((N_EXPERTS, D_MODEL), lambda b, i: (0, 0)),
            pl.BlockSpec((N_EXPERTS, 1), lambda b, i: (0, 0)),
        ],
        out_specs=[
            pl.BlockSpec((None, TILE, D_MODEL), lambda b, i: (b, i, 0)),
            pl.BlockSpec((TILE, D_MODEL), lambda b, i: (b * nt + i, 0)),
            pl.BlockSpec((8, TILE), tok),
            pl.BlockSpec((8, TILE), tok),
        ],
        out_shape=[
            jax.ShapeDtypeStruct((B, S, D_MODEL), F32),
            jax.ShapeDtypeStruct((B * S, D_MODEL), F32),
            jax.ShapeDtypeStruct((8, B * S), I32),
            jax.ShapeDtypeStruct((8, B * S), F32),
        ],
        compiler_params=_cparams(vmem=VMEM_LIMIT),
        name="out_proj_norm_router",
    )(*xs, *ys, g1, sc2, sh2, ln_g, ln_b, w_out, rw_t, rb)


def _rank_kernel(idx_ref, rank_ref, cnt_ref, carry_scr):
    tt = idx_ref.shape[1]

    @pl.when(pl.program_id(0) == 0)
    def _():
        carry_scr[...] = jnp.zeros(carry_scr.shape, F32)

    eid = lax.broadcasted_iota(I32, (N_EXPERTS, tt), 0)
    before = jnp.where(lax.broadcasted_iota(I32, (tt, tt), 0) < lax.broadcasted_iota(I32, (tt, tt), 1),
                       1.0, 0.0).astype(BF16)
    orow = lax.broadcasted_iota(I32, (8, tt), 0)
    base = carry_scr[:, 0:1]
    out = jnp.zeros((8, tt), I32)
    for kk in range(TOP_K):
        hit = eid == idx_ref[kk:kk + 1, :]
        onehot = jnp.where(hit, 1.0, 0.0).astype(BF16)
        cum = _dot(onehot, before)
        rank = jnp.sum(jnp.where(hit, base + cum, 0.0), axis=0, keepdims=True)
        out = jnp.where(orow == kk, rank.astype(I32), out)
        base = base + jnp.sum(jnp.where(hit, 1.0, 0.0), axis=1, keepdims=True)
    rank_ref[...] = out
    carry_scr[...] = jnp.broadcast_to(base, carry_scr.shape)
    cnt_ref[...] = jnp.broadcast_to(base, cnt_ref.shape)


def _rank(idx, *, T):
    tt = MOE_TOK_TILE
    return pl.pallas_call(
        _rank_kernel,
        grid=(T // tt,),
        in_specs=[pl.BlockSpec((8, tt), lambda i: (0, i))],
        out_specs=[pl.BlockSpec((8, tt), lambda i: (0, i)), pl.BlockSpec((N_EXPERTS, 128), lambda i: (0, 0))],
        out_shape=[jax.ShapeDtypeStruct((8, T), I32), jax.ShapeDtypeStruct((N_EXPERTS, 128), F32)],
        scratch_shapes=[pltpu.VMEM((N_EXPERTS, 128), F32)],
        compiler_params=_cparams(sem=("arbitrary",)),
        name="moe_rank",
    )(idx)


def _row_copy(src_ref, src_row, dst_ref, dst_row, sem):
    return pltpu.make_async_copy(src_ref.at[pl.ds(src_row, 1), :], dst_ref.at[pl.ds(dst_row, 1), :], sem)


def _wait_rows(src_ref, dst_ref, sem, count):
    def body(_, carry):
        _row_copy(src_ref, 0, dst_ref, 0, sem).wait()
        return carry
    lax.fori_loop(0, count, body, 0)


def _dispatch_kernel(pstart_ref, cnt_ref, pad_ref, idx_ref, rank_ref, h_ref, xs_ref, zero_scr, sem):
    step = pl.program_id(0)
    tt = idx_ref.shape[1]

    @pl.when(step == 0)
    def _():
        zero_scr[...] = jnp.zeros(zero_scr.shape, F32)

        def per_expert(e, total):
            n_fill = pad_ref[e] - cnt_ref[e]
            first = pstart_ref[e] + cnt_ref[e]

            def fill(r, carry):
                _row_copy(zero_scr, 0, xs_ref, first + r, sem).start()
                return carry
            lax.fori_loop(0, n_fill, fill, 0)
            return total + n_fill
        n_total = lax.fori_loop(0, N_EXPERTS, per_expert, 0)
        _wait_rows(zero_scr, xs_ref, sem, n_total)

    def issue(t, carry):
        for kk in range(TOP_K):
            dst = pstart_ref[idx_ref[kk, t]] + rank_ref[kk, t]
            _row_copy(h_ref, step * tt + t, xs_ref, dst, sem).start()
        return carry
    lax.fori_loop(0, tt, issue, 0)
    _wait_rows(h_ref, xs_ref, sem, TOP_K * tt)


def _dispatch(pstart, cnt, padded, idx, rank, h2p, *, T, P):
    tt = MOE_TOK_TILE
    smem_blk = pl.BlockSpec((8, tt), lambda i, *_: (0, i), memory_space=pltpu.SMEM)
    return pl.pallas_call(
        _dispatch_kernel,
        grid_spec=pltpu.PrefetchScalarGridSpec(
            num_scalar_prefetch=3,
            grid=(T // tt,),
            in_specs=[smem_blk, smem_blk, pl.BlockSpec(memory_space=pl.ANY)],
            out_specs=pl.BlockSpec(memory_space=pl.ANY),
            scratch_shapes=[pltpu.VMEM((8, h2p.shape[1]), F32), pltpu.SemaphoreType.DMA],
        ),
        out_shape=jax.ShapeDtypeStruct((P, h2p.shape[1]), F32),
        compiler_params=_cparams(sem=("arbitrary",)),
        name="moe_dispatch",
    )(pstart, cnt, padded, idx, rank, h2p)


def _collect_kernel(pstart_ref, idx_ref, rank_ref, ys_ref, yk_ref, sem, *, n_tok):
    step = pl.program_id(0)
    tt = idx_ref.shape[1]

    def issue(t, carry):
        for kk in range(TOP_K):
            src = pstart_ref[idx_ref[kk, t]] + rank_ref[kk, t]
            _row_copy(ys_ref, src, yk_ref, kk * n_tok + step * tt + t, sem).start()
        return carry
    lax.fori_loop(0, tt, issue, 0)
    _wait_rows(ys_ref, yk_ref, sem, TOP_K * tt)


def _collect(pstart, idx, rank, ys, *, T):
    tt = MOE_TOK_TILE
    smem_blk = pl.BlockSpec((8, tt), lambda i, *_: (0, i), memory_space=pltpu.SMEM)
    return pl.pallas_call(
        functools.partial(_collect_kernel, n_tok=T),
        grid_spec=pltpu.PrefetchScalarGridSpec(
            num_scalar_prefetch=1,
            grid=(T // tt,),
            in_specs=[smem_blk, smem_blk, pl.BlockSpec(memory_space=pl.ANY)],
            out_specs=pl.BlockSpec(memory_space=pl.ANY),
            scratch_shapes=[pltpu.SemaphoreType.DMA],
        ),
        out_shape=jax.ShapeDtypeStruct((TOP_K * T, ys.shape[1]), F32),
        compiler_params=_cparams(sem=("arbitrary",)),
        name="moe_collect",
    )(pstart, idx, rank, ys)


def _expert_kernel(be_ref, nu_ref, x_ref, w1_ref, b1_ref, w2_ref, b2_ref, y_ref):
    del be_ref

    @pl.when(pl.program_id(0) < nu_ref[0])
    def _():
        x = x_ref[...].astype(BF16)
        u = _dot(x, w1_ref[...]) + b1_ref[...]
        half = u.shape[1] // 2
        glu = jnp.minimum(u[:, :half], SWIGLU_LIMIT)
        lin = jnp.clip(u[:, half:], -SWIGLU_LIMIT, SWIGLU_LIMIT)
        act = glu * jax.nn.sigmoid(SWIGLU_ALPHA * glu) * (lin + 1.0)
        y_ref[...] = _dot(act.astype(BF16), w2_ref[...]) + b2_ref[...]


def _experts(blk_e, n_used, xs, w1, b1, w2, b2, *, P):
    nblk = P // MOE_BLK
    hp = xs.shape[1]
    d, two_f = w1.shape[1:]
    f = w2.shape[1]
    rows = lambda i, be, nu: (jnp.minimum(i, nu[0] - 1), 0)
    return pl.pallas_call(
        _expert_kernel,
        grid_spec=pltpu.PrefetchScalarGridSpec(
            num_scalar_prefetch=2,
            grid=(nblk,),
            in_specs=[
                pl.BlockSpec((MOE_BLK, hp), rows),
                pl.BlockSpec((None, d, two_f), lambda i, be, nu: (be[i], 0, 0)),
                pl.BlockSpec((None, 1, two_f), lambda i, be, nu: (be[i], 0, 0)),
                pl.BlockSpec((None, f, d), lambda i, be, nu: (be[i], 0, 0)),
                pl.BlockSpec((None, 1, d), lambda i, be, nu: (be[i], 0, 0)),
            ],
            out_specs=pl.BlockSpec((MOE_BLK, hp), rows),
        ),
        out_shape=jax.ShapeDtypeStruct((P, hp), F32),
        compiler_params=_cparams(sem=("arbitrary",), vmem=VMEM_LIMIT),
        name="moe_experts",
    )(blk_e, n_used, xs, w1, b1, w2, b2)


def _moe(h2p, idx, w1, b1, w2, b2, *, T):
    rank, cnt = _rank(idx, T=T)
    counts = cnt[:, 0].astype(I32)
    padded = (counts + MOE_BLK - 1) // MOE_BLK * MOE_BLK
    pad_end = jnp.cumsum(padded)
    pstart = (pad_end - padded).astype(I32)
    nblk = TOP_K * T // MOE_BLK + N_EXPERTS
    P = nblk * MOE_BLK
    blk_e = jnp.minimum(jnp.searchsorted(pad_end, jnp.arange(nblk, dtype=I32) * MOE_BLK, side="right"),
                        N_EXPERTS - 1).astype(I32)
    n_used = (pad_end[-1:] // MOE_BLK).astype(I32)
    xs = _dispatch(pstart, counts, padded.astype(I32), idx, rank, h2p, T=T, P=P)
    ys = _experts(blk_e, n_used, xs, w1, b1, w2, b2, P=P)
    return _collect(pstart, idx, rank, ys, T=T)


def _residual_kernel(y0_ref, y1_ref, y2_ref, y3_ref, gate_ref, x_ref, g2_ref, lg_ref, lb_ref, o_ref, *, alpha):
    gates = jnp.concatenate([gate_ref[...], jnp.zeros((128 - 8, gate_ref.shape[1]), F32)], axis=0).T
    f = jnp.zeros(x_ref.shape, F32)
    for kk, y_ref in enumerate((y0_ref, y1_ref, y2_ref, y3_ref)):
        f = f + gates[:, kk:kk + 1] * y_ref[...]
    o_ref[...] = _normalize(alpha * x_ref[...] + g2_ref[...] * f) * lg_ref[...] + lb_ref[...]


def _residual(yk, gates, x1, g2, ln_g, ln_b, *, B, S, n_out_tiles, n_lat_tiles, alpha):
    nt = S // TILE
    T = B * S
    hp = yk.shape[1]
    vec = pl.BlockSpec((1, D_MODEL), lambda b, i: (0, 0))
    yspecs = [pl.BlockSpec((TILE, hp), functools.partial(lambda b, i, kk: (kk * (T // TILE) + b * nt + i, 0), kk=kk))
              for kk in range(TOP_K)]
    return pl.pallas_call(
        functools.partial(_residual_kernel, alpha=alpha),
        grid=(B, n_out_tiles),
        in_specs=yspecs + [
            pl.BlockSpec((8, TILE), lambda b, i: (0, b * nt + i)),
            pl.BlockSpec((None, TILE, D_MODEL), lambda b, i: (b, i, 0)),
            _mod_spec(n_lat_tiles, B), vec, vec,
        ],
        out_specs=pl.BlockSpec((None, TILE, D_MODEL), lambda b, i: (b, i, 0)),
        out_shape=jax.ShapeDtypeStruct((B, n_out_tiles * TILE, D_MODEL), F32),
        compiler_params=_cparams(vmem=VMEM_LIMIT),
        name="ffn_residual_norm",
    )(yk, yk, yk, yk, gates, x1, g2, ln_g, ln_b)


def _rope_tables(n_lat, n_ctx):
    half = HEAD_DIM // 2
    pos = np.arange(n_lat)
    inv = 1.0 / (ROPE_BASE ** (np.arange(0, half, 2, dtype=np.float64) / half))
    ang_row = (pos // GRID_W)[:, None].astype(np.float64) * inv[None, :]
    ang_col = (pos % GRID_W)[:, None].astype(np.float64) * inv[None, :]
    ang = np.concatenate([ang_row, ang_row, ang_col, ang_col], axis=1)
    sign = np.concatenate([-np.ones(16), np.ones(16)] * 2)[None, :]
    cos = np.concatenate([np.cos(ang), np.ones((n_ctx, HEAD_DIM))], axis=0)
    sin = np.concatenate([np.sin(ang) * sign, np.zeros((n_ctx, HEAD_DIM))], axis=0)
    return (jnp.asarray(np.tile(cos, (1, 2)), dtype=F32), jnp.asarray(np.tile(sin, (1, 2)), dtype=F32))


def _extend_w_in(w):
    g = MIX_GROUP_W
    hd = HEAD_DIM
    a, bp, q = w[:, 0:g], w[:, g:2 * g], w[:, 2 * g:3 * g]
    k, v = w[:, 3 * g:3 * g + 2 * hd], w[:, 3 * g + 2 * hd:4 * g]
    rest = w[:, 4 * g:]
    dup = lambda t: jnp.concatenate([t[:, :hd], t[:, :hd], t[:, hd:], t[:, hd:]], axis=1)
    return jnp.concatenate([a, bp, q, dup(k), dup(v), rest], axis=1).astype(BF16)


def _block_diag(pw):
    n, c, _ = pw.shape
    out = jnp.zeros((n * c, n * c), pw.dtype)
    for gi in range(n):
        out = lax.dynamic_update_slice(out, pw[gi], (gi * c, gi * c))
    return out


def kernel(x, c, ctx, c_ctx, w_mod, b_mod, w_in, pool_w, pool_scale, attn_sink, ret_decay, w_out, ln_g, ln_b,
           router_w, router_b, exp_w1, exp_b1, exp_w2, exp_b2):
    B, N, D = x.shape
    L = ctx.shape[1]
    depth = w_mod.shape[0]
    assert D == D_MODEL and L == TILE and N % TILE == 0 and N % (FFT_RADIX2 * 16) == 0 and B < 8
    S = N + L
    n_lat_tiles = N // TILE
    T = B * S
    assert T % MOE_TOK_TILE == 0
    alpha = float((2 * depth) ** 0.25)

    cvec = jnp.zeros((8, D), F32).at[:B].set(c).at[B].set(c_ctx)
    mods = _modulation(cvec, w_mod, b_mod)
    cos, sin = _rope_tables(N, L)

    xs = (x, ctx)
    out = None
    for l in range(depth):
        last = l == depth - 1
        m = mods[l].reshape(8, 6, 1, D)
        sh1, sc1, g1, sh2, sc2, g2 = [m[:, j] for j in range(6)]
        proj = _projection(xs, sc1, sh1, _extend_w_in(w_in[l]), cos, sin, B=B, S=S, n_lat_tiles=n_lat_tiles)
        a, bp, q, k2, v2, rq, rk, rv, rg = proj
        y_f = _fourier_mix(a, B=B, N=N, L=L, with_ctx=not last)
        y_p = _pool_mix(bp, _block_diag(pool_w[l]).astype(BF16), pool_scale[l].reshape(1, -1), B=B, N=N, L=L)
        y_a = _attention(attn_sink[l], q, k2, v2, B=B, N=N, L=L)
        dec = jnp.repeat(ret_decay[l], MIX_GROUP_W // RET_HEADS, axis=1)
        y_r = _retention(dec, rq, rk, rv, rg, B=B, N=N, L=L)
        x1, h2p, idx, gates = _merge(
            xs, (y_f, y_p, y_a, y_r), g1, sc2, sh2, ln_g[l, 0:1], ln_b[l, 0:1], w_out[l].astype(BF16),
            router_w[l].T.astype(BF16), router_b[l].reshape(-1, 1), B=B, S=S, n_lat_tiles=n_lat_tiles, alpha=alpha)
        w1 = jnp.concatenate([exp_w1[l][:, :, 0::2], exp_w1[l][:, :, 1::2]], axis=-1).astype(BF16)
        b1 = jnp.concatenate([exp_b1[l][:, 0::2], exp_b1[l][:, 1::2]], axis=-1)[:, None, :]
        yk = _moe(h2p, idx, w1, b1, exp_w2[l].astype(BF16), exp_b2[l][:, None, :], T=T)
        out = _residual(yk, gates, x1, g2, ln_g[l, 1:2], ln_b[l, 1:2], B=B, S=S,
                        n_out_tiles=n_lat_tiles if last else S // TILE, n_lat_tiles=n_lat_tiles, alpha=alpha)
        xs = (out,)
    return out
```

```python
import functools

import numpy as np
import jax
import jax.numpy as jnp
from jax import lax
from jax.experimental import pallas as pl
from jax.experimental.pallas import tpu as pltpu

F32 = jnp.float32
BF16 = jnp.bfloat16
I32 = jnp.int32

D_MODEL = 1024
GRID_W = 64
FOURIER_GROUP_W = 64
POOL_WINDOWS = (2, 4, 8, 16)
HEAD_DIM = 64
WINDOW = 128
ROPE_BASE = 10000.0
RET_HEADS = 4
RET_CHUNK = 128
N_EXPERTS = 32
TOP_K = 4
SWIGLU_LIMIT = 7.0
SWIGLU_ALPHA = 1.702
LN_EPS = 1e-6
NEG_INF = -1e30
MIX_GROUP_W = 256
FFT_RADIX2 = 64

TILE = 256
MOE_BLK = 256
MOE_TOK_TILE = 512
ROWS_PER_TOKEN = D_MODEL // 128
RUN_ALIGN = 16
RUN_SHIFT = 4
CHUNK_ROWS = RUN_ALIGN * ROWS_PER_TOKEN
STAGE_TOKENS = TOP_K * MOE_TOK_TILE + N_EXPERTS * RUN_ALIGN
VMEM_LIMIT = 48 * 1024 * 1024
EXPERT_VMEM_LIMIT = 56 * 1024 * 1024


def _dot(a, b):
    return jnp.dot(a, b, preferred_element_type=F32)


def _dot_nt(a, b):
    return lax.dot_general(a, b, (((1,), (1,)), ((), ())), preferred_element_type=F32)


def _dot_tn(a, b):
    return lax.dot_general(a, b, (((0,), (0,)), ((), ())), preferred_element_type=F32)


def _normalize(x):
    mu = jnp.mean(x, axis=-1, keepdims=True)
    xc = x - mu
    var = jnp.mean(xc * xc, axis=-1, keepdims=True)
    return xc * lax.rsqrt(var + LN_EPS)


def _cparams(sem=None, vmem=None):
    kw = {}
    if sem is not None:
        kw["dimension_semantics"] = sem
    if vmem is not None:
        kw["vmem_limit_bytes"] = vmem
    return pltpu.CompilerParams(**kw)


def _to_token_tiles(ref, val):
    n = val.shape[0]
    for c in range(ROWS_PER_TOKEN):
        ref[pl.ds(c, n, stride=ROWS_PER_TOKEN), :] = val[:, 128 * c:128 * (c + 1)]


def _from_token_tiles(ref, n):
    return jnp.concatenate([ref[pl.ds(c, n, stride=ROWS_PER_TOKEN), :] for c in range(ROWS_PER_TOKEN)], axis=-1)


def _token_rows(tok, count=1):
    return pl.ds(pl.multiple_of(tok * ROWS_PER_TOKEN, ROWS_PER_TOKEN), count * ROWS_PER_TOKEN)


def _mod_kernel(c_ref, w_ref, b_ref, o_ref):
    c = c_ref[...]
    a = c * jax.nn.sigmoid(c)
    w = w_ref[...]
    a_hi = a.astype(BF16)
    a_lo = (a - a_hi.astype(F32)).astype(BF16)
    w_hi = w.astype(BF16)
    w_lo = (w - w_hi.astype(F32)).astype(BF16)
    o_ref[...] = _dot(a_hi, w_hi) + _dot(a_lo, w_hi) + _dot(a_hi, w_lo) + b_ref[...]


def _modulation(cvec, w_mod, b_mod):
    depth, d, six_d = w_mod.shape
    nb = 1536
    return pl.pallas_call(
        _mod_kernel,
        grid=(depth, six_d // nb),
        in_specs=[
            pl.BlockSpec((8, d), lambda l, j: (0, 0)),
            pl.BlockSpec((None, d, nb), lambda l, j: (l, 0, j)),
            pl.BlockSpec((None, 1, nb), lambda l, j: (l, 0, j)),
        ],
        out_specs=pl.BlockSpec((None, 8, nb), lambda l, j: (l, 0, j)),
        out_shape=jax.ShapeDtypeStruct((depth, 8, six_d), F32),
        compiler_params=_cparams(vmem=VMEM_LIMIT),
        name="modulation",
    )(cvec, w_mod, b_mod.reshape(depth, 1, six_d))


def _proj_kernel(*refs, split_input, n_lat_tiles):
    if split_input:
        x_ref, ctx_ref = refs[:2]
        refs = refs[2:]
        is_ctx = pl.program_id(1) == n_lat_tiles
        x = jnp.where(is_ctx, ctx_ref[...], x_ref[...])
    else:
        x = refs[0][...]
        refs = refs[1:]
    sc_ref, sh_ref, w_ref, cos_ref, sin_ref = refs[:5]
    a_ref, bp_ref, q_ref, k_ref, v_ref, rq_ref, rk_ref, rv_ref, rg_ref = refs[5:]
    h = _normalize(x) * (1.0 + sc_ref[...]) + sh_ref[...]
    p = _dot(h.astype(BF16), w_ref[...])
    g = MIX_GROUP_W
    a_ref[...] = p[:, 0:g].astype(BF16)
    bp_ref[...] = p[:, g:2 * g].astype(BF16)
    qk = p[:, 2 * g:4 * g]
    lane = lax.broadcasted_iota(I32, qk.shape, 1)
    first = (lane & 16) == 0
    partner = jnp.where(first, pltpu.roll(qk, 2 * g - 16, 1), pltpu.roll(qk, 16, 1))
    cos = jnp.concatenate([cos_ref[...]] * 4, axis=-1)
    sin = jnp.concatenate([sin_ref[...]] * 4, axis=-1)
    qk = qk * cos + partner * sin
    q_ref[...] = (qk[:, :g] * (HEAD_DIM ** -0.5)).astype(BF16)
    k_ref[...] = qk[:, g:].astype(BF16)
    v_ref[...] = p[:, 4 * g:5 * g].astype(BF16)
    rq_ref[...] = p[:, 5 * g:6 * g].astype(BF16)
    rk_ref[...] = p[:, 6 * g:7 * g].astype(BF16)
    rv_ref[...] = p[:, 7 * g:8 * g].astype(BF16)
    rg_ref[...] = p[:, 8 * g:9 * g].astype(BF16)


def _mod_spec(n_lat_tiles, ctx_row):
    return pl.BlockSpec((None, 1, D_MODEL), lambda b, i: (jnp.where(i == n_lat_tiles, ctx_row, b), 0, 0))


def _x_specs(split_input, n_lat_tiles):
    if split_input:
        return [
            pl.BlockSpec((None, TILE, D_MODEL), lambda b, i: (b, jnp.minimum(i, n_lat_tiles - 1), 0)),
            pl.BlockSpec((None, TILE, D_MODEL), lambda b, i: (b, 0, 0)),
        ]
    return [pl.BlockSpec((None, TILE, D_MODEL), lambda b, i: (b, i, 0))]


def _projection(xs, sc, sh, w_ext, cos, sin, *, B, S, n_lat_tiles):
    split_input = len(xs) == 2
    nt = S // TILE
    g = MIX_GROUP_W
    out_spec = pl.BlockSpec((None, TILE, g), lambda b, i: (b, i, 0))
    out_shape = jax.ShapeDtypeStruct((B, S, g), BF16)
    return pl.pallas_call(
        functools.partial(_proj_kernel, split_input=split_input, n_lat_tiles=n_lat_tiles),
        grid=(B, nt),
        in_specs=_x_specs(split_input, n_lat_tiles) + [
            _mod_spec(n_lat_tiles, B),
            _mod_spec(n_lat_tiles, B),
            pl.BlockSpec(w_ext.shape, lambda b, i: (0, 0)),
            pl.BlockSpec((TILE, 128), lambda b, i: (i, 0)),
            pl.BlockSpec((TILE, 128), lambda b, i: (i, 0)),
        ],
        out_specs=[out_spec] * 9,
        out_shape=[out_shape] * 9,
        compiler_params=_cparams(vmem=VMEM_LIMIT),
        name="ln_mod_in_proj",
    )(*xs, sc, sh, w_ext, cos, sin)


def _channel_dft_matrix():
    c = np.arange(FOURIER_GROUP_W)
    ang = 2.0 * np.pi * np.outer(c, c) / FOURIER_GROUP_W
    eye = np.eye(MIX_GROUP_W // FOURIER_GROUP_W)
    cs = np.concatenate([np.kron(eye, np.cos(ang)), -np.kron(eye, np.sin(ang))], axis=1)
    return jnp.asarray(cs, dtype=BF16)


def _seq_dft_stage_a(n1):
    n2 = FFT_RADIX2
    n = n1 * n2
    k1 = np.arange(n1)[None, :, None]
    i1 = np.arange(n1)[None, None, :]
    i2 = np.arange(n2)[:, None, None]
    th = 2.0 * np.pi * ((k1 * (n2 * i1 + i2)) % n) / n
    wr, wi = np.cos(th), -np.sin(th)
    wa = np.concatenate([np.concatenate([wr, -wi], axis=2), np.concatenate([wi, wr], axis=2)], axis=1)
    return jnp.asarray(wa, dtype=BF16)


def _seq_dft_stage_b():
    n2 = FFT_RADIX2
    k2 = np.arange(n2)
    ph = 2.0 * np.pi * (np.outer(k2, k2) % n2) / n2
    return jnp.asarray(np.concatenate([np.cos(ph), np.sin(ph)], axis=1), dtype=BF16)


def _dense_dft(length):
    k = np.arange(length)
    th = 2.0 * np.pi * (np.outer(k, k) % length) / length
    return jnp.asarray(np.concatenate([np.cos(th), np.sin(th)], axis=1), dtype=BF16)


def _fft_a_kernel(a_ref, cs_ref, wa_ref, t_ref, *, nb, n1):
    g = MIX_GROUP_W
    for b in range(nb):
        p = _dot(a_ref[b], cs_ref[...])
        pp = jnp.concatenate([p[:, :g], p[:, g:]], axis=0).astype(BF16)
        t = _dot(wa_ref[...], pp)
        t_ref[b] = jnp.concatenate([t[:n1], t[n1:]], axis=-1).astype(BF16)


def _fft_b_kernel(t_ref, vb_ref, y_ref, *, nb, scale):
    g = MIX_GROUP_W
    n2 = FFT_RADIX2
    for b in range(nb):
        t = t_ref[b]
        tt = jnp.concatenate([t[:, :g], t[:, g:]], axis=0)
        y = _dot(vb_ref[...], tt) * scale
        y_ref[b, 0:n2, :] = y.astype(BF16)
        y_ref[b, n2:, :] = jnp.zeros((y_ref.shape[1] - n2, g), BF16)


def _fft_ctx_kernel(a_ref, cs_ref, wc_ref, yin_ref, y_ref, *, scale):
    del yin_ref
    g = MIX_GROUP_W
    p = _dot(a_ref[...], cs_ref[...])
    pp = jnp.concatenate([p[:, :g], p[:, g:]], axis=0).astype(BF16)
    y_ref[...] = (_dot(wc_ref[...], pp) * scale).astype(BF16)


def _fourier_mix(a, *, B, N, L, with_ctx):
    S = N + L
    g = MIX_GROUP_W
    n2 = FFT_RADIX2
    n1 = N // n2
    cs = _channel_dft_matrix()
    a2 = a.reshape(B, S // n2, n2 * g)
    t = pl.pallas_call(
        functools.partial(_fft_a_kernel, nb=B, n1=n1),
        grid=(n2,),
        in_specs=[
            pl.BlockSpec((B, n1, g), lambda j: (0, 0, j)),
            pl.BlockSpec(cs.shape, lambda j: (0, 0)),
            pl.BlockSpec((None, 2 * n1, 2 * n1), lambda j: (j, 0, 0)),
        ],
        out_specs=pl.BlockSpec((B, n1, 2 * g), lambda j: (0, 0, j)),
        out_shape=jax.ShapeDtypeStruct((B, n1, n2 * 2 * g), BF16),
        name="fourier_seq_stage_a",
    )(a2, cs, _seq_dft_stage_a(n1))
    t = t.reshape(B, n1 * n2, 2 * g)
    rows = S // n1
    y = pl.pallas_call(
        functools.partial(_fft_b_kernel, nb=B, scale=float((N * FOURIER_GROUP_W) ** -0.5)),
        grid=(n1,),
        in_specs=[
            pl.BlockSpec((B, n2, 2 * g), lambda k: (0, k, 0)),
            pl.BlockSpec((n2, 2 * n2), lambda k: (0, 0)),
        ],
        out_specs=pl.BlockSpec((B, rows, g), lambda k: (0, 0, k)),
        out_shape=jax.ShapeDtypeStruct((B, rows, n1 * g), BF16),
        name="fourier_seq_stage_b",
    )(t, _seq_dft_stage_b())
    y = y.reshape(B, S, g)
    if not with_ctx:
        return y
    ctx_tile = N // TILE
    return pl.pallas_call(
        functools.partial(_fft_ctx_kernel, scale=float((L * FOURIER_GROUP_W) ** -0.5)),
        grid=(B,),
        in_specs=[
            pl.BlockSpec((None, TILE, g), lambda b: (b, ctx_tile, 0)),
            pl.BlockSpec(cs.shape, lambda b: (0, 0)),
            pl.BlockSpec((L, 2 * L), lambda b: (0, 0)),
            pl.BlockSpec(memory_space=pl.ANY),
        ],
        out_specs=pl.BlockSpec((None, TILE, g), lambda b: (b, ctx_tile, 0)),
        out_shape=jax.ShapeDtypeStruct((B, S, g), BF16),
        input_output_aliases={3: 0},
        name="fourier_ctx",
    )(a, cs, _dense_dft(L), y)


POOL_HALO = 16


def _pool_kernel(prev_ref, x_ref, next_ref, pw_ref, ps_ref, o_ref, *, n_lat, s_tot):
    i = pl.program_id(1)
    xm = x_ref[...]
    xc = jnp.concatenate([prev_ref[...], xm, next_ref[...]], axis=0)
    t0 = i * TILE
    is_ctx = t0 >= n_lat
    r0 = jnp.where(is_ctx, n_lat, 0)
    r1 = jnp.where(is_ctx, s_tot, n_lat)
    kw = TILE + 2 * POOL_HALO
    t_col = t0 + lax.broadcasted_iota(I32, (TILE, 1), 0)
    s_pos = t0 - POOL_HALO + lax.broadcasted_iota(I32, (TILE, kw), 1)
    group = lax.broadcasted_iota(I32, (TILE, MIX_GROUP_W), 1) // FOURIER_GROUP_W
    acc = jnp.zeros((TILE, MIX_GROUP_W), F32)
    for gi, w in enumerate(POOL_WINDOWS):
        lo = jnp.maximum(t_col - w // 2, r0)
        hi = jnp.minimum(t_col + w // 2, r1)
        band = jnp.where(s_pos >= lo, jnp.where(s_pos < hi, 1.0, 0.0), 0.0).astype(BF16)
        mean = _dot(band, xc) / (hi - lo).astype(F32)
        acc = jnp.where(group == gi, mean, acc)
    pooled = acc - xm.astype(F32)
    o_ref[...] = (_dot(pooled.astype(BF16), pw_ref[...]) * ps_ref[...]).astype(BF16)


def _pool_mix(bp, pw_bd, pscale, *, B, N, L):
    S = N + L
    nt = S // TILE
    g = MIX_GROUP_W
    hb = TILE // POOL_HALO
    last = S // POOL_HALO - 1
    return pl.pallas_call(
        functools.partial(_pool_kernel, n_lat=N, s_tot=S),
        grid=(B, nt),
        in_specs=[
            pl.BlockSpec((None, POOL_HALO, g), lambda b, i: (b, jnp.maximum(i * hb - 1, 0), 0)),
            pl.BlockSpec((None, TILE, g), lambda b, i: (b, i, 0)),
            pl.BlockSpec((None, POOL_HALO, g), lambda b, i: (b, jnp.minimum((i + 1) * hb, last), 0)),
            pl.BlockSpec((g, g), lambda b, i: (0, 0)),
            pl.BlockSpec((1, g), lambda b, i: (0, 0)),
        ],
        out_specs=pl.BlockSpec((None, TILE, g), lambda b, i: (b, i, 0)),
        out_shape=jax.ShapeDtypeStruct((B, S, g), BF16),
        name="pool_mix",
    )(bp, bp, bp, pw_bd, pscale)


def _attn_kernel(sink_ref, q_ref, kp_ref, kc_ref, kn_ref, kx_ref, vp_ref, vc_ref, vn_ref, vx_ref, o_ref,
                 *, n_lat):
    i = pl.program_id(1)
    t0 = i * TILE
    nband = TILE + 2 * WINDOW
    nkeys = nband + kx_ref.shape[0]
    row = lax.broadcasted_iota(I32, (2 * TILE, nkeys), 0)
    col = lax.broadcasted_iota(I32, (2 * TILE, nkeys), 1)
    qpos = t0 + jnp.where(row >= TILE, row - TILE, row)
    kpos = t0 - WINDOW + col
    in_band = (jnp.abs(qpos - kpos) <= WINDOW) & (kpos >= 0) & (kpos < n_lat) & (t0 < n_lat)
    valid = (col >= nband) | in_band
    lane = lax.broadcasted_iota(I32, (TILE, 2 * HEAD_DIM), 1)
    low = lane < HEAD_DIM
    keep_low = jnp.where(low, 1.0, 0.0).astype(BF16)
    keep_high = jnp.where(low, 0.0, 1.0).astype(BF16)
    row1 = lax.broadcasted_iota(I32, (2 * TILE, 1), 0)
    outs = []
    for h in range(2):
        sl = slice(2 * HEAD_DIM * h, 2 * HEAD_DIM * (h + 1))
        qh = q_ref[:, sl]
        qs = jnp.concatenate([qh * keep_low, qh * keep_high], axis=0)
        kh = jnp.concatenate([kp_ref[:, sl], kc_ref[:, sl], kn_ref[:, sl], kx_ref[:, sl]], axis=0)
        vh = jnp.concatenate([vp_ref[:, sl], vc_ref[:, sl], vn_ref[:, sl], vx_ref[:, sl]], axis=0)
        s = jnp.where(valid, _dot_nt(qs, kh), NEG_INF)
        sink = jnp.where(row1 < TILE, sink_ref[2 * h], sink_ref[2 * h + 1])
        m = jnp.maximum(jnp.max(s, axis=-1, keepdims=True), sink)
        p = jnp.exp(s - m)
        den = jnp.sum(p, axis=-1, keepdims=True) + jnp.exp(sink - m)
        o = _dot(p.astype(BF16), vh) / den
        outs.append(jnp.where(low, o[:TILE], o[TILE:]))
    o_ref[...] = jnp.concatenate(outs, axis=-1).astype(BF16)


def _attention(sinks, q, k2, v2, *, B, N, L):
    S = N + L
    nt = S // TILE
    g = MIX_GROUP_W
    hb = TILE // WINDOW
    last = N // WINDOW - 1
    ctx_tile = N // TILE
    prev = pl.BlockSpec((None, WINDOW, g), lambda b, i: (b, jnp.clip(i * hb - 1, 0, last), 0))
    cur = pl.BlockSpec((None, TILE, g), lambda b, i: (b, i, 0))
    nxt = pl.BlockSpec((None, WINDOW, g), lambda b, i: (b, jnp.minimum((i + 1) * hb, last), 0))
    ctx = pl.BlockSpec((None, L, g), lambda b, i: (b, ctx_tile, 0))
    return pl.pallas_call(
        functools.partial(_attn_kernel, n_lat=N),
        grid=(B, nt),
        in_specs=[pl.BlockSpec(memory_space=pltpu.SMEM), cur, prev, cur, nxt, ctx, prev, cur, nxt, ctx],
        out_specs=pl.BlockSpec((None, TILE, g), lambda b, i: (b, i, 0)),
        out_shape=jax.ShapeDtypeStruct((B, S, g), BF16),
        compiler_params=_cparams(vmem=VMEM_LIMIT),
        name="banded_attention",
    )(sinks, q, k2, k2, k2, k2, v2, v2, v2, v2)


def _ret_kernel(*refs, reverse, nb):
    if reverse:
        dec_ref, q_ref, k_ref, v_ref, of_ref, g_ref, o_ref, r_scr = refs
    else:
        dec_ref, q_ref, k_ref, v_ref, o_ref, r_scr = refs
    c = RET_CHUNK
    w = MIX_GROUP_W
    dh = w // RET_HEADS

    @pl.when(pl.program_id(0) == 0)
    def _():
        r_scr[...] = jnp.zeros(r_scr.shape, F32)

    lg = -jnp.exp(dec_ref[...])
    idx = lax.broadcasted_iota(I32, (c, 1), 0).astype(F32)
    jdx = lax.broadcasted_iota(I32, (1, c), 1).astype(F32)
    if reverse:
        idx, jdx = (c - 1.0) - idx, (c - 1.0) - jdx
    diff = idx - jdx
    dmask = [jnp.where(diff >= 0, jnp.exp(jnp.maximum(diff, 0.0) * lg[:, dh * h:dh * h + 1]), 0.0)
             for h in range(RET_HEADS)]
    xi = jnp.exp((idx + 1.0) * lg)
    zeta = jnp.exp((c - 1.0 - idx) * lg)
    chunk_decay = jnp.exp(float(c) * lg)
    lane_head = lax.broadcasted_iota(I32, (1, w), 1) // dh
    keep_head = [jnp.where(lane_head == h, 1.0, 0.0).astype(BF16) for h in range(RET_HEADS)]
    same_head = (lax.broadcasted_iota(I32, (w, w), 0) // dh) == (lax.broadcasted_iota(I32, (w, w), 1) // dh)

    for b in range(nb):
        q = q_ref[b]
        k = k_ref[b] * jnp.asarray(dh ** -0.5, BF16)
        v = v_ref[b]
        state = r_scr[b]
        o = _dot(q, state.astype(BF16)) * xi
        for h in range(RET_HEADS):
            mh = lane_head == h
            s = _dot_nt(q * keep_head[h], k) * dmask[h]
            o = o + jnp.where(mh, _dot(s.astype(BF16), v), 0.0)
        kv = _dot_tn(k, (v.astype(F32) * zeta).astype(BF16))
        r_scr[b] = state * chunk_decay + jnp.where(same_head, kv, 0.0)
        if not reverse:
            o_ref[b] = o
        else:
            tot = o + of_ref[b]
            mu = jnp.zeros_like(tot)
            for h in range(RET_HEADS):
                mh = lane_head == h
                mu = jnp.where(mh, jnp.sum(jnp.where(mh, tot, 0.0), axis=-1, keepdims=True) / dh, mu)
            cen = tot - mu
            var = jnp.zeros_like(tot)
            for h in range(RET_HEADS):
                mh = lane_head == h
                var = jnp.where(mh, jnp.sum(jnp.where(mh, cen * cen, 0.0), axis=-1, keepdims=True) / dh, var)
            gate = g_ref[b].astype(F32)
            o_ref[b] = (gate * jax.nn.sigmoid(gate) * (cen * lax.rsqrt(var + LN_EPS))).astype(BF16)


def _retention(dec, rq, rk, rv, rg, *, B, N, L):
    S = N + L
    c = RET_CHUNK
    w = MIX_GROUP_W
    nc_lat, nc_ctx = N // c, L // c
    nc = nc_lat + nc_ctx

    def fwd_blk(j):
        return jnp.where(j < nc_ctx, nc_lat + j, j - nc_ctx)

    def bwd_blk(j):
        return nc - 1 - j

    def call(blk, reverse, extra):
        spec = pl.BlockSpec((B, c, w), lambda j: (0, blk(j), 0))
        dspec = pl.BlockSpec((1, w), lambda j: (0, 0))
        return pl.pallas_call(
            functools.partial(_ret_kernel, reverse=reverse, nb=B),
            grid=(nc,),
            in_specs=[dspec] + [spec] * (3 + len(extra)),
            out_specs=spec,
            out_shape=jax.ShapeDtypeStruct((B, S, w), BF16 if reverse else F32),
            scratch_shapes=[pltpu.VMEM((B, w, w), F32)],
            compiler_params=_cparams(sem=("arbitrary",), vmem=VMEM_LIMIT),
            name="retention_bwd" if reverse else "retention_fwd",
        )(dec[1:2] if reverse else dec[0:1], rq, rk, rv, *extra)

    o_f = call(fwd_blk, False, ())
    return call(bwd_blk, True, (o_f, rg))


def _merge_kernel(*refs, split_input, n_lat_tiles, alpha):
    if split_input:
        x_ref, ctx_ref = refs[:2]
        refs = refs[2:]
        x = jnp.where(pl.program_id(1) == n_lat_tiles, ctx_ref[...], x_ref[...])
    else:
        x = refs[0][...]
        refs = refs[1:]
    (yf_ref, yp_ref, ya_ref, yr_ref, g1_ref, sc_ref, sh_ref, lg_ref, lb_ref, wo_ref, rw_ref, rb_ref,
     x1_ref, h2_ref, idx_ref, gate_ref) = refs
    ycat = jnp.concatenate([yf_ref[...], yp_ref[...], ya_ref[...], yr_ref[...]], axis=-1)
    y = _dot(ycat, wo_ref[...])
    x1 = _normalize(alpha * x + g1_ref[...] * y) * lg_ref[...] + lb_ref[...]
    x1_ref[...] = x1
    h2 = _normalize(x1) * (1.0 + sc_ref[...]) + sh_ref[...]
    _to_token_tiles(h2_ref, h2)
    logits = _dot_nt(rw_ref[...], h2.astype(BF16)) + rb_ref[...]
    eid = lax.broadcasted_iota(I32, logits.shape, 0)
    orow = lax.broadcasted_iota(I32, (8, logits.shape[1]), 0)
    idx_out = jnp.zeros((8, logits.shape[1]), I32)
    val_out = jnp.zeros((8, logits.shape[1]), F32)
    top = None
    den = None
    for kk in range(TOP_K):
        m = jnp.max(logits, axis=0, keepdims=True)
        sel = jnp.min(jnp.where(logits == m, eid, N_EXPERTS), axis=0, keepdims=True)
        logits = jnp.where(eid == sel, -jnp.inf, logits)
        if kk == 0:
            top = m
        e = jnp.exp(m - top)
        den = e if kk == 0 else den + e
        idx_out = jnp.where(orow == kk, sel, idx_out)
        val_out = jnp.where(orow == kk, e, val_out)
    idx_ref[...] = idx_out
    gate_ref[...] = val_out / den


def _merge(xs, ys, g1, sc2, sh2, ln_g, ln_b, w_out, rw_t, rb, *, B, S, n_lat_tiles, alpha):
    split_input = len(xs) == 2
    nt = S // TILE
    g = MIX_GROUP_W
    yspec = pl.BlockSpec((None, TILE, g), lambda b, i: (b, i, 0))
    vec = pl.BlockSpec((1, D_MODEL), lambda b, i: (0, 0))
    tok = lambda b, i: (0, b * nt + i)
    return pl.pallas_call(
        functools.partial(_merge_kernel, split_input=split_input, n_lat_tiles=n_lat_tiles, alpha=alpha),
        grid=(B, nt),
        in_specs=_x_specs(split_input, n_lat_tiles) + [yspec] * 4 + [
            _mod_spec(n_lat_tiles, B), _mod_spec(n_lat_tiles, B), _mod_spec(n_lat_tiles, B), vec, vec,
            pl.BlockSpec((D_MODEL, D_MODEL), lambda b, i: (0, 0)),
            pl.BlockSpec((N_EXPERTS, D_MODEL), lambda b, i: (0, 0)),
            pl.BlockSpec((N_EXPERTS, 1), lambda b, i: (0, 0)),
        ],
        out_specs=[
            pl.BlockSpec((None, TILE, D_MODEL), lambda b, i: (b, i, 0)),
            pl.BlockSpec((TILE * ROWS_PER_TOKEN, 128), lambda b, i: (b * nt + i, 0)),
            pl.BlockSpec((8, TILE), tok),
            pl.BlockSpec((8, TILE), tok),
        ],
        out_shape=[
            jax.ShapeDtypeStruct((B, S, D_MODEL), F32),
            jax.ShapeDtypeStruct((B * S * ROWS_PER_TOKEN, 128), F32),
            jax.ShapeDtypeStruct((8, B * S), I32),
            jax.ShapeDtypeStruct((8, B * S), F32),
        ],
        compiler_params=_cparams(vmem=VMEM_LIMIT),
        name="out_proj_norm_router",
    )(*xs, *ys, g1, sc2, sh2, ln_g, ln_b, w_out, rw_t, rb)


def _rank_kernel(idx_ref, lpos_ref, ntab_ref, cbtab_ref, cnt_ref, carry_scr):
    tt = idx_ref.shape[1]

    @pl.when(pl.program_id(0) == 0)
    def _():
        carry_scr[...] = jnp.zeros(carry_scr.shape, F32)

    eid = lax.broadcasted_iota(I32, (N_EXPERTS, tt), 0)
    before = jnp.where(lax.broadcasted_iota(I32, (tt, tt), 0) < lax.broadcasted_iota(I32, (tt, tt), 1),
                       1.0, 0.0).astype(BF16)
    hits, cums, pres = [], [], []
    n_tile = jnp.zeros((N_EXPERTS, 1), F32)
    for kk in range(TOP_K):
        hit = eid == idx_ref[kk:kk + 1, :]
        hits.append(hit)
        cums.append(_dot(jnp.where(hit, 1.0, 0.0).astype(BF16), before))
        pres.append(n_tile)
        n_tile = n_tile + jnp.sum(jnp.where(hit, 1.0, 0.0), axis=1, keepdims=True)
    n16 = ((n_tile.astype(I32) + (RUN_ALIGN - 1)) >> RUN_SHIFT) << RUN_SHIFT
    lower = jnp.where(lax.broadcasted_iota(I32, (N_EXPERTS, N_EXPERTS), 1)
                      < lax.broadcasted_iota(I32, (N_EXPERTS, N_EXPERTS), 0), 1.0, 0.0).astype(BF16)
    hi = jnp.broadcast_to((n16 >> 8).astype(F32), (N_EXPERTS, 128)).astype(BF16)
    lo = jnp.broadcast_to((n16 & 255).astype(F32), (N_EXPERTS, 128)).astype(BF16)
    lstart = (256.0 * _dot(lower, hi) + _dot(lower, lo))[:, 0:1]
    orow = lax.broadcasted_iota(I32, (8, tt), 0)
    out = jnp.zeros((8, tt), I32)
    for kk in range(TOP_K):
        slot = jnp.sum(jnp.where(hits[kk], lstart + pres[kk] + cums[kk], 0.0), axis=0, keepdims=True)
        out = jnp.where(orow == kk, slot.astype(I32), out)
    lpos_ref[...] = out
    carry = carry_scr[...]
    ntab_ref[...] = jnp.broadcast_to(n_tile, ntab_ref.shape).astype(I32)
    cbtab_ref[...] = carry.astype(I32)
    carry = carry + n_tile
    carry_scr[...] = carry
    cnt_ref[...] = carry


def _rank(idx, *, T):
    tt = MOE_TOK_TILE
    nt = T // tt
    tab = pl.BlockSpec((None, N_EXPERTS, 128), lambda i: (i, 0, 0))
    return pl.pallas_call(
        _rank_kernel,
        grid=(nt,),
        in_specs=[pl.BlockSpec((8, tt), lambda i: (0, i))],
        out_specs=[pl.BlockSpec((8, tt), lambda i: (0, i)), tab, tab,
                   pl.BlockSpec((N_EXPERTS, 128), lambda i: (0, 0))],
        out_shape=[jax.ShapeDtypeStruct((8, T), I32), jax.ShapeDtypeStruct((nt, N_EXPERTS, 128), I32),
                   jax.ShapeDtypeStruct((nt, N_EXPERTS, 128), I32), jax.ShapeDtypeStruct((N_EXPERTS, 128), F32)],
        scratch_shapes=[pltpu.VMEM((N_EXPERTS, 128), F32)],
        compiler_params=_cparams(sem=("arbitrary",)),
        name="moe_rank",
    )(idx)


def _run_chunk(src_ref, src_tok, dst_ref, dst_tok, sem):
    return pltpu.make_async_copy(src_ref.at[_token_rows(src_tok, RUN_ALIGN), :],
                                 dst_ref.at[_token_rows(dst_tok, RUN_ALIGN), :], sem)


def _wait_chunks(src_ref, dst_ref, sem, count):
    def body(_, carry):
        _run_chunk(src_ref, 0, dst_ref, 0, sem).wait()
        return carry
    lax.fori_loop(0, count, body, 0)


def _copy_runs(step, pstart_ref, ntab_ref, cbtab_ref, chunk):
    def per_expert(e, carry):
        src0, total = carry
        nch = (ntab_ref[step * N_EXPERTS + e] + (RUN_ALIGN - 1)) >> RUN_SHIFT
        dst0 = pstart_ref[e] + cbtab_ref[step * N_EXPERTS + e]

        def one(c, cc):
            chunk(src0 + c * RUN_ALIGN, dst0 + c * RUN_ALIGN).start()
            return cc
        lax.fori_loop(0, nch, one, 0)
        return src0 + nch * RUN_ALIGN, total + nch
    return lax.fori_loop(0, N_EXPERTS, per_expert, (0, 0))[1]


def _dispatch_kernel(pstart_ref, cnt_ref, pad_ref, ntab_ref, cbtab_ref, lpos_ref, h_ref, xs_ref,
                     stage, zero_scr, sem):
    step = pl.program_id(0)
    tt = lpos_ref.shape[1]

    @pl.when(step == 0)
    def _():
        zero_scr[...] = jnp.zeros(zero_scr.shape, F32)

        def per_expert(e, total):
            first = pstart_ref[e] + ((cnt_ref[e] >> RUN_SHIFT) << RUN_SHIFT)
            n_fill = (pstart_ref[e] + pad_ref[e] - first) >> RUN_SHIFT

            def fill(c, carry):
                _run_chunk(zero_scr, 0, xs_ref, first + c * RUN_ALIGN, sem).start()
                return carry
            lax.fori_loop(0, n_fill, fill, 0)
            return total + n_fill
        n_total = lax.fori_loop(0, N_EXPERTS, per_expert, 0)
        used = pstart_ref[N_EXPERTS - 1] + pad_ref[N_EXPERTS - 1]
        n_tail = (xs_ref.shape[0] // ROWS_PER_TOKEN - used) >> RUN_SHIFT

        def fill_tail(c, carry):
            _run_chunk(zero_scr, 0, xs_ref, used + c * RUN_ALIGN, sem).start()
            return carry
        lax.fori_loop(0, n_tail, fill_tail, 0)
        _wait_chunks(zero_scr, xs_ref, sem, n_total + n_tail)

    stage[...] = jnp.zeros(stage.shape, F32)
    for kk in range(TOP_K):
        def place(t, carry, kk=kk):
            stage[_token_rows(lpos_ref[kk, t]), :] = h_ref[_token_rows(t), :]
            return carry
        lax.fori_loop(0, tt, place, 0, unroll=8)
    total = _copy_runs(step, pstart_ref, ntab_ref, cbtab_ref,
                       lambda s, d: _run_chunk(stage, s, xs_ref, d, sem))
    _wait_chunks(stage, xs_ref, sem, total)


def _smem_tile_spec(tt):
    return pl.BlockSpec((8, tt), lambda i, *_: (0, i), memory_space=pltpu.SMEM)


def _dispatch(pstart, cnt, padded, ntab, cbtab, lpos, h2t, *, T, P):
    tt = MOE_TOK_TILE
    r = ROWS_PER_TOKEN
    return pl.pallas_call(
        _dispatch_kernel,
        grid_spec=pltpu.PrefetchScalarGridSpec(
            num_scalar_prefetch=5,
            grid=(T // tt,),
            in_specs=[_smem_tile_spec(tt), pl.BlockSpec((tt * r, 128), lambda i, *_: (i, 0))],
            out_specs=pl.BlockSpec(memory_space=pl.ANY),
            scratch_shapes=[pltpu.VMEM((STAGE_TOKENS * r, 128), F32), pltpu.VMEM((CHUNK_ROWS, 128), F32),
                            pltpu.SemaphoreType.DMA],
        ),
        out_shape=jax.ShapeDtypeStruct((P * r, 128), F32),
        compiler_params=_cparams(sem=("arbitrary",), vmem=VMEM_LIMIT),
        name="moe_dispatch",
    )(pstart, cnt, padded, ntab, cbtab, lpos, h2t)


def _collect_kernel(pstart_ref, ntab_ref, cbtab_ref, lpos_ref, gate_ref, ys_ref, f_ref, stage, sem):
    step = pl.program_id(0)
    tt = lpos_ref.shape[1]
    total = _copy_runs(step, pstart_ref, ntab_ref, cbtab_ref,
                       lambda s, d: _run_chunk(ys_ref, d, stage, s, sem))
    _wait_chunks(ys_ref, stage, sem, total)

    def combine(t, carry):
        acc = gate_ref[0, t] * stage[_token_rows(lpos_ref[0, t]), :]
        for kk in range(1, TOP_K):
            acc = acc + gate_ref[kk, t] * stage[_token_rows(lpos_ref[kk, t]), :]
        f_ref[_token_rows(t), :] = acc
        return carry
    lax.fori_loop(0, tt, combine, 0, unroll=4)


def _collect(pstart, ntab, cbtab, lpos, gates, ys, *, T):
    tt = MOE_TOK_TILE
    r = ROWS_PER_TOKEN
    return pl.pallas_call(
        _collect_kernel,
        grid_spec=pltpu.PrefetchScalarGridSpec(
            num_scalar_prefetch=3,
            grid=(T // tt,),
            in_specs=[_smem_tile_spec(tt), _smem_tile_spec(tt), pl.BlockSpec(memory_space=pl.ANY)],
            out_specs=pl.BlockSpec((tt * r, 128), lambda i, *_: (i, 0)),
            scratch_shapes=[pltpu.VMEM((STAGE_TOKENS * r, 128), F32), pltpu.SemaphoreType.DMA],
        ),
        out_shape=jax.ShapeDtypeStruct((T * r, 128), F32),
        compiler_params=_cparams(sem=("arbitrary",), vmem=VMEM_LIMIT),
        name="moe_collect",
    )(pstart, ntab, cbtab, lpos, gates, ys)


def _deinterleave_matrix():
    m = np.zeros((256, 256), np.float32)
    j = np.arange(128)
    m[2 * j, j] = 1.0
    m[2 * j + 1, 128 + j] = 1.0
    return jnp.asarray(m, dtype=BF16)


def _expert_kernel(be_ref, nu_ref, x_ref, w1_ref, b1_ref, w2_ref, b2_ref, perm_ref, y_ref, w1_scr, w2_scr):
    i = pl.program_id(0)

    @pl.when(i >= nu_ref[0])
    def _():
        y_ref[...] = jnp.zeros(y_ref.shape, F32)

    @pl.when(i < nu_ref[0])
    def _():
        @pl.when((i == 0) | (be_ref[i] != be_ref[jnp.maximum(i - 1, 0)]))
        def _():
            half = w1_scr.shape[1] // 2
            for m in range(w1_ref.shape[1] // 256):
                r = _dot(w1_ref[:, 256 * m:256 * (m + 1)].astype(BF16), perm_ref[...])
                w1_scr[:, 128 * m:128 * (m + 1)] = r[:, :128].astype(BF16)
                w1_scr[:, half + 128 * m:half + 128 * (m + 1)] = r[:, 128:].astype(BF16)
            w2_scr[...] = w2_ref[...].astype(BF16)

        x = _from_token_tiles(x_ref, MOE_BLK).astype(BF16)
        u = _dot(x, w1_scr[...]) + b1_ref[...]
        half = u.shape[1] // 2
        glu = jnp.minimum(u[:, :half], SWIGLU_LIMIT)
        lin = jnp.clip(u[:, half:], -SWIGLU_LIMIT, SWIGLU_LIMIT)
        act = glu * jax.nn.sigmoid(SWIGLU_ALPHA * glu) * (lin + 1.0)
        _to_token_tiles(y_ref, _dot(act.astype(BF16), w2_scr[...]) + b2_ref[...])


def _experts(blk_e, n_used, xs, w1, b1, w2, b2, *, P):
    nblk = P // MOE_BLK
    r = ROWS_PER_TOKEN
    d, two_f = w1.shape[1:]
    f = w2.shape[1]
    rows = lambda i, be, nu: (jnp.minimum(i, nu[0] - 1), 0)
    return pl.pallas_call(
        _expert_kernel,
        grid_spec=pltpu.PrefetchScalarGridSpec(
            num_scalar_prefetch=2,
            grid=(nblk,),
            in_specs=[
                pl.BlockSpec((MOE_BLK * r, 128), rows),
                pl.BlockSpec((None, d, two_f), lambda i, be, nu: (be[i], 0, 0)),
                pl.BlockSpec((None, 1, two_f), lambda i, be, nu: (be[i], 0, 0)),
                pl.BlockSpec((None, f, d), lambda i, be, nu: (be[i], 0, 0)),
                pl.BlockSpec((None, 1, d), lambda i, be, nu: (be[i], 0, 0)),
                pl.BlockSpec((256, 256), lambda i, be, nu: (0, 0)),
            ],
            out_specs=pl.BlockSpec((MOE_BLK * r, 128), lambda i, be, nu: (i, 0)),
            scratch_shapes=[pltpu.VMEM((d, two_f), BF16), pltpu.VMEM((f, d), BF16)],
        ),
        out_shape=jax.ShapeDtypeStruct((P * r, 128), F32),
        compiler_params=_cparams(sem=("arbitrary",), vmem=EXPERT_VMEM_LIMIT),
        name="moe_experts",
    )(blk_e, n_used, xs, w1, b1, w2, b2, _deinterleave_matrix())


def _moe(h2t, idx, gates, w1, b1, w2, b2, *, T):
    lpos, ntab, cbtab, cnt = _rank(idx, T=T)
    counts = cnt[:, 0].astype(I32)
    padded = (counts + RUN_ALIGN + MOE_BLK - 1) // MOE_BLK * MOE_BLK
    pad_end = jnp.cumsum(padded)
    pstart = (pad_end - padded).astype(I32)
    nblk = TOP_K * T // MOE_BLK + 2 * N_EXPERTS
    P = nblk * MOE_BLK
    starts = jnp.arange(nblk, dtype=I32) * MOE_BLK
    blk_e = jnp.minimum(jnp.sum((pad_end[None, :] <= starts[:, None]).astype(I32), axis=1), N_EXPERTS - 1)
    n_used = (pad_end[-1:] // MOE_BLK).astype(I32)
    ntab = ntab[:, :, 0].reshape(-1)
    cbtab = cbtab[:, :, 0].reshape(-1)
    xs = _dispatch(pstart, counts, padded.astype(I32), ntab, cbtab, lpos, h2t, T=T, P=P)
    ys = _experts(blk_e.astype(I32), n_used, xs, w1, b1, w2, b2, P=P)
    return _collect(pstart, ntab, cbtab, lpos, gates, ys, T=T)


def _residual_kernel(f_ref, x_ref, g2_ref, lg_ref, lb_ref, o_ref, *, alpha):
    f = _from_token_tiles(f_ref, TILE)
    o_ref[...] = _normalize(alpha * x_ref[...] + g2_ref[...] * f) * lg_ref[...] + lb_ref[...]


def _residual(ft, x1, g2, ln_g, ln_b, *, B, S, n_out_tiles, n_lat_tiles, alpha):
    nt = S // TILE
    vec = pl.BlockSpec((1, D_MODEL), lambda b, i: (0, 0))
    return pl.pallas_call(
        functools.partial(_residual_kernel, alpha=alpha),
        grid=(B, n_out_tiles),
        in_specs=[
            pl.BlockSpec((TILE * ROWS_PER_TOKEN, 128), lambda b, i: (b * nt + i, 0)),
            pl.BlockSpec((None, TILE, D_MODEL), lambda b, i: (b, i, 0)),
            _mod_spec(n_lat_tiles, B), vec, vec,
        ],
        out_specs=pl.BlockSpec((None, TILE, D_MODEL), lambda b, i: (b, i, 0)),
        out_shape=jax.ShapeDtypeStruct((B, n_out_tiles * TILE, D_MODEL), F32),
        compiler_params=_cparams(vmem=VMEM_LIMIT),
        name="ffn_residual_norm",
    )(ft, x1, g2, ln_g, ln_b)


def _rope_tables(n_lat, n_ctx):
    half = HEAD_DIM // 2
    pos = np.arange(n_lat)
    inv = 1.0 / (ROPE_BASE ** (np.arange(0, half, 2, dtype=np.float64) / half))
    ang_row = (pos // GRID_W)[:, None].astype(np.float64) * inv[None, :]
    ang_col = (pos % GRID_W)[:, None].astype(np.float64) * inv[None, :]
    ang = np.concatenate([ang_row, ang_row, ang_col, ang_col], axis=1)
    sign = np.concatenate([-np.ones(16), np.ones(16)] * 2)[None, :]
    cos = np.concatenate([np.cos(ang), np.ones((n_ctx, HEAD_DIM))], axis=0)
    sin = np.concatenate([np.sin(ang) * sign, np.zeros((n_ctx, HEAD_DIM))], axis=0)
    return (jnp.asarray(np.tile(cos, (1, 2)), dtype=F32), jnp.asarray(np.tile(sin, (1, 2)), dtype=F32))


def _extend_w_in(w):
    g = MIX_GROUP_W
    hd = HEAD_DIM
    a, bp, q = w[:, 0:g], w[:, g:2 * g], w[:, 2 * g:3 * g]
    k, v = w[:, 3 * g:3 * g + 2 * hd], w[:, 3 * g + 2 * hd:4 * g]
    rest = w[:, 4 * g:]
    dup = lambda t: jnp.concatenate([t[:, :hd], t[:, :hd], t[:, hd:], t[:, hd:]], axis=1)
    return jnp.concatenate([a, bp, q, dup(k), dup(v), rest], axis=1).astype(BF16)


def _block_diag(pw):
    n, c, _ = pw.shape
    out = jnp.zeros((n * c, n * c), pw.dtype)
    for gi in range(n):
        out = lax.dynamic_update_slice(out, pw[gi], (gi * c, gi * c))
    return out


def kernel(x, c, ctx, c_ctx, w_mod, b_mod, w_in, pool_w, pool_scale, attn_sink, ret_decay, w_out, ln_g, ln_b,
           router_w, router_b, exp_w1, exp_b1, exp_w2, exp_b2):
    B, N, D = x.shape
    L = ctx.shape[1]
    depth = w_mod.shape[0]
    assert D == D_MODEL and L == TILE and N % TILE == 0 and N % (FFT_RADIX2 * 16) == 0 and B < 8
    S = N + L
    n_lat_tiles = N // TILE
    T = B * S
    assert T % MOE_TOK_TILE == 0
    alpha = float((2 * depth) ** 0.25)

    cvec = jnp.zeros((8, D), F32).at[:B].set(c).at[B].set(c_ctx)
    mods = _modulation(cvec, w_mod, b_mod)
    cos, sin = _rope_tables(N, L)

    xs = (x, ctx)
    out = None
    for l in range(depth):
        last = l == depth - 1
        m = mods[l].reshape(8, 6, 1, D)
        sh1, sc1, g1, sh2, sc2, g2 = [m[:, j] for j in range(6)]
        proj = _projection(xs, sc1, sh1, _extend_w_in(w_in[l]), cos, sin, B=B, S=S, n_lat_tiles=n_lat_tiles)
        a, bp, q, k2, v2, rq, rk, rv, rg = proj
        y_f = _fourier_mix(a, B=B, N=N, L=L, with_ctx=not last)
        y_p = _pool_mix(bp, _block_diag(pool_w[l]).astype(BF16), pool_scale[l].reshape(1, -1), B=B, N=N, L=L)
        y_a = _attention(attn_sink[l], q, k2, v2, B=B, N=N, L=L)
        dec = jnp.repeat(ret_decay[l], MIX_GROUP_W // RET_HEADS, axis=1)
        y_r = _retention(dec, rq, rk, rv, rg, B=B, N=N, L=L)
        x1, h2t, idx, gates = _merge(
            xs, (y_f, y_p, y_a, y_r), g1, sc2, sh2, ln_g[l, 0:1], ln_b[l, 0:1], w_out[l].astype(BF16),
            router_w[l].T.astype(BF16), router_b[l].reshape(-1, 1), B=B, S=S, n_lat_tiles=n_lat_tiles, alpha=alpha)
        b1 = jnp.concatenate([exp_b1[l][:, 0::2], exp_b1[l][:, 1::2]], axis=-1)[:, None, :]
        ft = _moe(h2t, idx, gates, exp_w1[l], b1, exp_w2[l], exp_b2[l][:, None, :], T=T)
        out = _residual(ft, x1, g2, ln_g[l, 1:2], ln_b[l, 1:2], B=B, S=S,
                        n_out_tiles=n_lat_tiles if last else S // TILE, n_lat_tiles=n_lat_tiles, alpha=alpha)
        xs = (out,)
    return out
```

```python
import functools

import numpy as np
import jax
import jax.numpy as jnp
from jax import lax
from jax.experimental import pallas as pl
from jax.experimental.pallas import tpu as pltpu

F32 = jnp.float32
BF16 = jnp.bfloat16
I32 = jnp.int32

D_MODEL = 1024
GRID_W = 64
FOURIER_GROUP_W = 64
POOL_WINDOWS = (2, 4, 8, 16)
HEAD_DIM = 64
WINDOW = 128
ROPE_BASE = 10000.0
RET_HEADS = 4
RET_CHUNK = 128
N_EXPERTS = 32
TOP_K = 4
SWIGLU_LIMIT = 7.0
SWIGLU_ALPHA = 1.702
LN_EPS = 1e-6
NEG_INF = -1e30
MIX_GROUP_W = 256
FFT_RADIX2 = 64

TILE = 256
MOE_BLK = 256
MOE_TOK_TILE = 512
ROWS_PER_TOKEN = D_MODEL // 128
RUN_ALIGN = 16
RUN_SHIFT = 4
CHUNK_ROWS = RUN_ALIGN * ROWS_PER_TOKEN
STAGE_TOKENS = TOP_K * MOE_TOK_TILE + N_EXPERTS * RUN_ALIGN
VMEM_LIMIT = 48 * 1024 * 1024
EXPERT_VMEM_LIMIT = 56 * 1024 * 1024


def _dot(a, b):
    return jnp.dot(a, b, preferred_element_type=F32)


def _dot_nt(a, b):
    return lax.dot_general(a, b, (((1,), (1,)), ((), ())), preferred_element_type=F32)


def _dot_tn(a, b):
    return lax.dot_general(a, b, (((0,), (0,)), ((), ())), preferred_element_type=F32)


def _normalize(x):
    mu = jnp.mean(x, axis=-1, keepdims=True)
    xc = x - mu
    var = jnp.mean(xc * xc, axis=-1, keepdims=True)
    return xc * lax.rsqrt(var + LN_EPS)


def _cparams(sem=None, vmem=None):
    kw = {}
    if sem is not None:
        kw["dimension_semantics"] = sem
    if vmem is not None:
        kw["vmem_limit_bytes"] = vmem
    return pltpu.CompilerParams(**kw)


def _to_token_tiles(ref, val):
    n = val.shape[0]
    for c in range(ROWS_PER_TOKEN):
        ref[pl.ds(c, n, stride=ROWS_PER_TOKEN), :] = val[:, 128 * c:128 * (c + 1)]


def _from_token_tiles(ref, n):
    return jnp.concatenate([ref[pl.ds(c, n, stride=ROWS_PER_TOKEN), :] for c in range(ROWS_PER_TOKEN)], axis=-1)


def _token_rows(tok, count=1):
    return pl.ds(pl.multiple_of(tok * ROWS_PER_TOKEN, ROWS_PER_TOKEN), count * ROWS_PER_TOKEN)


def _mod_kernel(c_ref, w_ref, b_ref, o_ref):
    c = c_ref[...]
    a = c * jax.nn.sigmoid(c)
    w = w_ref[...]
    a_hi = a.astype(BF16)
    a_lo = (a - a_hi.astype(F32)).astype(BF16)
    w_hi = w.astype(BF16)
    w_lo = (w - w_hi.astype(F32)).astype(BF16)
    o_ref[...] = _dot(a_hi, w_hi) + _dot(a_lo, w_hi) + _dot(a_hi, w_lo) + b_ref[...]


def _modulation(cvec, w_mod, b_mod):
    depth, d, six_d = w_mod.shape
    nb = 1536
    return pl.pallas_call(
        _mod_kernel,
        grid=(depth, six_d // nb),
        in_specs=[
            pl.BlockSpec((8, d), lambda l, j: (0, 0)),
            pl.BlockSpec((None, d, nb), lambda l, j: (l, 0, j)),
            pl.BlockSpec((None, 1, nb), lambda l, j: (l, 0, j)),
        ],
        out_specs=pl.BlockSpec((None, 8, nb), lambda l, j: (l, 0, j)),
        out_shape=jax.ShapeDtypeStruct((depth, 8, six_d), F32),
        compiler_params=_cparams(vmem=VMEM_LIMIT),
        name="modulation",
    )(cvec, w_mod, b_mod.reshape(depth, 1, six_d))


def _proj_kernel(*refs, split_input, n_lat_tiles):
    if split_input:
        x_ref, ctx_ref = refs[:2]
        refs = refs[2:]
        is_ctx = pl.program_id(1) == n_lat_tiles
        x = jnp.where(is_ctx, ctx_ref[...], x_ref[...])
    else:
        x = refs[0][...]
        refs = refs[1:]
    sc_ref, sh_ref, w_ref, cos_ref, sin_ref = refs[:5]
    a_ref, bp_ref, q_ref, k_ref, v_ref, rq_ref, rk_ref, rv_ref, rg_ref = refs[5:]
    h = _normalize(x) * (1.0 + sc_ref[...]) + sh_ref[...]
    p = _dot(h.astype(BF16), w_ref[...])
    g = MIX_GROUP_W
    a_ref[...] = p[:, 0:g].astype(BF16)
    bp_ref[...] = p[:, g:2 * g].astype(BF16)
    qk = p[:, 2 * g:4 * g]
    lane = lax.broadcasted_iota(I32, qk.shape, 1)
    first = (lane & 16) == 0
    partner = jnp.where(first, pltpu.roll(qk, 2 * g - 16, 1), pltpu.roll(qk, 16, 1))
    cos = jnp.concatenate([cos_ref[...]] * 4, axis=-1)
    sin = jnp.concatenate([sin_ref[...]] * 4, axis=-1)
    qk = qk * cos + partner * sin
    q_ref[...] = (qk[:, :g] * (HEAD_DIM ** -0.5)).astype(BF16)
    k_ref[...] = qk[:, g:].astype(BF16)
    v_ref[...] = p[:, 4 * g:5 * g].astype(BF16)
    rq_ref[...] = p[:, 5 * g:6 * g].astype(BF16)
    rk_ref[...] = p[:, 6 * g:7 * g].astype(BF16)
    rv_ref[...] = p[:, 7 * g:8 * g].astype(BF16)
    rg_ref[...] = p[:, 8 * g:9 * g].astype(BF16)


def _mod_spec(n_lat_tiles, ctx_row):
    return pl.BlockSpec((None, 1, D_MODEL), lambda b, i: (jnp.where(i == n_lat_tiles, ctx_row, b), 0, 0))


def _x_specs(split_input, n_lat_tiles):
    if split_input:
        return [
            pl.BlockSpec((None, TILE, D_MODEL), lambda b, i: (b, jnp.minimum(i, n_lat_tiles - 1), 0)),
            pl.BlockSpec((None, TILE, D_MODEL), lambda b, i: (b, 0, 0)),
        ]
    return [pl.BlockSpec((None, TILE, D_MODEL), lambda b, i: (b, i, 0))]


def _projection(xs, sc, sh, w_ext, cos, sin, *, B, S, n_lat_tiles):
    split_input = len(xs) == 2
    nt = S // TILE
    g = MIX_GROUP_W
    out_spec = pl.BlockSpec((None, TILE, g), lambda b, i: (b, i, 0))
    out_shape = jax.ShapeDtypeStruct((B, S, g), BF16)
    return pl.pallas_call(
        functools.partial(_proj_kernel, split_input=split_input, n_lat_tiles=n_lat_tiles),
        grid=(B, nt),
        in_specs=_x_specs(split_input, n_lat_tiles) + [
            _mod_spec(n_lat_tiles, B),
            _mod_spec(n_lat_tiles, B),
            pl.BlockSpec(w_ext.shape, lambda b, i: (0, 0)),
            pl.BlockSpec((TILE, 128), lambda b, i: (i, 0)),
            pl.BlockSpec((TILE, 128), lambda b, i: (i, 0)),
        ],
        out_specs=[out_spec] * 9,
        out_shape=[out_shape] * 9,
        compiler_params=_cparams(vmem=VMEM_LIMIT),
        name="ln_mod_in_proj",
    )(*xs, sc, sh, w_ext, cos, sin)


def _channel_dft_matrix():
    c = np.arange(FOURIER_GROUP_W)
    ang = 2.0 * np.pi * np.outer(c, c) / FOURIER_GROUP_W
    eye = np.eye(MIX_GROUP_W // FOURIER_GROUP_W)
    cs = np.concatenate([np.kron(eye, np.cos(ang)), -np.kron(eye, np.sin(ang))], axis=1)
    return jnp.asarray(cs, dtype=BF16)


def _seq_dft_stage_a(n1):
    n2 = FFT_RADIX2
    n = n1 * n2
    k1 = np.arange(n1)[None, :, None]
    i1 = np.arange(n1)[None, None, :]
    i2 = np.arange(n2)[:, None, None]
    th = 2.0 * np.pi * ((k1 * (n2 * i1 + i2)) % n) / n
    wr, wi = np.cos(th), -np.sin(th)
    wa = np.concatenate([np.concatenate([wr, -wi], axis=2), np.concatenate([wi, wr], axis=2)], axis=1)
    return jnp.asarray(wa, dtype=BF16)


def _seq_dft_stage_b():
    n2 = FFT_RADIX2
    k2 = np.arange(n2)
    ph = 2.0 * np.pi * (np.outer(k2, k2) % n2) / n2
    return jnp.asarray(np.concatenate([np.cos(ph), np.sin(ph)], axis=1), dtype=BF16)


def _dense_dft(length):
    k = np.arange(length)
    th = 2.0 * np.pi * (np.outer(k, k) % length) / length
    return jnp.asarray(np.concatenate([np.cos(th), np.sin(th)], axis=1), dtype=BF16)


def _fft_a_kernel(a_ref, cs_ref, wa_ref, t_ref, *, nb, n1):
    g = MIX_GROUP_W
    for b in range(nb):
        p = _dot(a_ref[b], cs_ref[...])
        pp = jnp.concatenate([p[:, :g], p[:, g:]], axis=0).astype(BF16)
        t = _dot(wa_ref[...], pp)
        t_ref[b] = jnp.concatenate([t[:n1], t[n1:]], axis=-1).astype(BF16)


def _fft_b_kernel(t_ref, vb_ref, y_ref, *, nb, scale):
    g = MIX_GROUP_W
    n2 = FFT_RADIX2
    for b in range(nb):
        t = t_ref[b]
        tt = jnp.concatenate([t[:, :g], t[:, g:]], axis=0)
        y = _dot(vb_ref[...], tt) * scale
        y_ref[b, 0:n2, :] = y.astype(BF16)
        y_ref[b, n2:, :] = jnp.zeros((y_ref.shape[1] - n2, g), BF16)


def _fft_ctx_kernel(a_ref, cs_ref, wc_ref, yin_ref, y_ref, *, scale):
    del yin_ref
    g = MIX_GROUP_W
    p = _dot(a_ref[...], cs_ref[...])
    pp = jnp.concatenate([p[:, :g], p[:, g:]], axis=0).astype(BF16)
    y_ref[...] = (_dot(wc_ref[...], pp) * scale).astype(BF16)


def _fourier_mix(a, *, B, N, L, with_ctx):
    S = N + L
    g = MIX_GROUP_W
    n2 = FFT_RADIX2
    n1 = N // n2
    cs = _channel_dft_matrix()
    a2 = a.reshape(B, S // n2, n2 * g)
    t = pl.pallas_call(
        functools.partial(_fft_a_kernel, nb=B, n1=n1),
        grid=(n2,),
        in_specs=[
            pl.BlockSpec((B, n1, g), lambda j: (0, 0, j)),
            pl.BlockSpec(cs.shape, lambda j: (0, 0)),
            pl.BlockSpec((None, 2 * n1, 2 * n1), lambda j: (j, 0, 0)),
        ],
        out_specs=pl.BlockSpec((B, n1, 2 * g), lambda j: (0, 0, j)),
        out_shape=jax.ShapeDtypeStruct((B, n1, n2 * 2 * g), BF16),
        name="fourier_seq_stage_a",
    )(a2, cs, _seq_dft_stage_a(n1))
    t = t.reshape(B, n1 * n2, 2 * g)
    rows = S // n1
    y = pl.pallas_call(
        functools.partial(_fft_b_kernel, nb=B, scale=float((N * FOURIER_GROUP_W) ** -0.5)),
        grid=(n1,),
        in_specs=[
            pl.BlockSpec((B, n2, 2 * g), lambda k: (0, k, 0)),
            pl.BlockSpec((n2, 2 * n2), lambda k: (0, 0)),
        ],
        out_specs=pl.BlockSpec((B, rows, g), lambda k: (0, 0, k)),
        out_shape=jax.ShapeDtypeStruct((B, rows, n1 * g), BF16),
        name="fourier_seq_stage_b",
    )(t, _seq_dft_stage_b())
    y = y.reshape(B, S, g)
    if not with_ctx:
        return y
    ctx_tile = N // TILE
    return pl.pallas_call(
        functools.partial(_fft_ctx_kernel, scale=float((L * FOURIER_GROUP_W) ** -0.5)),
        grid=(B,),
        in_specs=[
            pl.BlockSpec((None, TILE, g), lambda b: (b, ctx_tile, 0)),
            pl.BlockSpec(cs.shape, lambda b: (0, 0)),
            pl.BlockSpec((L, 2 * L), lambda b: (0, 0)),
            pl.BlockSpec(memory_space=pl.ANY),
        ],
        out_specs=pl.BlockSpec((None, TILE, g), lambda b: (b, ctx_tile, 0)),
        out_shape=jax.ShapeDtypeStruct((B, S, g), BF16),
        input_output_aliases={3: 0},
        name="fourier_ctx",
    )(a, cs, _dense_dft(L), y)


POOL_HALO = 16


def _pool_kernel(prev_ref, x_ref, next_ref, pw_ref, ps_ref, o_ref, *, n_lat, s_tot):
    i = pl.program_id(1)
    xm = x_ref[...]
    xc = jnp.concatenate([prev_ref[...], xm, next_ref[...]], axis=0)
    t0 = i * TILE
    is_ctx = t0 >= n_lat
    r0 = jnp.where(is_ctx, n_lat, 0)
    r1 = jnp.where(is_ctx, s_tot, n_lat)
    kw = TILE + 2 * POOL_HALO
    t_col = t0 + lax.broadcasted_iota(I32, (TILE, 1), 0)
    s_pos = t0 - POOL_HALO + lax.broadcasted_iota(I32, (TILE, kw), 1)
    group = lax.broadcasted_iota(I32, (TILE, MIX_GROUP_W), 1) // FOURIER_GROUP_W
    acc = jnp.zeros((TILE, MIX_GROUP_W), F32)
    for gi, w in enumerate(POOL_WINDOWS):
        lo = jnp.maximum(t_col - w // 2, r0)
        hi = jnp.minimum(t_col + w // 2, r1)
        band = jnp.where(s_pos >= lo, jnp.where(s_pos < hi, 1.0, 0.0), 0.0).astype(BF16)
        mean = _dot(band, xc) / (hi - lo).astype(F32)
        acc = jnp.where(group == gi, mean, acc)
    pooled = acc - xm.astype(F32)
    o_ref[...] = (_dot(pooled.astype(BF16), pw_ref[...]) * ps_ref[...]).astype(BF16)


def _pool_mix(bp, pw_bd, pscale, *, B, N, L):
    S = N + L
    nt = S // TILE
    g = MIX_GROUP_W
    hb = TILE // POOL_HALO
    last = S // POOL_HALO - 1
    return pl.pallas_call(
        functools.partial(_pool_kernel, n_lat=N, s_tot=S),
        grid=(B, nt),
        in_specs=[
            pl.BlockSpec((None, POOL_HALO, g), lambda b, i: (b, jnp.maximum(i * hb - 1, 0), 0)),
            pl.BlockSpec((None, TILE, g), lambda b, i: (b, i, 0)),
            pl.BlockSpec((None, POOL_HALO, g), lambda b, i: (b, jnp.minimum((i + 1) * hb, last), 0)),
            pl.BlockSpec((g, g), lambda b, i: (0, 0)),
            pl.BlockSpec((1, g), lambda b, i: (0, 0)),
        ],
        out_specs=pl.BlockSpec((None, TILE, g), lambda b, i: (b, i, 0)),
        out_shape=jax.ShapeDtypeStruct((B, S, g), BF16),
        name="pool_mix",
    )(bp, bp, bp, pw_bd, pscale)


def _attn_kernel(sink_ref, q_ref, kp_ref, kc_ref, kn_ref, kx_ref, vp_ref, vc_ref, vn_ref, vx_ref, o_ref,
                 *, n_lat):
    i = pl.program_id(1)
    t0 = i * TILE
    nband = TILE + 2 * WINDOW
    nkeys = nband + kx_ref.shape[0]
    row = lax.broadcasted_iota(I32, (2 * TILE, nkeys), 0)
    col = lax.broadcasted_iota(I32, (2 * TILE, nkeys), 1)
    qpos = t0 + jnp.where(row >= TILE, row - TILE, row)
    kpos = t0 - WINDOW + col
    in_band = (jnp.abs(qpos - kpos) <= WINDOW) & (kpos >= 0) & (kpos < n_lat) & (t0 < n_lat)
    valid = (col >= nband) | in_band
    lane = lax.broadcasted_iota(I32, (TILE, 2 * HEAD_DIM), 1)
    low = lane < HEAD_DIM
    keep_low = jnp.where(low, 1.0, 0.0).astype(BF16)
    keep_high = jnp.where(low, 0.0, 1.0).astype(BF16)
    row1 = lax.broadcasted_iota(I32, (2 * TILE, 1), 0)
    outs = []
    for h in range(2):
        sl = slice(2 * HEAD_DIM * h, 2 * HEAD_DIM * (h + 1))
        qh = q_ref[:, sl]
        qs = jnp.concatenate([qh * keep_low, qh * keep_high], axis=0)
        kh = jnp.concatenate([kp_ref[:, sl], kc_ref[:, sl], kn_ref[:, sl], kx_ref[:, sl]], axis=0)
        vh = jnp.concatenate([vp_ref[:, sl], vc_ref[:, sl], vn_ref[:, sl], vx_ref[:, sl]], axis=0)
        s = jnp.where(valid, _dot_nt(qs, kh), NEG_INF)
        sink = jnp.where(row1 < TILE, sink_ref[2 * h], sink_ref[2 * h + 1])
        m = jnp.maximum(jnp.max(s, axis=-1, keepdims=True), sink)
        p = jnp.exp(s - m)
        den = jnp.sum(p, axis=-1, keepdims=True) + jnp.exp(sink - m)
        o = _dot(p.astype(BF16), vh) / den
        outs.append(jnp.where(low, o[:TILE], o[TILE:]))
    o_ref[...] = jnp.concatenate(outs, axis=-1).astype(BF16)


def _attention(sinks, q, k2, v2, *, B, N, L):
    S = N + L
    nt = S // TILE
    g = MIX_GROUP_W
    hb = TILE // WINDOW
    last = N // WINDOW - 1
    ctx_tile = N // TILE
    prev = pl.BlockSpec((None, WINDOW, g), lambda b, i: (b, jnp.clip(i * hb - 1, 0, last), 0))
    cur = pl.BlockSpec((None, TILE, g), lambda b, i: (b, i, 0))
    nxt = pl.BlockSpec((None, WINDOW, g), lambda b, i: (b, jnp.minimum((i + 1) * hb, last), 0))
    ctx = pl.BlockSpec((None, L, g), lambda b, i: (b, ctx_tile, 0))
    return pl.pallas_call(
        functools.partial(_attn_kernel, n_lat=N),
        grid=(B, nt),
        in_specs=[pl.BlockSpec(memory_space=pltpu.SMEM), cur, prev, cur, nxt, ctx, prev, cur, nxt, ctx],
        out_specs=pl.BlockSpec((None, TILE, g), lambda b, i: (b, i, 0)),
        out_shape=jax.ShapeDtypeStruct((B, S, g), BF16),
        compiler_params=_cparams(vmem=VMEM_LIMIT),
        name="banded_attention",
    )(sinks, q, k2, k2, k2, k2, v2, v2, v2, v2)


def _ret_kernel(*refs, reverse, nb):
    if reverse:
        dec_ref, q_ref, k_ref, v_ref, of_ref, g_ref, o_ref, r_scr = refs
    else:
        dec_ref, q_ref, k_ref, v_ref, o_ref, r_scr = refs
    c = RET_CHUNK
    w = MIX_GROUP_W
    dh = w // RET_HEADS

    @pl.when(pl.program_id(0) == 0)
    def _():
        r_scr[...] = jnp.zeros(r_scr.shape, F32)

    lg = -jnp.exp(dec_ref[...])
    idx = lax.broadcasted_iota(I32, (c, 1), 0).astype(F32)
    jdx = lax.broadcasted_iota(I32, (1, c), 1).astype(F32)
    if reverse:
        idx, jdx = (c - 1.0) - idx, (c - 1.0) - jdx
    diff = idx - jdx
    dmask = [jnp.where(diff >= 0, jnp.exp(jnp.maximum(diff, 0.0) * lg[:, dh * h:dh * h + 1]), 0.0)
             for h in range(RET_HEADS)]
    xi = jnp.exp((idx + 1.0) * lg)
    zeta = jnp.exp((c - 1.0 - idx) * lg)
    chunk_decay = jnp.exp(float(c) * lg)
    lane_head = lax.broadcasted_iota(I32, (1, w), 1) // dh
    keep_head = [jnp.where(lane_head == h, 1.0, 0.0).astype(BF16) for h in range(RET_HEADS)]
    same_head = (lax.broadcasted_iota(I32, (w, w), 0) // dh) == (lax.broadcasted_iota(I32, (w, w), 1) // dh)

    for b in range(nb):
        q = q_ref[b]
        k = k_ref[b] * jnp.asarray(dh ** -0.5, BF16)
        v = v_ref[b]
        state = r_scr[b]
        o = _dot(q, state.astype(BF16)) * xi
        for h in range(RET_HEADS):
            mh = lane_head == h
            s = _dot_nt(q * keep_head[h], k) * dmask[h]
            o = o + jnp.where(mh, _dot(s.astype(BF16), v), 0.0)
        kv = _dot_tn(k, (v.astype(F32) * zeta).astype(BF16))
        r_scr[b] = state * chunk_decay + jnp.where(same_head, kv, 0.0)
        if not reverse:
            o_ref[b] = o
        else:
            tot = o + of_ref[b]
            mu = jnp.zeros_like(tot)
            for h in range(RET_HEADS):
                mh = lane_head == h
                mu = jnp.where(mh, jnp.sum(jnp.where(mh, tot, 0.0), axis=-1, keepdims=True) / dh, mu)
            cen = tot - mu
            var = jnp.zeros_like(tot)
            for h in range(RET_HEADS):
                mh = lane_head == h
                var = jnp.where(mh, jnp.sum(jnp.where(mh, cen * cen, 0.0), axis=-1, keepdims=True) / dh, var)
            gate = g_ref[b].astype(F32)
            o_ref[b] = (gate * jax.nn.sigmoid(gate) * (cen * lax.rsqrt(var + LN_EPS))).astype(BF16)


def _retention(dec, rq, rk, rv, rg, *, B, N, L):
    S = N + L
    c = RET_CHUNK
    w = MIX_GROUP_W
    nc_lat, nc_ctx = N // c, L // c
    nc = nc_lat + nc_ctx

    def fwd_blk(j):
        return jnp.where(j < nc_ctx, nc_lat + j, j - nc_ctx)

    def bwd_blk(j):
        return nc - 1 - j

    def call(blk, reverse, extra):
        spec = pl.BlockSpec((B, c, w), lambda j: (0, blk(j), 0))
        dspec = pl.BlockSpec((1, w), lambda j: (0, 0))
        return pl.pallas_call(
            functools.partial(_ret_kernel, reverse=reverse, nb=B),
            grid=(nc,),
            in_specs=[dspec] + [spec] * (3 + len(extra)),
            out_specs=spec,
            out_shape=jax.ShapeDtypeStruct((B, S, w), BF16 if reverse else F32),
            scratch_shapes=[pltpu.VMEM((B, w, w), F32)],
            compiler_params=_cparams(sem=("arbitrary",), vmem=VMEM_LIMIT),
            name="retention_bwd" if reverse else "retention_fwd",
        )(dec[1:2] if reverse else dec[0:1], rq, rk, rv, *extra)

    o_f = call(fwd_blk, False, ())
    return call(bwd_blk, True, (o_f, rg))


def _merge_kernel(*refs, split_input, n_lat_tiles, alpha):
    if split_input:
        x_ref, ctx_ref = refs[:2]
        refs = refs[2:]
        x = jnp.where(pl.program_id(1) == n_lat_tiles, ctx_ref[...], x_ref[...])
    else:
        x = refs[0][...]
        refs = refs[1:]
    (yf_ref, yp_ref, ya_ref, yr_ref, g1_ref, sc_ref, sh_ref, lg_ref, lb_ref, wo_ref, rw_ref, rb_ref,
     x1_ref, h2_ref, idx_ref, gate_ref) = refs
    ycat = jnp.concatenate([yf_ref[...], yp_ref[...], ya_ref[...], yr_ref[...]], axis=-1)
    y = _dot(ycat, wo_ref[...])
    x1 = _normalize(alpha * x + g1_ref[...] * y) * lg_ref[...] + lb_ref[...]
    x1_ref[...] = x1
    h2 = _normalize(x1) * (1.0 + sc_ref[...]) + sh_ref[...]
    _to_token_tiles(h2_ref, h2)
    logits = _dot_nt(rw_ref[...], h2.astype(BF16)) + rb_ref[...]
    eid = lax.broadcasted_iota(I32, logits.shape, 0)
    orow = lax.broadcasted_iota(I32, (8, logits.shape[1]), 0)
    idx_out = jnp.zeros((8, logits.shape[1]), I32)
    val_out = jnp.zeros((8, logits.shape[1]), F32)
    top = None
    den = None
    for kk in range(TOP_K):
        m = jnp.max(logits, axis=0, keepdims=True)
        sel = jnp.min(jnp.where(logits == m, eid, N_EXPERTS), axis=0, keepdims=True)
        logits = jnp.where(eid == sel, -jnp.inf, logits)
        if kk == 0:
            top = m
        e = jnp.exp(m - top)
        den = e if kk == 0 else den + e
        idx_out = jnp.where(orow == kk, sel, idx_out)
        val_out = jnp.where(orow == kk, e, val_out)
    idx_ref[...] = idx_out
    gate_ref[...] = val_out / den


def _merge(xs, ys, g1, sc2, sh2, ln_g, ln_b, w_out, rw_t, rb, *, B, S, n_lat_tiles, alpha):
    split_input = len(xs) == 2
    nt = S // TILE
    g = MIX_GROUP_W
    yspec = pl.BlockSpec((None, TILE, g), lambda b, i: (b, i, 0))
    vec = pl.BlockSpec((1, D_MODEL), lambda b, i: (0, 0))
    tok = lambda b, i: (0, b * nt + i)
    return pl.pallas_call(
        functools.partial(_merge_kernel, split_input=split_input, n_lat_tiles=n_lat_tiles, alpha=alpha),
        grid=(B, nt),
        in_specs=_x_specs(split_input, n_lat_tiles) + [yspec] * 4 + [
            _mod_spec(n_lat_tiles, B), _mod_spec(n_lat_tiles, B), _mod_spec(n_lat_tiles, B), vec, vec,
            pl.BlockSpec((D_MODEL, D_MODEL), lambda b, i: (0, 0)),
            pl.BlockSpec((N_EXPERTS, D_MODEL), lambda b, i: (0, 0)),
            pl.BlockSpec((N_EXPERTS, 1), lambda b, i: (0, 0)),
        ],
        out_specs=[
            pl.BlockSpec((None, TILE, D_MODEL), lambda b, i: (b, i, 0)),
            pl.BlockSpec((TILE * ROWS_PER_TOKEN, 128), lambda b, i: (b * nt + i, 0)),
            pl.BlockSpec((8, TILE), tok),
            pl.BlockSpec((8, TILE), tok),
        ],
        out_shape=[
            jax.ShapeDtypeStruct((B, S, D_MODEL), F32),
            jax.ShapeDtypeStruct((B * S * ROWS_PER_TOKEN, 128), F32),
            jax.ShapeDtypeStruct((8, B * S), I32),
            jax.ShapeDtypeStruct((8, B * S), F32),
        ],
        compiler_params=_cparams(vmem=VMEM_LIMIT),
        name="out_proj_norm_router",
    )(*xs, *ys, g1, sc2, sh2, ln_g, ln_b, w_out, rw_t, rb)


def _rank_kernel(idx_ref, lpos_ref, ntab_ref, cbtab_ref, cnt_ref, carry_scr):
    tt = idx_ref.shape[1]

    @pl.when(pl.program_id(0) == 0)
    def _():
        carry_scr[...] = jnp.zeros(carry_scr.shape, F32)

    eid = lax.broadcasted_iota(I32, (N_EXPERTS, tt), 0)
    before = jnp.where(lax.broadcasted_iota(I32, (tt, tt), 0) < lax.broadcasted_iota(I32, (tt, tt), 1),
                       1.0, 0.0).astype(BF16)
    hits, cums, pres = [], [], []
    n_tile = jnp.zeros((N_EXPERTS, 1), F32)
    for kk in range(TOP_K):
        hit = eid == idx_ref[kk:kk + 1, :]
        hits.append(hit)
        cums.append(_dot(jnp.where(hit, 1.0, 0.0).astype(BF16), before))
        pres.append(n_tile)
        n_tile = n_tile + jnp.sum(jnp.where(hit, 1.0, 0.0), axis=1, keepdims=True)
    n16 = ((n_tile.astype(I32) + (RUN_ALIGN - 1)) >> RUN_SHIFT) << RUN_SHIFT
    lower = jnp.where(lax.broadcasted_iota(I32, (N_EXPERTS, N_EXPERTS), 1)
                      < lax.broadcasted_iota(I32, (N_EXPERTS, N_EXPERTS), 0), 1.0, 0.0).astype(BF16)
    hi = jnp.broadcast_to((n16 >> 8).astype(F32), (N_EXPERTS, 128)).astype(BF16)
    lo = jnp.broadcast_to((n16 & 255).astype(F32), (N_EXPERTS, 128)).astype(BF16)
    lstart = (256.0 * _dot(lower, hi) + _dot(lower, lo))[:, 0:1]
    orow = lax.broadcasted_iota(I32, (8, tt), 0)
    out = jnp.zeros((8, tt), I32)
    for kk in range(TOP_K):
        slot = jnp.sum(jnp.where(hits[kk], lstart + pres[kk] + cums[kk], 0.0), axis=0, keepdims=True)
        out = jnp.where(orow == kk, slot.astype(I32), out)
    lpos_ref[...] = out
    carry = carry_scr[...]
    ntab_ref[...] = jnp.broadcast_to(n_tile, ntab_ref.shape).astype(I32)
    cbtab_ref[...] = carry.astype(I32)
    carry = carry + n_tile
    carry_scr[...] = carry
    cnt_ref[...] = carry


def _rank(idx, *, T):
    tt = MOE_TOK_TILE
    nt = T // tt
    tab = pl.BlockSpec((None, N_EXPERTS, 128), lambda i: (i, 0, 0))
    return pl.pallas_call(
        _rank_kernel,
        grid=(nt,),
        in_specs=[pl.BlockSpec((8, tt), lambda i: (0, i))],
        out_specs=[pl.BlockSpec((8, tt), lambda i: (0, i)), tab, tab,
                   pl.BlockSpec((N_EXPERTS, 128), lambda i: (0, 0))],
        out_shape=[jax.ShapeDtypeStruct((8, T), I32), jax.ShapeDtypeStruct((nt, N_EXPERTS, 128), I32),
                   jax.ShapeDtypeStruct((nt, N_EXPERTS, 128), I32), jax.ShapeDtypeStruct((N_EXPERTS, 128), F32)],
        scratch_shapes=[pltpu.VMEM((N_EXPERTS, 128), F32)],
        compiler_params=_cparams(sem=("arbitrary",)),
        name="moe_rank",
    )(idx)


def _run_chunk(src_ref, src_tok, dst_ref, dst_tok, sem):
    return pltpu.make_async_copy(src_ref.at[_token_rows(src_tok, RUN_ALIGN), :],
                                 dst_ref.at[_token_rows(dst_tok, RUN_ALIGN), :], sem)


def _wait_chunks(src_ref, dst_ref, sem, count):
    def body(_, carry):
        _run_chunk(src_ref, 0, dst_ref, 0, sem).wait()
        return carry
    lax.fori_loop(0, count, body, 0)


def _copy_runs(step, pstart_ref, ntab_ref, cbtab_ref, chunk):
    def per_expert(e, carry):
        src0, total = carry
        nch = (ntab_ref[step * N_EXPERTS + e] + (RUN_ALIGN - 1)) >> RUN_SHIFT
        dst0 = pstart_ref[e] + cbtab_ref[step * N_EXPERTS + e]

        def one(c, cc):
            chunk(src0 + c * RUN_ALIGN, dst0 + c * RUN_ALIGN).start()
            return cc
        lax.fori_loop(0, nch, one, 0)
        return src0 + nch * RUN_ALIGN, total + nch
    return lax.fori_loop(0, N_EXPERTS, per_expert, (0, 0))[1]


def _dispatch_kernel(pstart_ref, cnt_ref, pad_ref, ntab_ref, cbtab_ref, lpos_ref, h_ref, xs_ref,
                     stage, zero_scr, sems, sent):
    step = pl.program_id(0)
    tt = lpos_ref.shape[1]

    @pl.when(step == 0)
    def _():
        sem = sems.at[0]
        zero_scr[...] = jnp.zeros(zero_scr.shape, F32)

        def per_expert(e, total):
            first = pstart_ref[e] + ((cnt_ref[e] >> RUN_SHIFT) << RUN_SHIFT)
            n_fill = (pstart_ref[e] + pad_ref[e] - first) >> RUN_SHIFT

            def fill(c, carry):
                _run_chunk(zero_scr, 0, xs_ref, first + c * RUN_ALIGN, sem).start()
                return carry
            lax.fori_loop(0, n_fill, fill, 0)
            return total + n_fill
        n_total = lax.fori_loop(0, N_EXPERTS, per_expert, 0)
        used = pstart_ref[N_EXPERTS - 1] + pad_ref[N_EXPERTS - 1]
        n_tail = (xs_ref.shape[0] // ROWS_PER_TOKEN - used) >> RUN_SHIFT

        def fill_tail(c, carry):
            _run_chunk(zero_scr, 0, xs_ref, used + c * RUN_ALIGN, sem).start()
            return carry
        lax.fori_loop(0, n_tail, fill_tail, 0)
        _wait_chunks(zero_scr, xs_ref, sem, n_total + n_tail)
        sent[0] = 0
        sent[1] = 0

    slot = step % 2
    buf = stage.at[slot]
    sem = sems.at[slot]

    def zero_run_tail(e, src0):
        n = ntab_ref[step * N_EXPERTS + e]
        n_up = ((n + (RUN_ALIGN - 1)) >> RUN_SHIFT) << RUN_SHIFT

        def zero_one(r, carry):
            buf[_token_rows(src0 + r), :] = jnp.zeros((ROWS_PER_TOKEN, 128), F32)
            return carry
        lax.fori_loop(n, n_up, zero_one, 0)
        return src0 + n_up
    lax.fori_loop(0, N_EXPERTS, zero_run_tail, 0)
    for kk in range(TOP_K):
        def place(t, carry, kk=kk):
            buf[_token_rows(lpos_ref[kk, t]), :] = h_ref[_token_rows(t), :]
            return carry
        lax.fori_loop(0, tt, place, 0, unroll=8)
    _wait_chunks(stage.at[1 - slot], xs_ref, sems.at[1 - slot], sent[1 - slot])
    sent[1 - slot] = 0
    sent[slot] = _copy_runs(step, pstart_ref, ntab_ref, cbtab_ref,
                            lambda s, d: _run_chunk(buf, s, xs_ref, d, sem))

    @pl.when(step == pl.num_programs(0) - 1)
    def _():
        _wait_chunks(buf, xs_ref, sem, sent[slot])


def _smem_tile_spec(tt):
    return pl.BlockSpec((8, tt), lambda i, *_: (0, i), memory_space=pltpu.SMEM)


def _dispatch(pstart, cnt, padded, ntab, cbtab, lpos, h2t, *, T, P):
    tt = MOE_TOK_TILE
    r = ROWS_PER_TOKEN
    return pl.pallas_call(
        _dispatch_kernel,
        grid_spec=pltpu.PrefetchScalarGridSpec(
            num_scalar_prefetch=5,
            grid=(T // tt,),
            in_specs=[_smem_tile_spec(tt), pl.BlockSpec((tt * r, 128), lambda i, *_: (i, 0))],
            out_specs=pl.BlockSpec(memory_space=pl.ANY),
            scratch_shapes=[pltpu.VMEM((2, STAGE_TOKENS * r, 128), F32), pltpu.VMEM((CHUNK_ROWS, 128), F32),
                            pltpu.SemaphoreType.DMA((2,)), pltpu.SMEM((2,), I32)],
        ),
        out_shape=jax.ShapeDtypeStruct((P * r, 128), F32),
        compiler_params=_cparams(sem=("arbitrary",), vmem=VMEM_LIMIT),
        name="moe_dispatch",
    )(pstart, cnt, padded, ntab, cbtab, lpos, h2t)


def _collect_kernel(pstart_ref, ntab_ref, cbtab_ref, lpos_ref, gate_ref, ys_ref, f_ref, stage, sems, sent):
    step = pl.program_id(0)
    tt = lpos_ref.shape[1]
    slot = step % 2

    def fetch(tile, b):
        sent[b] = _copy_runs(tile, pstart_ref, ntab_ref, cbtab_ref,
                             lambda s, d: _run_chunk(ys_ref, d, stage.at[b], s, sems.at[b]))

    @pl.when(step == 0)
    def _():
        fetch(0, 0)

    @pl.when(step + 1 < pl.num_programs(0))
    def _():
        fetch(step + 1, 1 - slot)

    buf = stage.at[slot]
    _wait_chunks(ys_ref, buf, sems.at[slot], sent[slot])

    def combine(t, carry):
        acc = gate_ref[0, t] * buf[_token_rows(lpos_ref[0, t]), :]
        for kk in range(1, TOP_K):
            acc = acc + gate_ref[kk, t] * buf[_token_rows(lpos_ref[kk, t]), :]
        f_ref[_token_rows(t), :] = acc
        return carry
    lax.fori_loop(0, tt, combine, 0, unroll=4)


def _collect(pstart, ntab, cbtab, lpos, gates, ys, *, T):
    tt = MOE_TOK_TILE
    r = ROWS_PER_TOKEN
    return pl.pallas_call(
        _collect_kernel,
        grid_spec=pltpu.PrefetchScalarGridSpec(
            num_scalar_prefetch=3,
            grid=(T // tt,),
            in_specs=[_smem_tile_spec(tt), _smem_tile_spec(tt), pl.BlockSpec(memory_space=pl.ANY)],
            out_specs=pl.BlockSpec((tt * r, 128), lambda i, *_: (i, 0)),
            scratch_shapes=[pltpu.VMEM((2, STAGE_TOKENS * r, 128), F32), pltpu.SemaphoreType.DMA((2,)),
                            pltpu.SMEM((2,), I32)],
        ),
        out_shape=jax.ShapeDtypeStruct((T * r, 128), F32),
        compiler_params=_cparams(sem=("arbitrary",), vmem=VMEM_LIMIT),
        name="moe_collect",
    )(pstart, ntab, cbtab, lpos, gates, ys)


def _deinterleave_matrix():
    m = np.zeros((256, 256), np.float32)
    j = np.arange(128)
    m[2 * j, j] = 1.0
    m[2 * j + 1, 128 + j] = 1.0
    return jnp.asarray(m, dtype=BF16)


def _expert_kernel(be_ref, nu_ref, x_ref, w1_ref, b1_ref, w2_ref, b2_ref, perm_ref, y_ref, w1_scr, w2_scr):
    i = pl.program_id(0)

    @pl.when(i >= nu_ref[0])
    def _():
        y_ref[...] = jnp.zeros(y_ref.shape, F32)

    @pl.when(i < nu_ref[0])
    def _():
        @pl.when((i == 0) | (be_ref[i] != be_ref[jnp.maximum(i - 1, 0)]))
        def _():
            half = w1_scr.shape[1] // 2
            for m in range(w1_ref.shape[1] // 256):
                r = _dot(w1_ref[:, 256 * m:256 * (m + 1)].astype(BF16), perm_ref[...])
                w1_scr[:, 128 * m:128 * (m + 1)] = r[:, :128].astype(BF16)
                w1_scr[:, half + 128 * m:half + 128 * (m + 1)] = r[:, 128:].astype(BF16)
            w2_scr[...] = w2_ref[...].astype(BF16)

        x = _from_token_tiles(x_ref, MOE_BLK).astype(BF16)
        u = _dot(x, w1_scr[...]) + b1_ref[...]
        half = u.shape[1] // 2
        glu = jnp.minimum(u[:, :half], SWIGLU_LIMIT)
        lin = jnp.clip(u[:, half:], -SWIGLU_LIMIT, SWIGLU_LIMIT)
        act = glu * jax.nn.sigmoid(SWIGLU_ALPHA * glu) * (lin + 1.0)
        _to_token_tiles(y_ref, _dot(act.astype(BF16), w2_scr[...]) + b2_ref[...])


def _experts(blk_e, n_used, xs, w1, b1, w2, b2, *, P, layer):
    nblk = P // MOE_BLK
    r = ROWS_PER_TOKEN
    d, two_f = w1.shape[2:]
    f = w2.shape[2]
    rows = lambda i, be, nu: (jnp.minimum(i, nu[0] - 1), 0)
    return pl.pallas_call(
        _expert_kernel,
        grid_spec=pltpu.PrefetchScalarGridSpec(
            num_scalar_prefetch=2,
            grid=(nblk,),
            in_specs=[
                pl.BlockSpec((MOE_BLK * r, 128), rows),
                pl.BlockSpec((None, None, d, two_f), lambda i, be, nu: (layer, be[i], 0, 0)),
                pl.BlockSpec((None, 1, two_f), lambda i, be, nu: (be[i], 0, 0)),
                pl.BlockSpec((None, None, f, d), lambda i, be, nu: (layer, be[i], 0, 0)),
                pl.BlockSpec((None, 1, d), lambda i, be, nu: (be[i], 0, 0)),
                pl.BlockSpec((256, 256), lambda i, be, nu: (0, 0)),
            ],
            out_specs=pl.BlockSpec((MOE_BLK * r, 128), lambda i, be, nu: (i, 0)),
            scratch_shapes=[pltpu.VMEM((d, two_f), BF16), pltpu.VMEM((f, d), BF16)],
        ),
        out_shape=jax.ShapeDtypeStruct((P * r, 128), F32),
        compiler_params=_cparams(sem=("arbitrary",), vmem=EXPERT_VMEM_LIMIT),
        name="moe_experts",
    )(blk_e, n_used, xs, w1, b1, w2, b2, _deinterleave_matrix())


def _moe(h2t, idx, gates, w1, b1, w2, b2, *, T, layer):
    lpos, ntab, cbtab, cnt = _rank(idx, T=T)
    counts = cnt[:, 0].astype(I32)
    padded = (counts + RUN_ALIGN + MOE_BLK - 1) // MOE_BLK * MOE_BLK
    pad_end = jnp.cumsum(padded)
    pstart = (pad_end - padded).astype(I32)
    nblk = TOP_K * T // MOE_BLK + 2 * N_EXPERTS
    P = nblk * MOE_BLK
    starts = jnp.arange(nblk, dtype=I32) * MOE_BLK
    blk_e = jnp.minimum(jnp.sum((pad_end[None, :] <= starts[:, None]).astype(I32), axis=1), N_EXPERTS - 1)
    n_used = (pad_end[-1:] // MOE_BLK).astype(I32)
    ntab = ntab[:, :, 0].reshape(-1)
    cbtab = cbtab[:, :, 0].reshape(-1)
    xs = _dispatch(pstart, counts, padded.astype(I32), ntab, cbtab, lpos, h2t, T=T, P=P)
    ys = _experts(blk_e.astype(I32), n_used, xs, w1, b1, w2, b2, P=P, layer=layer)
    return _collect(pstart, ntab, cbtab, lpos, gates, ys, T=T)


def _residual_kernel(f_ref, x_ref, g2_ref, lg_ref, lb_ref, o_ref, *, alpha):
    f = _from_token_tiles(f_ref, TILE)
    o_ref[...] = _normalize(alpha * x_ref[...] + g2_ref[...] * f) * lg_ref[...] + lb_ref[...]


def _residual(ft, x1, g2, ln_g, ln_b, *, B, S, n_out_tiles, n_lat_tiles, alpha):
    nt = S // TILE
    vec = pl.BlockSpec((1, D_MODEL), lambda b, i: (0, 0))
    return pl.pallas_call(
        functools.partial(_residual_kernel, alpha=alpha),
        grid=(B, n_out_tiles),
        in_specs=[
            pl.BlockSpec((TILE * ROWS_PER_TOKEN, 128), lambda b, i: (b * nt + i, 0)),
            pl.BlockSpec((None, TILE, D_MODEL), lambda b, i: (b, i, 0)),
            _mod_spec(n_lat_tiles, B), vec, vec,
        ],
        out_specs=pl.BlockSpec((None, TILE, D_MODEL), lambda b, i: (b, i, 0)),
        out_shape=jax.ShapeDtypeStruct((B, n_out_tiles * TILE, D_MODEL), F32),
        compiler_params=_cparams(vmem=VMEM_LIMIT),
        name="ffn_residual_norm",
    )(ft, x1, g2, ln_g, ln_b)


def _rope_tables(n_lat, n_ctx):
    half = HEAD_DIM // 2
    pos = np.arange(n_lat)
    inv = 1.0 / (ROPE_BASE ** (np.arange(0, half, 2, dtype=np.float64) / half))
    ang_row = (pos // GRID_W)[:, None].astype(np.float64) * inv[None, :]
    ang_col = (pos % GRID_W)[:, None].astype(np.float64) * inv[None, :]
    ang = np.concatenate([ang_row, ang_row, ang_col, ang_col], axis=1)
    sign = np.concatenate([-np.ones(16), np.ones(16)] * 2)[None, :]
    cos = np.concatenate([np.cos(ang), np.ones((n_ctx, HEAD_DIM))], axis=0)
    sin = np.concatenate([np.sin(ang) * sign, np.zeros((n_ctx, HEAD_DIM))], axis=0)
    return (jnp.asarray(np.tile(cos, (1, 2)), dtype=F32), jnp.asarray(np.tile(sin, (1, 2)), dtype=F32))


def _extend_w_in(w):
    g = MIX_GROUP_W
    hd = HEAD_DIM
    a, bp, q = w[:, 0:g], w[:, g:2 * g], w[:, 2 * g:3 * g]
    k, v = w[:, 3 * g:3 * g + 2 * hd], w[:, 3 * g + 2 * hd:4 * g]
    rest = w[:, 4 * g:]
    dup = lambda t: jnp.concatenate([t[:, :hd], t[:, :hd], t[:, hd:], t[:, hd:]], axis=1)
    return jnp.concatenate([a, bp, q, dup(k), dup(v), rest], axis=1).astype(BF16)


def _block_diag(pw):
    n, c, _ = pw.shape
    out = jnp.zeros((n * c, n * c), pw.dtype)
    for gi in range(n):
        out = lax.dynamic_update_slice(out, pw[gi], (gi * c, gi * c))
    return out


def kernel(x, c, ctx, c_ctx, w_mod, b_mod, w_in, pool_w, pool_scale, attn_sink, ret_decay, w_out, ln_g, ln_b,
           router_w, router_b, exp_w1, exp_b1, exp_w2, exp_b2):
    B, N, D = x.shape
    L = ctx.shape[1]
    depth = w_mod.shape[0]
    assert D == D_MODEL and L == TILE and N % TILE == 0 and N % (FFT_RADIX2 * 16) == 0 and B < 8
    S = N + L
    n_lat_tiles = N // TILE
    T = B * S
    assert T % MOE_TOK_TILE == 0
    alpha = float((2 * depth) ** 0.25)

    cvec = jnp.zeros((8, D), F32).at[:B].set(c).at[B].set(c_ctx)
    mods = _modulation(cvec, w_mod, b_mod)
    cos, sin = _rope_tables(N, L)

    xs = (x, ctx)
    out = None
    for l in range(depth):
        last = l == depth - 1
        m = mods[l].reshape(8, 6, 1, D)
        sh1, sc1, g1, sh2, sc2, g2 = [m[:, j] for j in range(6)]
        proj = _projection(xs, sc1, sh1, _extend_w_in(w_in[l]), cos, sin, B=B, S=S, n_lat_tiles=n_lat_tiles)
        a, bp, q, k2, v2, rq, rk, rv, rg = proj
        y_f = _fourier_mix(a, B=B, N=N, L=L, with_ctx=not last)
        y_p = _pool_mix(bp, _block_diag(pool_w[l]).astype(BF16), pool_scale[l].reshape(1, -1), B=B, N=N, L=L)
        y_a = _attention(attn_sink[l], q, k2, v2, B=B, N=N, L=L)
        dec = jnp.repeat(ret_decay[l], MIX_GROUP_W // RET_HEADS, axis=1)
        y_r = _retention(dec, rq, rk, rv, rg, B=B, N=N, L=L)
        x1, h2t, idx, gates = _merge(
            xs, (y_f, y_p, y_a, y_r), g1, sc2, sh2, ln_g[l, 0:1], ln_b[l, 0:1], w_out[l].astype(BF16),
            router_w[l].T.astype(BF16), router_b[l].reshape(-1, 1), B=B, S=S, n_lat_tiles=n_lat_tiles, alpha=alpha)
        b1 = jnp.concatenate([exp_b1[l][:, 0::2], exp_b1[l][:, 1::2]], axis=-1)[:, None, :]
        ft = _moe(h2t, idx, gates, exp_w1, b1, exp_w2, exp_b2[l][:, None, :], T=T, layer=l)
        out = _residual(ft, x1, g2, ln_g[l, 1:2], ln_b[l, 1:2], B=B, S=S,
                        n_out_tiles=n_lat_tiles if last else S // TILE, n_lat_tiles=n_lat_tiles, alpha=alpha)
        xs = (out,)
    return out
```

```python
import functools

import numpy as np
import jax
import jax.numpy as jnp
from jax import lax
from jax.experimental import pallas as pl
from jax.experimental.pallas import tpu as pltpu

F32 = jnp.float32
BF16 = jnp.bfloat16
I32 = jnp.int32

D_MODEL = 1024
GRID_W = 64
FOURIER_GROUP_W = 64
POOL_WINDOWS = (2, 4, 8, 16)
HEAD_DIM = 64
WINDOW = 128
ROPE_BASE = 10000.0
RET_HEADS = 4
RET_CHUNK = 128
N_EXPERTS = 32
TOP_K = 4
SWIGLU_LIMIT = 7.0
SWIGLU_ALPHA = 1.702
LN_EPS = 1e-6
NEG_INF = -1e30
MIX_GROUP_W = 256
FFT_RADIX2 = 64

TILE = 256
MOE_BLK = 256
MOE_TOK_TILE = 512
ROWS_PER_TOKEN = D_MODEL // 128
RUN_ALIGN = 16
RUN_SHIFT = 4
CHUNK_ROWS = RUN_ALIGN * ROWS_PER_TOKEN
STAGE_TOKENS = TOP_K * MOE_TOK_TILE + N_EXPERTS * RUN_ALIGN
VMEM_LIMIT = 48 * 1024 * 1024
EXPERT_VMEM_LIMIT = 56 * 1024 * 1024


def _dot(a, b):
    return jnp.dot(a, b, preferred_element_type=F32)


def _dot_nt(a, b):
    return lax.dot_general(a, b, (((1,), (1,)), ((), ())), preferred_element_type=F32)


def _dot_tn(a, b):
    return lax.dot_general(a, b, (((0,), (0,)), ((), ())), preferred_element_type=F32)


def _normalize(x):
    mu = jnp.mean(x, axis=-1, keepdims=True)
    xc = x - mu
    var = jnp.mean(xc * xc, axis=-1, keepdims=True)
    return xc * lax.rsqrt(var + LN_EPS)


def _cparams(sem=None, vmem=None):
    kw = {}
    if sem is not None:
        kw["dimension_semantics"] = sem
    if vmem is not None:
        kw["vmem_limit_bytes"] = vmem
    return pltpu.CompilerParams(**kw)


def _to_token_tiles(ref, val):
    n = val.shape[0]
    for c in range(ROWS_PER_TOKEN):
        ref[pl.ds(c, n, stride=ROWS_PER_TOKEN), :] = val[:, 128 * c:128 * (c + 1)]


def _from_token_tiles(ref, n):
    return jnp.concatenate([ref[pl.ds(c, n, stride=ROWS_PER_TOKEN), :] for c in range(ROWS_PER_TOKEN)], axis=-1)


def _token_rows(tok, count=1):
    return pl.ds(pl.multiple_of(tok * ROWS_PER_TOKEN, ROWS_PER_TOKEN), count * ROWS_PER_TOKEN)


def _mod_kernel(c_ref, w_ref, b_ref, o_ref):
    c = c_ref[...]
    a = c * jax.nn.sigmoid(c)
    w = w_ref[...]
    a_hi = a.astype(BF16)
    a_lo = (a - a_hi.astype(F32)).astype(BF16)
    w_hi = w.astype(BF16)
    w_lo = (w - w_hi.astype(F32)).astype(BF16)
    o_ref[...] = _dot(a_hi, w_hi) + _dot(a_lo, w_hi) + _dot(a_hi, w_lo) + b_ref[...]


def _modulation(cvec, w_mod, b_mod):
    depth, d, six_d = w_mod.shape
    nb = 1536
    return pl.pallas_call(
        _mod_kernel,
        grid=(depth, six_d // nb),
        in_specs=[
            pl.BlockSpec((8, d), lambda l, j: (0, 0)),
            pl.BlockSpec((None, d, nb), lambda l, j: (l, 0, j)),
            pl.BlockSpec((None, 1, nb), lambda l, j: (l, 0, j)),
        ],
        out_specs=pl.BlockSpec((None, 8, nb), lambda l, j: (l, 0, j)),
        out_shape=jax.ShapeDtypeStruct((depth, 8, six_d), F32),
        compiler_params=_cparams(vmem=VMEM_LIMIT),
        name="modulation",
    )(cvec, w_mod, b_mod.reshape(depth, 1, six_d))


def _proj_kernel(*refs, split_input, n_lat_tiles):
    if split_input:
        x_ref, ctx_ref = refs[:2]
        refs = refs[2:]
        is_ctx = pl.program_id(1) == n_lat_tiles
        x = jnp.where(is_ctx, ctx_ref[...], x_ref[...])
    else:
        x = refs[0][...]
        refs = refs[1:]
    sc_ref, sh_ref, w_ref, cos_ref, sin_ref = refs[:5]
    a_ref, bp_ref, q_ref, k_ref, v_ref, rq_ref, rk_ref, rv_ref, rg_ref = refs[5:]
    h = _normalize(x) * (1.0 + sc_ref[...]) + sh_ref[...]
    p = _dot(h.astype(BF16), w_ref[...])
    g = MIX_GROUP_W
    a_ref[...] = p[:, 0:g].astype(BF16)
    bp_ref[...] = p[:, g:2 * g].astype(BF16)
    qk = p[:, 2 * g:4 * g]
    lane = lax.broadcasted_iota(I32, qk.shape, 1)
    first = (lane & 16) == 0
    partner = jnp.where(first, pltpu.roll(qk, 2 * g - 16, 1), pltpu.roll(qk, 16, 1))
    cos = jnp.concatenate([cos_ref[...]] * 4, axis=-1)
    sin = jnp.concatenate([sin_ref[...]] * 4, axis=-1)
    qk = qk * cos + partner * sin
    q_ref[...] = (qk[:, :g] * (HEAD_DIM ** -0.5)).astype(BF16)
    k_ref[...] = qk[:, g:].astype(BF16)
    v_ref[...] = p[:, 4 * g:5 * g].astype(BF16)
    rq_ref[...] = p[:, 5 * g:6 * g].astype(BF16)
    rk_ref[...] = p[:, 6 * g:7 * g].astype(BF16)
    rv_ref[...] = p[:, 7 * g:8 * g].astype(BF16)
    rg_ref[...] = p[:, 8 * g:9 * g].astype(BF16)


def _mod_spec(n_lat_tiles, ctx_row):
    return pl.BlockSpec((None, 1, D_MODEL), lambda b, i: (jnp.where(i == n_lat_tiles, ctx_row, b), 0, 0))


def _x_specs(split_input, n_lat_tiles):
    if split_input:
        return [
            pl.BlockSpec((None, TILE, D_MODEL), lambda b, i: (b, jnp.minimum(i, n_lat_tiles - 1), 0)),
            pl.BlockSpec((None, TILE, D_MODEL), lambda b, i: (b, 0, 0)),
        ]
    return [pl.BlockSpec((None, TILE, D_MODEL), lambda b, i: (b, i, 0))]


def _projection(xs, sc, sh, w_ext, cos, sin, *, B, S, n_lat_tiles):
    split_input = len(xs) == 2
    nt = S // TILE
    g = MIX_GROUP_W
    out_spec = pl.BlockSpec((None, TILE, g), lambda b, i: (b, i, 0))
    out_shape = jax.ShapeDtypeStruct((B, S, g), BF16)
    return pl.pallas_call(
        functools.partial(_proj_kernel, split_input=split_input, n_lat_tiles=n_lat_tiles),
        grid=(B, nt),
        in_specs=_x_specs(split_input, n_lat_tiles) + [
            _mod_spec(n_lat_tiles, B),
            _mod_spec(n_lat_tiles, B),
            pl.BlockSpec(w_ext.shape, lambda b, i: (0, 0)),
            pl.BlockSpec((TILE, 128), lambda b, i: (i, 0)),
            pl.BlockSpec((TILE, 128), lambda b, i: (i, 0)),
        ],
        out_specs=[out_spec] * 9,
        out_shape=[out_shape] * 9,
        compiler_params=_cparams(vmem=VMEM_LIMIT),
        name="ln_mod_in_proj",
    )(*xs, sc, sh, w_ext, cos, sin)


def _channel_dft_matrix():
    c = np.arange(FOURIER_GROUP_W)
    ang = 2.0 * np.pi * np.outer(c, c) / FOURIER_GROUP_W
    eye = np.eye(MIX_GROUP_W // FOURIER_GROUP_W)
    cs = np.concatenate([np.kron(eye, np.cos(ang)), -np.kron(eye, np.sin(ang))], axis=1)
    return jnp.asarray(cs, dtype=BF16)


def _seq_dft_stage_a(n1):
    n2 = FFT_RADIX2
    n = n1 * n2
    k1 = np.arange(n1)[None, :, None]
    i1 = np.arange(n1)[None, None, :]
    i2 = np.arange(n2)[:, None, None]
    th = 2.0 * np.pi * ((k1 * (n2 * i1 + i2)) % n) / n
    wr, wi = np.cos(th), -np.sin(th)
    wa = np.concatenate([np.concatenate([wr, -wi], axis=2), np.concatenate([wi, wr], axis=2)], axis=1)
    return jnp.asarray(wa, dtype=BF16)


def _seq_dft_stage_b():
    n2 = FFT_RADIX2
    k2 = np.arange(n2)
    ph = 2.0 * np.pi * (np.outer(k2, k2) % n2) / n2
    return jnp.asarray(np.concatenate([np.cos(ph), np.sin(ph)], axis=1), dtype=BF16)


def _dense_dft(length):
    k = np.arange(length)
    th = 2.0 * np.pi * (np.outer(k, k) % length) / length
    return jnp.asarray(np.concatenate([np.cos(th), np.sin(th)], axis=1), dtype=BF16)


def _fft_a_kernel(a_ref, cs_ref, wa_ref, t_ref, *, nb, n1):
    g = MIX_GROUP_W
    for b in range(nb):
        p = _dot(a_ref[b], cs_ref[...])
        pp = jnp.concatenate([p[:, :g], p[:, g:]], axis=0).astype(BF16)
        t = _dot(wa_ref[...], pp)
        t_ref[b] = jnp.concatenate([t[:n1], t[n1:]], axis=-1).astype(BF16)


def _fft_b_kernel(t_ref, vb_ref, y_ref, *, nb, scale):
    g = MIX_GROUP_W
    n2 = FFT_RADIX2
    for b in range(nb):
        t = t_ref[b]
        tt = jnp.concatenate([t[:, :g], t[:, g:]], axis=0)
        y = _dot(vb_ref[...], tt) * scale
        y_ref[b, 0:n2, :] = y.astype(BF16)
        y_ref[b, n2:, :] = jnp.zeros((y_ref.shape[1] - n2, g), BF16)


def _fft_ctx_kernel(a_ref, cs_ref, wc_ref, yin_ref, y_ref, *, scale):
    del yin_ref
    g = MIX_GROUP_W
    p = _dot(a_ref[...], cs_ref[...])
    pp = jnp.concatenate([p[:, :g], p[:, g:]], axis=0).astype(BF16)
    y_ref[...] = (_dot(wc_ref[...], pp) * scale).astype(BF16)


def _fourier_mix(a, *, B, N, L, with_ctx):
    S = N + L
    g = MIX_GROUP_W
    n2 = FFT_RADIX2
    n1 = N // n2
    cs = _channel_dft_matrix()
    a2 = a.reshape(B, S // n2, n2 * g)
    t = pl.pallas_call(
        functools.partial(_fft_a_kernel, nb=B, n1=n1),
        grid=(n2,),
        in_specs=[
            pl.BlockSpec((B, n1, g), lambda j: (0, 0, j)),
            pl.BlockSpec(cs.shape, lambda j: (0, 0)),
            pl.BlockSpec((None, 2 * n1, 2 * n1), lambda j: (j, 0, 0)),
        ],
        out_specs=pl.BlockSpec((B, n1, 2 * g), lambda j: (0, 0, j)),
        out_shape=jax.ShapeDtypeStruct((B, n1, n2 * 2 * g), BF16),
        name="fourier_seq_stage_a",
    )(a2, cs, _seq_dft_stage_a(n1))
    t = t.reshape(B, n1 * n2, 2 * g)
    rows = S // n1
    y = pl.pallas_call(
        functools.partial(_fft_b_kernel, nb=B, scale=float((N * FOURIER_GROUP_W) ** -0.5)),
        grid=(n1,),
        in_specs=[
            pl.BlockSpec((B, n2, 2 * g), lambda k: (0, k, 0)),
            pl.BlockSpec((n2, 2 * n2), lambda k: (0, 0)),
        ],
        out_specs=pl.BlockSpec((B, rows, g), lambda k: (0, 0, k)),
        out_shape=jax.ShapeDtypeStruct((B, rows, n1 * g), BF16),
        name="fourier_seq_stage_b",
    )(t, _seq_dft_stage_b())
    y = y.reshape(B, S, g)
    if not with_ctx:
        return y
    ctx_tile = N // TILE
    return pl.pallas_call(
        functools.partial(_fft_ctx_kernel, scale=float((L * FOURIER_GROUP_W) ** -0.5)),
        grid=(B,),
        in_specs=[
            pl.BlockSpec((None, TILE, g), lambda b: (b, ctx_tile, 0)),
            pl.BlockSpec(cs.shape, lambda b: (0, 0)),
            pl.BlockSpec((L, 2 * L), lambda b: (0, 0)),
            pl.BlockSpec(memory_space=pl.ANY),
        ],
        out_specs=pl.BlockSpec((None, TILE, g), lambda b: (b, ctx_tile, 0)),
        out_shape=jax.ShapeDtypeStruct((B, S, g), BF16),
        input_output_aliases={3: 0},
        name="fourier_ctx",
    )(a, cs, _dense_dft(L), y)


POOL_HALO = 16


def _pool_kernel(prev_ref, x_ref, next_ref, pw_ref, ps_ref, o_ref, *, n_lat, s_tot):
    i = pl.program_id(1)
    xm = x_ref[...]
    xc = jnp.concatenate([prev_ref[...], xm, next_ref[...]], axis=0)
    t0 = i * TILE
    is_ctx = t0 >= n_lat
    r0 = jnp.where(is_ctx, n_lat, 0)
    r1 = jnp.where(is_ctx, s_tot, n_lat)
    kw = TILE + 2 * POOL_HALO
    t_col = t0 + lax.broadcasted_iota(I32, (TILE, 1), 0)
    s_pos = t0 - POOL_HALO + lax.broadcasted_iota(I32, (TILE, kw), 1)
    group = lax.broadcasted_iota(I32, (TILE, MIX_GROUP_W), 1) // FOURIER_GROUP_W
    acc = jnp.zeros((TILE, MIX_GROUP_W), F32)
    for gi, w in enumerate(POOL_WINDOWS):
        lo = jnp.maximum(t_col - w // 2, r0)
        hi = jnp.minimum(t_col + w // 2, r1)
        band = jnp.where(s_pos >= lo, jnp.where(s_pos < hi, 1.0, 0.0), 0.0).astype(BF16)
        mean = _dot(band, xc) / (hi - lo).astype(F32)
        acc = jnp.where(group == gi, mean, acc)
    pooled = acc - xm.astype(F32)
    o_ref[...] = (_dot(pooled.astype(BF16), pw_ref[...]) * ps_ref[...]).astype(BF16)


def _pool_mix(bp, pw_bd, pscale, *, B, N, L):
    S = N + L
    nt = S // TILE
    g = MIX_GROUP_W
    hb = TILE // POOL_HALO
    last = S // POOL_HALO - 1
    return pl.pallas_call(
        functools.partial(_pool_kernel, n_lat=N, s_tot=S),
        grid=(B, nt),
        in_specs=[
            pl.BlockSpec((None, POOL_HALO, g), lambda b, i: (b, jnp.maximum(i * hb - 1, 0), 0)),
            pl.BlockSpec((None, TILE, g), lambda b, i: (b, i, 0)),
            pl.BlockSpec((None, POOL_HALO, g), lambda b, i: (b, jnp.minimum((i + 1) * hb, last), 0)),
            pl.BlockSpec((g, g), lambda b, i: (0, 0)),
            pl.BlockSpec((1, g), lambda b, i: (0, 0)),
        ],
        out_specs=pl.BlockSpec((None, TILE, g), lambda b, i: (b, i, 0)),
        out_shape=jax.ShapeDtypeStruct((B, S, g), BF16),
        name="pool_mix",
    )(bp, bp, bp, pw_bd, pscale)


def _attn_kernel(sink_ref, q_ref, kp_ref, kc_ref, kn_ref, kx_ref, vp_ref, vc_ref, vn_ref, vx_ref, o_ref,
                 *, n_lat):
    i = pl.program_id(1)
    t0 = i * TILE
    nband = TILE + 2 * WINDOW
    nkeys = nband + kx_ref.shape[0]
    row = lax.broadcasted_iota(I32, (2 * TILE, nkeys), 0)
    col = lax.broadcasted_iota(I32, (2 * TILE, nkeys), 1)
    qpos = t0 + jnp.where(row >= TILE, row - TILE, row)
    kpos = t0 - WINDOW + col
    in_band = (jnp.abs(qpos - kpos) <= WINDOW) & (kpos >= 0) & (kpos < n_lat) & (t0 < n_lat)
    valid = (col >= nband) | in_band
    lane = lax.broadcasted_iota(I32, (TILE, 2 * HEAD_DIM), 1)
    low = lane < HEAD_DIM
    keep_low = jnp.where(low, 1.0, 0.0).astype(BF16)
    keep_high = jnp.where(low, 0.0, 1.0).astype(BF16)
    row1 = lax.broadcasted_iota(I32, (2 * TILE, 1), 0)
    outs = []
    for h in range(2):
        sl = slice(2 * HEAD_DIM * h, 2 * HEAD_DIM * (h + 1))
        qh = q_ref[:, sl]
        qs = jnp.concatenate([qh * keep_low, qh * keep_high], axis=0)
        kh = jnp.concatenate([kp_ref[:, sl], kc_ref[:, sl], kn_ref[:, sl], kx_ref[:, sl]], axis=0)
        vh = jnp.concatenate([vp_ref[:, sl], vc_ref[:, sl], vn_ref[:, sl], vx_ref[:, sl]], axis=0)
        s = jnp.where(valid, _dot_nt(qs, kh), NEG_INF)
        sink = jnp.where(row1 < TILE, sink_ref[2 * h], sink_ref[2 * h + 1])
        m = jnp.maximum(jnp.max(s, axis=-1, keepdims=True), sink)
        p = jnp.exp(s - m)
        den = jnp.sum(p, axis=-1, keepdims=True) + jnp.exp(sink - m)
        o = _dot(p.astype(BF16), vh) / den
        outs.append(jnp.where(low, o[:TILE], o[TILE:]))
    o_ref[...] = jnp.concatenate(outs, axis=-1).astype(BF16)


def _attention(sinks, q, k2, v2, *, B, N, L):
    S = N + L
    nt = S // TILE
    g = MIX_GROUP_W
    hb = TILE // WINDOW
    last = N // WINDOW - 1
    ctx_tile = N // TILE
    prev = pl.BlockSpec((None, WINDOW, g), lambda b, i: (b, jnp.clip(i * hb - 1, 0, last), 0))
    cur = pl.BlockSpec((None, TILE, g), lambda b, i: (b, i, 0))
    nxt = pl.BlockSpec((None, WINDOW, g), lambda b, i: (b, jnp.minimum((i + 1) * hb, last), 0))
    ctx = pl.BlockSpec((None, L, g), lambda b, i: (b, ctx_tile, 0))
    return pl.pallas_call(
        functools.partial(_attn_kernel, n_lat=N),
        grid=(B, nt),
        in_specs=[pl.BlockSpec(memory_space=pltpu.SMEM), cur, prev, cur, nxt, ctx, prev, cur, nxt, ctx],
        out_specs=pl.BlockSpec((None, TILE, g), lambda b, i: (b, i, 0)),
        out_shape=jax.ShapeDtypeStruct((B, S, g), BF16),
        compiler_params=_cparams(vmem=VMEM_LIMIT),
        name="banded_attention",
    )(sinks, q, k2, k2, k2, k2, v2, v2, v2, v2)


def _ret_kernel(*refs, reverse, nb):
    if reverse:
        dec_ref, q_ref, k_ref, v_ref, of_ref, g_ref, o_ref, r_scr = refs
    else:
        dec_ref, q_ref, k_ref, v_ref, o_ref, r_scr = refs
    c = RET_CHUNK
    w = MIX_GROUP_W
    dh = w // RET_HEADS

    @pl.when(pl.program_id(0) == 0)
    def _():
        r_scr[...] = jnp.zeros(r_scr.shape, F32)

    lg = -jnp.exp(dec_ref[...])
    idx = lax.broadcasted_iota(I32, (c, 1), 0).astype(F32)
    jdx = lax.broadcasted_iota(I32, (1, c), 1).astype(F32)
    if reverse:
        idx, jdx = (c - 1.0) - idx, (c - 1.0) - jdx
    diff = idx - jdx
    dmask = [jnp.where(diff >= 0, jnp.exp(jnp.maximum(diff, 0.0) * lg[:, dh * h:dh * h + 1]), 0.0)
             for h in range(RET_HEADS)]
    xi = jnp.exp((idx + 1.0) * lg)
    zeta = jnp.exp((c - 1.0 - idx) * lg)
    chunk_decay = jnp.exp(float(c) * lg)
    lane_head = lax.broadcasted_iota(I32, (1, w), 1) // dh
    keep_head = [jnp.where(lane_head == h, 1.0, 0.0).astype(BF16) for h in range(RET_HEADS)]
    same_head = (lax.broadcasted_iota(I32, (w, w), 0) // dh) == (lax.broadcasted_iota(I32, (w, w), 1) // dh)

    for b in range(nb):
        q = q_ref[b]
        k = k_ref[b] * jnp.asarray(dh ** -0.5, BF16)
        v = v_ref[b]
        state = r_scr[b]
        o = _dot(q, state.astype(BF16)) * xi
        for h in range(RET_HEADS):
            mh = lane_head == h
            s = _dot_nt(q * keep_head[h], k) * dmask[h]
            o = o + jnp.where(mh, _dot(s.astype(BF16), v), 0.0)
        kv = _dot_tn(k, (v.astype(F32) * zeta).astype(BF16))
        r_scr[b] = state * chunk_decay + jnp.where(same_head, kv, 0.0)
        if not reverse:
            o_ref[b] = o
        else:
            tot = o + of_ref[b]
            mu = jnp.zeros_like(tot)
            for h in range(RET_HEADS):
                mh = lane_head == h
                mu = jnp.where(mh, jnp.sum(jnp.where(mh, tot, 0.0), axis=-1, keepdims=True) / dh, mu)
            cen = tot - mu
            var = jnp.zeros_like(tot)
            for h in range(RET_HEADS):
                mh = lane_head == h
                var = jnp.where(mh, jnp.sum(jnp.where(mh, cen * cen, 0.0), axis=-1, keepdims=True) / dh, var)
            gate = g_ref[b].astype(F32)
            o_ref[b] = (gate * jax.nn.sigmoid(gate) * (cen * lax.rsqrt(var + LN_EPS))).astype(BF16)


def _retention(dec, rq, rk, rv, rg, *, B, N, L):
    S = N + L
    c = RET_CHUNK
    w = MIX_GROUP_W
    nc_lat, nc_ctx = N // c, L // c
    nc = nc_lat + nc_ctx

    def fwd_blk(j):
        return jnp.where(j < nc_ctx, nc_lat + j, j - nc_ctx)

    def bwd_blk(j):
        return nc - 1 - j

    def call(blk, reverse, extra):
        spec = pl.BlockSpec((B, c, w), lambda j: (0, blk(j), 0))
        dspec = pl.BlockSpec((1, w), lambda j: (0, 0))
        return pl.pallas_call(
            functools.partial(_ret_kernel, reverse=reverse, nb=B),
            grid=(nc,),
            in_specs=[dspec] + [spec] * (3 + len(extra)),
            out_specs=spec,
            out_shape=jax.ShapeDtypeStruct((B, S, w), BF16 if reverse else F32),
            scratch_shapes=[pltpu.VMEM((B, w, w), F32)],
            compiler_params=_cparams(sem=("arbitrary",), vmem=VMEM_LIMIT),
            name="retention_bwd" if reverse else "retention_fwd",
        )(dec[1:2] if reverse else dec[0:1], rq, rk, rv, *extra)

    o_f = call(fwd_blk, False, ())
    return call(bwd_blk, True, (o_f, rg))


def _merge_kernel(*refs, split_input, n_lat_tiles, alpha):
    if split_input:
        x_ref, ctx_ref = refs[:2]
        refs = refs[2:]
        x = jnp.where(pl.program_id(1) == n_lat_tiles, ctx_ref[...], x_ref[...])
    else:
        x = refs[0][...]
        refs = refs[1:]
    (yf_ref, yp_ref, ya_ref, yr_ref, g1_ref, sc_ref, sh_ref, lg_ref, lb_ref, wo_ref, rw_ref, rb_ref,
     x1_ref, h2_ref, idx_ref, gate_ref) = refs
    ycat = jnp.concatenate([yf_ref[...], yp_ref[...], ya_ref[...], yr_ref[...]], axis=-1)
    y = _dot(ycat, wo_ref[...])
    x1 = _normalize(alpha * x + g1_ref[...] * y) * lg_ref[...] + lb_ref[...]
    x1_ref[...] = x1
    h2 = _normalize(x1) * (1.0 + sc_ref[...]) + sh_ref[...]
    _to_token_tiles(h2_ref, h2)
    logits = _dot_nt(rw_ref[...], h2.astype(BF16)) + rb_ref[...]
    eid = lax.broadcasted_iota(I32, logits.shape, 0)
    orow = lax.broadcasted_iota(I32, (8, logits.shape[1]), 0)
    idx_out = jnp.zeros((8, logits.shape[1]), I32)
    val_out = jnp.zeros((8, logits.shape[1]), F32)
    top = None
    den = None
    for kk in range(TOP_K):
        m = jnp.max(logits, axis=0, keepdims=True)
        sel = jnp.min(jnp.where(logits == m, eid, N_EXPERTS), axis=0, keepdims=True)
        logits = jnp.where(eid == sel, -jnp.inf, logits)
        if kk == 0:
            top = m
        e = jnp.exp(m - top)
        den = e if kk == 0 else den + e
        idx_out = jnp.where(orow == kk, sel, idx_out)
        val_out = jnp.where(orow == kk, e, val_out)
    idx_ref[...] = idx_out
    gate_ref[...] = val_out / den


def _merge(xs, ys, g1, sc2, sh2, ln_g, ln_b, w_out, rw_t, rb, *, B, S, n_lat_tiles, alpha):
    split_input = len(xs) == 2
    nt = S // TILE
    g = MIX_GROUP_W
    yspec = pl.BlockSpec((None, TILE, g), lambda b, i: (b, i, 0))
    vec = pl.BlockSpec((1, D_MODEL), lambda b, i: (0, 0))
    tok = lambda b, i: (0, b * nt + i)
    return pl.pallas_call(
        functools.partial(_merge_kernel, split_input=split_input, n_lat_tiles=n_lat_tiles, alpha=alpha),
        grid=(B, nt),
        in_specs=_x_specs(split_input, n_lat_tiles) + [yspec] * 4 + [
            _mod_spec(n_lat_tiles, B), _mod_spec(n_lat_tiles, B), _mod_spec(n_lat_tiles, B), vec, vec,
            pl.BlockSpec((D_MODEL, D_MODEL), lambda b, i: (0, 0)),
            pl.BlockSpec((N_EXPERTS, D_MODEL), lambda b, i: (0, 0)),
            pl.BlockSpec((N_EXPERTS, 1), lambda b, i: (0, 0)),
        ],
        out_specs=[
            pl.BlockSpec((None, TILE, D_MODEL), lambda b, i: (b, i, 0)),
            pl.BlockSpec((TILE * ROWS_PER_TOKEN, 128), lambda b, i: (b * nt + i, 0)),
            pl.BlockSpec((8, TILE), tok),
            pl.BlockSpec((8, TILE), tok),
        ],
        out_shape=[
            jax.ShapeDtypeStruct((B, S, D_MODEL), F32),
            jax.ShapeDtypeStruct((B * S * ROWS_PER_TOKEN, 128), F32),
            jax.ShapeDtypeStruct((8, B * S), I32),
            jax.ShapeDtypeStruct((8, B * S), F32),
        ],
        compiler_params=_cparams(vmem=VMEM_LIMIT),
        name="out_proj_norm_router",
    )(*xs, *ys, g1, sc2, sh2, ln_g, ln_b, w_out, rw_t, rb)


def _rank_kernel(idx_ref, lpos_ref, ntab_ref, cbtab_ref, cnt_ref, carry_scr):
    tt = idx_ref.shape[1]

    @pl.when(pl.program_id(0) == 0)
    def _():
        carry_scr[...] = jnp.zeros(carry_scr.shape, F32)

    eid = lax.broadcasted_iota(I32, (N_EXPERTS, tt), 0)
    before = jnp.where(lax.broadcasted_iota(I32, (tt, tt), 0) < lax.broadcasted_iota(I32, (tt, tt), 1),
                       1.0, 0.0).astype(BF16)
    hits, cums, pres = [], [], []
    n_tile = jnp.zeros((N_EXPERTS, 1), F32)
    for kk in range(TOP_K):
        hit = eid == idx_ref[kk:kk + 1, :]
        hits.append(hit)
        cums.append(_dot(jnp.where(hit, 1.0, 0.0).astype(BF16), before))
        pres.append(n_tile)
        n_tile = n_tile + jnp.sum(jnp.where(hit, 1.0, 0.0), axis=1, keepdims=True)
    n16 = ((n_tile.astype(I32) + (RUN_ALIGN - 1)) >> RUN_SHIFT) << RUN_SHIFT
    lower = jnp.where(lax.broadcasted_iota(I32, (N_EXPERTS, N_EXPERTS), 1)
                      < lax.broadcasted_iota(I32, (N_EXPERTS, N_EXPERTS), 0), 1.0, 0.0).astype(BF16)
    hi = jnp.broadcast_to((n16 >> 8).astype(F32), (N_EXPERTS, 128)).astype(BF16)
    lo = jnp.broadcast_to((n16 & 255).astype(F32), (N_EXPERTS, 128)).astype(BF16)
    lstart = (256.0 * _dot(lower, hi) + _dot(lower, lo))[:, 0:1]
    orow = lax.broadcasted_iota(I32, (8, tt), 0)
    out = jnp.zeros((8, tt), I32)
    for kk in range(TOP_K):
        slot = jnp.sum(jnp.where(hits[kk], lstart + pres[kk] + cums[kk], 0.0), axis=0, keepdims=True)
        out = jnp.where(orow == kk, slot.astype(I32) * ROWS_PER_TOKEN, out)
    lpos_ref[...] = out
    carry = carry_scr[...]
    ntab_ref[...] = jnp.broadcast_to(n_tile, ntab_ref.shape).astype(I32)
    cbtab_ref[...] = carry.astype(I32)
    carry = carry + n_tile
    carry_scr[...] = carry
    cnt_ref[...] = carry


def _rank(idx, *, T):
    tt = MOE_TOK_TILE
    nt = T // tt
    tab = pl.BlockSpec((None, N_EXPERTS, 128), lambda i: (i, 0, 0))
    return pl.pallas_call(
        _rank_kernel,
        grid=(nt,),
        in_specs=[pl.BlockSpec((8, tt), lambda i: (0, i))],
        out_specs=[pl.BlockSpec((8, tt), lambda i: (0, i)), tab, tab,
                   pl.BlockSpec((N_EXPERTS, 128), lambda i: (0, 0))],
        out_shape=[jax.ShapeDtypeStruct((8, T), I32), jax.ShapeDtypeStruct((nt, N_EXPERTS, 128), I32),
                   jax.ShapeDtypeStruct((nt, N_EXPERTS, 128), I32), jax.ShapeDtypeStruct((N_EXPERTS, 128), F32)],
        scratch_shapes=[pltpu.VMEM((N_EXPERTS, 128), F32)],
        compiler_params=_cparams(sem=("arbitrary",)),
        name="moe_rank",
    )(idx)


def _run_chunk(src_ref, src_tok, dst_ref, dst_tok, sem):
    return pltpu.make_async_copy(src_ref.at[_token_rows(src_tok, RUN_ALIGN), :],
                                 dst_ref.at[_token_rows(dst_tok, RUN_ALIGN), :], sem)


def _wait_chunks(src_ref, dst_ref, sem, count):
    def body(_, carry):
        _run_chunk(src_ref, 0, dst_ref, 0, sem).wait()
        return carry
    lax.fori_loop(0, count, body, 0)


def _copy_runs(step, pstart_ref, ntab_ref, cbtab_ref, chunk):
    def per_expert(e, carry):
        src0, total = carry
        nch = (ntab_ref[step * N_EXPERTS + e] + (RUN_ALIGN - 1)) >> RUN_SHIFT
        dst0 = pstart_ref[e] + cbtab_ref[step * N_EXPERTS + e]

        def one(c, cc):
            chunk(src0 + c * RUN_ALIGN, dst0 + c * RUN_ALIGN).start()
            return cc
        lax.fori_loop(0, nch, one, 0)
        return src0 + nch * RUN_ALIGN, total + nch
    return lax.fori_loop(0, N_EXPERTS, per_expert, (0, 0))[1]


def _dispatch_kernel(pstart_ref, cnt_ref, pad_ref, ntab_ref, cbtab_ref, lpos_ref, h_ref, xs_ref,
                     stage, zero_scr, sems, sent):
    step = pl.program_id(0)

    @pl.when(step == 0)
    def _():
        sem = sems.at[0]
        zero_scr[...] = jnp.zeros(zero_scr.shape, F32)

        def per_expert(e, total):
            first = pstart_ref[e] + ((cnt_ref[e] >> RUN_SHIFT) << RUN_SHIFT)
            n_fill = (pstart_ref[e] + pad_ref[e] - first) >> RUN_SHIFT

            def fill(c, carry):
                _run_chunk(zero_scr, 0, xs_ref, first + c * RUN_ALIGN, sem).start()
                return carry
            lax.fori_loop(0, n_fill, fill, 0)
            return total + n_fill
        n_total = lax.fori_loop(0, N_EXPERTS, per_expert, 0)
        used = pstart_ref[N_EXPERTS - 1] + pad_ref[N_EXPERTS - 1]
        n_tail = (xs_ref.shape[0] // ROWS_PER_TOKEN - used) >> RUN_SHIFT

        def fill_tail(c, carry):
            _run_chunk(zero_scr, 0, xs_ref, used + c * RUN_ALIGN, sem).start()
            return carry
        lax.fori_loop(0, n_tail, fill_tail, 0)
        _wait_chunks(zero_scr, xs_ref, sem, n_total + n_tail)
        sent[0] = 0
        sent[1] = 0

    slot = step % 2
    buf = stage.at[slot]
    sem = sems.at[slot]

    def zero_last_group(e, src0):
        n_up = ((ntab_ref[step * N_EXPERTS + e] + (RUN_ALIGN - 1)) >> RUN_SHIFT) << RUN_SHIFT
        buf[_token_rows(jnp.maximum(src0 + n_up - RUN_ALIGN, 0), RUN_ALIGN), :] = jnp.zeros((CHUNK_ROWS, 128), F32)
        return src0 + n_up
    lax.fori_loop(0, N_EXPERTS, zero_last_group, 0)
    tt = MOE_TOK_TILE
    for kk in range(TOP_K):
        def place(t, carry, kk=kk):
            row = pl.multiple_of(lpos_ref[kk * tt + t], ROWS_PER_TOKEN)
            buf[pl.ds(row, ROWS_PER_TOKEN), :] = h_ref[_token_rows(t), :]
            return carry
        lax.fori_loop(0, tt, place, 0, unroll=8)
    _wait_chunks(stage.at[1 - slot], xs_ref, sems.at[1 - slot], sent[1 - slot])
    sent[1 - slot] = 0
    sent[slot] = _copy_runs(step, pstart_ref, ntab_ref, cbtab_ref,
                            lambda s, d: _run_chunk(buf, s, xs_ref, d, sem))

    @pl.when(step == pl.num_programs(0) - 1)
    def _():
        _wait_chunks(buf, xs_ref, sem, sent[slot])


def _smem_tile_spec(tt):
    return pl.BlockSpec((8 * tt,), lambda i, *_: (i,), memory_space=pltpu.SMEM)


def _tile_major(table):
    tt = MOE_TOK_TILE
    return table.reshape(table.shape[0], -1, tt).transpose(1, 0, 2).reshape(-1)


def _dispatch(pstart, cnt, padded, ntab, cbtab, lpos, h2t, *, T, P):
    tt = MOE_TOK_TILE
    r = ROWS_PER_TOKEN
    return pl.pallas_call(
        _dispatch_kernel,
        grid_spec=pltpu.PrefetchScalarGridSpec(
            num_scalar_prefetch=5,
            grid=(T // tt,),
            in_specs=[_smem_tile_spec(tt), pl.BlockSpec((tt * r, 128), lambda i, *_: (i, 0))],
            out_specs=pl.BlockSpec(memory_space=pl.ANY),
            scratch_shapes=[pltpu.VMEM((2, STAGE_TOKENS * r, 128), F32), pltpu.VMEM((CHUNK_ROWS, 128), F32),
                            pltpu.SemaphoreType.DMA((2,)), pltpu.SMEM((2,), I32)],
        ),
        out_shape=jax.ShapeDtypeStruct((P * r, 128), F32),
        compiler_params=_cparams(sem=("arbitrary",), vmem=VMEM_LIMIT),
        name="moe_dispatch",
    )(pstart, cnt, padded, ntab, cbtab, _tile_major(lpos), h2t)


def _collect_kernel(pstart_ref, ntab_ref, cbtab_ref, lpos_ref, gate_ref, ys_ref, f_ref, stage, sems, sent):
    step = pl.program_id(0)
    slot = step % 2

    def fetch(tile, b):
        sent[b] = _copy_runs(tile, pstart_ref, ntab_ref, cbtab_ref,
                             lambda s, d: _run_chunk(ys_ref, d, stage.at[b], s, sems.at[b]))

    @pl.when(step == 0)
    def _():
        fetch(0, 0)

    @pl.when(step + 1 < pl.num_programs(0))
    def _():
        fetch(step + 1, 1 - slot)

    buf = stage.at[slot]
    _wait_chunks(ys_ref, buf, sems.at[slot], sent[slot])

    tt = MOE_TOK_TILE

    def combine(t, carry):
        acc = None
        for kk in range(TOP_K):
            row = pl.multiple_of(lpos_ref[kk * tt + t], ROWS_PER_TOKEN)
            term = gate_ref[kk * tt + t] * buf[pl.ds(row, ROWS_PER_TOKEN), :]
            acc = term if acc is None else acc + term
        f_ref[_token_rows(t), :] = acc
        return carry
    lax.fori_loop(0, tt, combine, 0, unroll=4)


def _collect(pstart, ntab, cbtab, lpos, gates, ys, *, T):
    tt = MOE_TOK_TILE
    r = ROWS_PER_TOKEN
    return pl.pallas_call(
        _collect_kernel,
        grid_spec=pltpu.PrefetchScalarGridSpec(
            num_scalar_prefetch=3,
            grid=(T // tt,),
            in_specs=[_smem_tile_spec(tt), _smem_tile_spec(tt), pl.BlockSpec(memory_space=pl.ANY)],
            out_specs=pl.BlockSpec((tt * r, 128), lambda i, *_: (i, 0)),
            scratch_shapes=[pltpu.VMEM((2, STAGE_TOKENS * r, 128), F32), pltpu.SemaphoreType.DMA((2,)),
                            pltpu.SMEM((2,), I32)],
        ),
        out_shape=jax.ShapeDtypeStruct((T * r, 128), F32),
        compiler_params=_cparams(sem=("arbitrary",), vmem=VMEM_LIMIT),
        name="moe_collect",
    )(pstart, ntab, cbtab, _tile_major(lpos), _tile_major(gates), ys)


def _deinterleave_matrix():
    m = np.zeros((256, 256), np.float32)
    j = np.arange(128)
    m[2 * j, j] = 1.0
    m[2 * j + 1, 128 + j] = 1.0
    return jnp.asarray(m, dtype=BF16)


def _expert_kernel(be_ref, nu_ref, x_ref, w1_ref, b1_ref, w2_ref, b2_ref, perm_ref, y_ref, w1_scr, w2_scr):
    i = pl.program_id(0)

    @pl.when(i >= nu_ref[0])
    def _():
        y_ref[...] = jnp.zeros(y_ref.shape, F32)

    @pl.when(i < nu_ref[0])
    def _():
        @pl.when((i == 0) | (be_ref[i] != be_ref[jnp.maximum(i - 1, 0)]))
        def _():
            half = w1_scr.shape[1] // 2
            for m in range(w1_ref.shape[1] // 256):
                r = _dot(w1_ref[:, 256 * m:256 * (m + 1)].astype(BF16), perm_ref[...])
                w1_scr[:, 128 * m:128 * (m + 1)] = r[:, :128].astype(BF16)
                w1_scr[:, half + 128 * m:half + 128 * (m + 1)] = r[:, 128:].astype(BF16)
            w2_scr[...] = w2_ref[...].astype(BF16)

        x = _from_token_tiles(x_ref, MOE_BLK).astype(BF16)
        u = _dot(x, w1_scr[...]) + b1_ref[...]
        half = u.shape[1] // 2
        glu = jnp.minimum(u[:, :half], SWIGLU_LIMIT)
        lin = jnp.clip(u[:, half:], -SWIGLU_LIMIT, SWIGLU_LIMIT)
        act = glu * jax.nn.sigmoid(SWIGLU_ALPHA * glu) * (lin + 1.0)
        _to_token_tiles(y_ref, _dot(act.astype(BF16), w2_scr[...]) + b2_ref[...])


def _experts(blk_e, n_used, xs, w1, b1, w2, b2, *, P, layer):
    nblk = P // MOE_BLK
    r = ROWS_PER_TOKEN
    d, two_f = w1.shape[2:]
    f = w2.shape[2]
    rows = lambda i, be, nu: (jnp.minimum(i, nu[0] - 1), 0)
    return pl.pallas_call(
        _expert_kernel,
        grid_spec=pltpu.PrefetchScalarGridSpec(
            num_scalar_prefetch=2,
            grid=(nblk,),
            in_specs=[
                pl.BlockSpec((MOE_BLK * r, 128), rows),
                pl.BlockSpec((None, None, d, two_f), lambda i, be, nu: (layer, be[i], 0, 0)),
                pl.BlockSpec((None, 1, two_f), lambda i, be, nu: (be[i], 0, 0)),
                pl.BlockSpec((None, None, f, d), lambda i, be, nu: (layer, be[i], 0, 0)),
                pl.BlockSpec((None, 1, d), lambda i, be, nu: (be[i], 0, 0)),
                pl.BlockSpec((256, 256), lambda i, be, nu: (0, 0)),
            ],
            out_specs=pl.BlockSpec((MOE_BLK * r, 128), lambda i, be, nu: (i, 0)),
            scratch_shapes=[pltpu.VMEM((d, two_f), BF16), pltpu.VMEM((f, d), BF16)],
        ),
        out_shape=jax.ShapeDtypeStruct((P * r, 128), F32),
        compiler_params=_cparams(sem=("arbitrary",), vmem=EXPERT_VMEM_LIMIT),
        name="moe_experts",
    )(blk_e, n_used, xs, w1, b1, w2, b2, _deinterleave_matrix())


def _moe(h2t, idx, gates, w1, b1, w2, b2, *, T, layer):
    lpos, ntab, cbtab, cnt = _rank(idx, T=T)
    counts = cnt[:, 0].astype(I32)
    padded = (counts + RUN_ALIGN + MOE_BLK - 1) // MOE_BLK * MOE_BLK
    pad_end = jnp.cumsum(padded)
    pstart = (pad_end - padded).astype(I32)
    nblk = TOP_K * T // MOE_BLK + 2 * N_EXPERTS
    P = nblk * MOE_BLK
    starts = jnp.arange(nblk, dtype=I32) * MOE_BLK
    blk_e = jnp.minimum(jnp.sum((pad_end[None, :] <= starts[:, None]).astype(I32), axis=1), N_EXPERTS - 1)
    n_used = (pad_end[-1:] // MOE_BLK).astype(I32)
    ntab = ntab[:, :, 0].reshape(-1)
    cbtab = cbtab[:, :, 0].reshape(-1)
    xs = _dispatch(pstart, counts, padded.astype(I32), ntab, cbtab, lpos, h2t, T=T, P=P)
    ys = _experts(blk_e.astype(I32), n_used, xs, w1, b1, w2, b2, P=P, layer=layer)
    return _collect(pstart, ntab, cbtab, lpos, gates, ys, T=T)


def _residual_kernel(f_ref, x_ref, g2_ref, lg_ref, lb_ref, o_ref, *, alpha):
    f = _from_token_tiles(f_ref, TILE)
    o_ref[...] = _normalize(alpha * x_ref[...] + g2_ref[...] * f) * lg_ref[...] + lb_ref[...]


def _residual(ft, x1, g2, ln_g, ln_b, *, B, S, n_out_tiles, n_lat_tiles, alpha):
    nt = S // TILE
    vec = pl.BlockSpec((1, D_MODEL), lambda b, i: (0, 0))
    return pl.pallas_call(
        functools.partial(_residual_kernel, alpha=alpha),
        grid=(B, n_out_tiles),
        in_specs=[
            pl.BlockSpec((TILE * ROWS_PER_TOKEN, 128), lambda b, i: (b * nt + i, 0)),
            pl.BlockSpec((None, TILE, D_MODEL), lambda b, i: (b, i, 0)),
            _mod_spec(n_lat_tiles, B), vec, vec,
        ],
        out_specs=pl.BlockSpec((None, TILE, D_MODEL), lambda b, i: (b, i, 0)),
        out_shape=jax.ShapeDtypeStruct((B, n_out_tiles * TILE, D_MODEL), F32),
        compiler_params=_cparams(vmem=VMEM_LIMIT),
        name="ffn_residual_norm",
    )(ft, x1, g2, ln_g, ln_b)


def _rope_tables(n_lat, n_ctx):
    half = HEAD_DIM // 2
    pos = np.arange(n_lat)
    inv = 1.0 / (ROPE_BASE ** (np.arange(0, half, 2, dtype=np.float64) / half))
    ang_row = (pos // GRID_W)[:, None].astype(np.float64) * inv[None, :]
    ang_col = (pos % GRID_W)[:, None].astype(np.float64) * inv[None, :]
    ang = np.concatenate([ang_row, ang_row, ang_col, ang_col], axis=1)
    sign = np.concatenate([-np.ones(16), np.ones(16)] * 2)[None, :]
    cos = np.concatenate([np.cos(ang), np.ones((n_ctx, HEAD_DIM))], axis=0)
    sin = np.concatenate([np.sin(ang) * sign, np.zeros((n_ctx, HEAD_DIM))], axis=0)
    return (jnp.asarray(np.tile(cos, (1, 2)), dtype=F32), jnp.asarray(np.tile(sin, (1, 2)), dtype=F32))


def _extend_w_in(w):
    g = MIX_GROUP_W
    hd = HEAD_DIM
    a, bp, q = w[:, 0:g], w[:, g:2 * g], w[:, 2 * g:3 * g]
    k, v = w[:, 3 * g:3 * g + 2 * hd], w[:, 3 * g + 2 * hd:4 * g]
    rest = w[:, 4 * g:]
    dup = lambda t: jnp.concatenate([t[:, :hd], t[:, :hd], t[:, hd:], t[:, hd:]], axis=1)
    return jnp.concatenate([a, bp, q, dup(k), dup(v), rest], axis=1).astype(BF16)


def _block_diag(pw):
    n, c, _ = pw.shape
    out = jnp.zeros((n * c, n * c), pw.dtype)
    for gi in range(n):
        out = lax.dynamic_update_slice(out, pw[gi], (gi * c, gi * c))
    return out


def kernel(x, c, ctx, c_ctx, w_mod, b_mod, w_in, pool_w, pool_scale, attn_sink, ret_decay, w_out, ln_g, ln_b,
           router_w, router_b, exp_w1, exp_b1, exp_w2, exp_b2):
    B, N, D = x.shape
    L = ctx.shape[1]
    depth = w_mod.shape[0]
    assert D == D_MODEL and L == TILE and N % TILE == 0 and N % (FFT_RADIX2 * 16) == 0 and B < 8
    S = N + L
    n_lat_tiles = N // TILE
    T = B * S
    assert T % MOE_TOK_TILE == 0
    alpha = float((2 * depth) ** 0.25)

    cvec = jnp.zeros((8, D), F32).at[:B].set(c).at[B].set(c_ctx)
    mods = _modulation(cvec, w_mod, b_mod)
    cos, sin = _rope_tables(N, L)

    xs = (x, ctx)
    out = None
    for l in range(depth):
        last = l == depth - 1
        m = mods[l].reshape(8, 6, 1, D)
        sh1, sc1, g1, sh2, sc2, g2 = [m[:, j] for j in range(6)]
        proj = _projection(xs, sc1, sh1, _extend_w_in(w_in[l]), cos, sin, B=B, S=S, n_lat_tiles=n_lat_tiles)
        a, bp, q, k2, v2, rq, rk, rv, rg = proj
        y_f = _fourier_mix(a, B=B, N=N, L=L, with_ctx=not last)
        y_p = _pool_mix(bp, _block_diag(pool_w[l]).astype(BF16), pool_scale[l].reshape(1, -1), B=B, N=N, L=L)
        y_a = _attention(attn_sink[l], q, k2, v2, B=B, N=N, L=L)
        dec = jnp.repeat(ret_decay[l], MIX_GROUP_W // RET_HEADS, axis=1)
        y_r = _retention(dec, rq, rk, rv, rg, B=B, N=N, L=L)
        x1, h2t, idx, gates = _merge(
            xs, (y_f, y_p, y_a, y_r), g1, sc2, sh2, ln_g[l, 0:1], ln_b[l, 0:1], w_out[l].astype(BF16),
            router_w[l].T.astype(BF16), router_b[l].reshape(-1, 1), B=B, S=S, n_lat_tiles=n_lat_tiles, alpha=alpha)
        b1 = jnp.concatenate([exp_b1[l][:, 0::2], exp_b1[l][:, 1::2]], axis=-1)[:, None, :]
        ft = _moe(h2t, idx, gates, exp_w1, b1, exp_w2, exp_b2[l][:, None, :], T=T, layer=l)
        out = _residual(ft, x1, g2, ln_g[l, 1:2], ln_b[l, 1:2], B=B, S=S,
                        n_out_tiles=n_lat_tiles if last else S // TILE, n_lat_tiles=n_lat_tiles, alpha=alpha)
        xs = (out,)
    return out
```

```python
import functools

import numpy as np
import jax
import jax.numpy as jnp
from jax import lax
from jax.experimental import pallas as pl
from jax.experimental.pallas import tpu as pltpu

F32 = jnp.float32
BF16 = jnp.bfloat16
I32 = jnp.int32

D_MODEL = 1024
GRID_W = 64
FOURIER_GROUP_W = 64
POOL_WINDOWS = (2, 4, 8, 16)
HEAD_DIM = 64
WINDOW = 128
ROPE_BASE = 10000.0
RET_HEADS = 4
RET_CHUNK = 128
N_EXPERTS = 32
TOP_K = 4
SWIGLU_LIMIT = 7.0
SWIGLU_ALPHA = 1.702
LN_EPS = 1e-6
NEG_INF = -1e30
MIX_GROUP_W = 256
FFT_RADIX2 = 64

TILE = 256
MOE_BLK = 512
EXPERT_CHUNK = 1024
MOE_TOK_TILE = 512
ROWS_PER_TOKEN = D_MODEL // 128
RUN_ALIGN = 16
RUN_SHIFT = 4
CHUNK_ROWS = RUN_ALIGN * ROWS_PER_TOKEN
STAGE_TOKENS = TOP_K * MOE_TOK_TILE + N_EXPERTS * RUN_ALIGN
VMEM_LIMIT = 48 * 1024 * 1024
EXPERT_VMEM_LIMIT = 56 * 1024 * 1024


def _dot(a, b):
    return jnp.dot(a, b, preferred_element_type=F32)


def _dot_nt(a, b):
    return lax.dot_general(a, b, (((1,), (1,)), ((), ())), preferred_element_type=F32)


def _dot_tn(a, b):
    return lax.dot_general(a, b, (((0,), (0,)), ((), ())), preferred_element_type=F32)


def _normalize(x):
    mu = jnp.mean(x, axis=-1, keepdims=True)
    xc = x - mu
    var = jnp.mean(xc * xc, axis=-1, keepdims=True)
    return xc * lax.rsqrt(var + LN_EPS)


def _cparams(sem=None, vmem=None):
    kw = {}
    if sem is not None:
        kw["dimension_semantics"] = sem
    if vmem is not None:
        kw["vmem_limit_bytes"] = vmem
    return pltpu.CompilerParams(**kw)


def _to_token_tiles(ref, val):
    n = val.shape[0]
    for c in range(ROWS_PER_TOKEN):
        ref[pl.ds(c, n, stride=ROWS_PER_TOKEN), :] = val[:, 128 * c:128 * (c + 1)]


def _from_token_tiles(ref, n):
    return jnp.concatenate([ref[pl.ds(c, n, stride=ROWS_PER_TOKEN), :] for c in range(ROWS_PER_TOKEN)], axis=-1)


def _token_rows(tok, count=1):
    return pl.ds(pl.multiple_of(tok * ROWS_PER_TOKEN, ROWS_PER_TOKEN), count * ROWS_PER_TOKEN)


def _mod_kernel(c_ref, w_ref, b_ref, o_ref):
    c = c_ref[...]
    a = c * jax.nn.sigmoid(c)
    w = w_ref[...]
    a_hi = a.astype(BF16)
    a_lo = (a - a_hi.astype(F32)).astype(BF16)
    w_hi = w.astype(BF16)
    w_lo = (w - w_hi.astype(F32)).astype(BF16)
    o_ref[...] = _dot(a_hi, w_hi) + _dot(a_lo, w_hi) + _dot(a_hi, w_lo) + b_ref[...]


def _modulation(cvec, w_mod, b_mod):
    depth, d, six_d = w_mod.shape
    nb = 1536
    return pl.pallas_call(
        _mod_kernel,
        grid=(depth, six_d // nb),
        in_specs=[
            pl.BlockSpec((8, d), lambda l, j: (0, 0)),
            pl.BlockSpec((None, d, nb), lambda l, j: (l, 0, j)),
            pl.BlockSpec((None, 1, nb), lambda l, j: (l, 0, j)),
        ],
        out_specs=pl.BlockSpec((None, 8, nb), lambda l, j: (l, 0, j)),
        out_shape=jax.ShapeDtypeStruct((depth, 8, six_d), F32),
        compiler_params=_cparams(vmem=VMEM_LIMIT),
        name="modulation",
    )(cvec, w_mod, b_mod.reshape(depth, 1, six_d))


def _tile_inputs(refs, b, split_input, n_lat_tiles):
    if split_input:
        x = jnp.where(pl.program_id(0) == n_lat_tiles, refs[1][b], refs[0][b])
        return x, refs[2:]
    return refs[0][b], refs[1:]


def _mod_row(b, n_lat_tiles, ctx_row):
    return jnp.where(pl.program_id(0) == n_lat_tiles, ctx_row, b)


def _proj_kernel(*refs, nb, split_input, n_lat_tiles):
    g = MIX_GROUP_W
    for b in range(nb):
        x, rest = _tile_inputs(refs, b, split_input, n_lat_tiles)
        sc_ref, sh_ref, w_ref, cos_ref, sin_ref = rest[:5]
        a_ref, bp_ref, q_ref, k_ref, v_ref, rq_ref, rk_ref, rv_ref, rg_ref = rest[5:]
        row = _mod_row(b, n_lat_tiles, nb)
        h = _normalize(x) * (1.0 + sc_ref[row]) + sh_ref[row]
        p = _dot(h.astype(BF16), w_ref[...])
        a_ref[b] = p[:, 0:g].astype(BF16)
        bp_ref[b] = p[:, g:2 * g].astype(BF16)
        qk = p[:, 2 * g:4 * g]
        lane = lax.broadcasted_iota(I32, qk.shape, 1)
        first = (lane & 16) == 0
        partner = jnp.where(first, pltpu.roll(qk, 2 * g - 16, 1), pltpu.roll(qk, 16, 1))
        cos = jnp.concatenate([cos_ref[...]] * 4, axis=-1)
        sin = jnp.concatenate([sin_ref[...]] * 4, axis=-1)
        qk = qk * cos + partner * sin
        q_ref[b] = (qk[:, :g] * (HEAD_DIM ** -0.5)).astype(BF16)
        k_ref[b] = qk[:, g:].astype(BF16)
        v_ref[b] = p[:, 4 * g:5 * g].astype(BF16)
        rq_ref[b] = p[:, 5 * g:6 * g].astype(BF16)
        rk_ref[b] = p[:, 6 * g:7 * g].astype(BF16)
        rv_ref[b] = p[:, 7 * g:8 * g].astype(BF16)
        rg_ref[b] = p[:, 8 * g:9 * g].astype(BF16)


def _mod_spec():
    return pl.BlockSpec((8, 1, D_MODEL), lambda i: (0, 0, 0))


def _x_specs(B, split_input, n_lat_tiles):
    if split_input:
        return [
            pl.BlockSpec((B, TILE, D_MODEL), lambda i: (0, jnp.minimum(i, n_lat_tiles - 1), 0)),
            pl.BlockSpec((B, TILE, D_MODEL), lambda i: (0, 0, 0)),
        ]
    return [pl.BlockSpec((B, TILE, D_MODEL), lambda i: (0, i, 0))]


def _projection(xs, sc, sh, w_ext, cos, sin, *, B, S, n_lat_tiles):
    split_input = len(xs) == 2
    nt = S // TILE
    g = MIX_GROUP_W
    out_spec = pl.BlockSpec((B, TILE, g), lambda i: (0, i, 0))
    out_shape = jax.ShapeDtypeStruct((B, S, g), BF16)
    return pl.pallas_call(
        functools.partial(_proj_kernel, nb=B, split_input=split_input, n_lat_tiles=n_lat_tiles),
        grid=(nt,),
        in_specs=_x_specs(B, split_input, n_lat_tiles) + [
            _mod_spec(),
            _mod_spec(),
            pl.BlockSpec(w_ext.shape, lambda i: (0, 0)),
            pl.BlockSpec((TILE, 128), lambda i: (i, 0)),
            pl.BlockSpec((TILE, 128), lambda i: (i, 0)),
        ],
        out_specs=[out_spec] * 9,
        out_shape=[out_shape] * 9,
        compiler_params=_cparams(vmem=VMEM_LIMIT),
        name="ln_mod_in_proj",
    )(*xs, sc, sh, w_ext, cos, sin)


def _channel_dft_matrix():
    c = np.arange(FOURIER_GROUP_W)
    ang = 2.0 * np.pi * np.outer(c, c) / FOURIER_GROUP_W
    eye = np.eye(MIX_GROUP_W // FOURIER_GROUP_W)
    cs = np.concatenate([np.kron(eye, np.cos(ang)), -np.kron(eye, np.sin(ang))], axis=1)
    return jnp.asarray(cs, dtype=BF16)


def _seq_dft_stage_a(n1):
    n2 = FFT_RADIX2
    n = n1 * n2
    k1 = np.arange(n1)[None, :, None]
    i1 = np.arange(n1)[None, None, :]
    i2 = np.arange(n2)[:, None, None]
    th = 2.0 * np.pi * ((k1 * (n2 * i1 + i2)) % n) / n
    wr, wi = np.cos(th), -np.sin(th)
    wa = np.concatenate([np.concatenate([wr, -wi], axis=2), np.concatenate([wi, wr], axis=2)], axis=1)
    return jnp.asarray(wa, dtype=BF16)


def _seq_dft_stage_b():
    n2 = FFT_RADIX2
    k2 = np.arange(n2)
    ph = 2.0 * np.pi * (np.outer(k2, k2) % n2) / n2
    return jnp.asarray(np.concatenate([np.cos(ph), np.sin(ph)], axis=1), dtype=BF16)


def _dense_dft(length):
    k = np.arange(length)
    th = 2.0 * np.pi * (np.outer(k, k) % length) / length
    return jnp.asarray(np.concatenate([np.cos(th), np.sin(th)], axis=1), dtype=BF16)


def _fft_a_kernel(a_ref, cs_ref, wa_ref, t_ref, *, nb, n1):
    g = MIX_GROUP_W
    for b in range(nb):
        p = _dot(a_ref[b], cs_ref[...])
        pp = jnp.concatenate([p[:, :g], p[:, g:]], axis=0).astype(BF16)
        t = _dot(wa_ref[...], pp)
        t_ref[b] = jnp.concatenate([t[:n1], t[n1:]], axis=-1).astype(BF16)


def _fft_b_kernel(t_ref, vb_ref, y_ref, *, nb, scale):
    g = MIX_GROUP_W
    n2 = FFT_RADIX2
    for b in range(nb):
        t = t_ref[b]
        tt = jnp.concatenate([t[:, :g], t[:, g:]], axis=0)
        y = _dot(vb_ref[...], tt) * scale
        y_ref[b, 0:n2, :] = y.astype(BF16)
        y_ref[b, n2:, :] = jnp.zeros((y_ref.shape[1] - n2, g), BF16)


def _fft_ctx_kernel(a_ref, cs_ref, wc_ref, yin_ref, y_ref, *, scale):
    del yin_ref
    g = MIX_GROUP_W
    p = _dot(a_ref[...], cs_ref[...])
    pp = jnp.concatenate([p[:, :g], p[:, g:]], axis=0).astype(BF16)
    y_ref[...] = (_dot(wc_ref[...], pp) * scale).astype(BF16)


def _fourier_mix(a, *, B, N, L, with_ctx):
    S = N + L
    g = MIX_GROUP_W
    n2 = FFT_RADIX2
    n1 = N // n2
    cs = _channel_dft_matrix()
    a2 = a.reshape(B, S // n2, n2 * g)
    t = pl.pallas_call(
        functools.partial(_fft_a_kernel, nb=B, n1=n1),
        grid=(n2,),
        in_specs=[
            pl.BlockSpec((B, n1, g), lambda j: (0, 0, j)),
            pl.BlockSpec(cs.shape, lambda j: (0, 0)),
            pl.BlockSpec((None, 2 * n1, 2 * n1), lambda j: (j, 0, 0)),
        ],
        out_specs=pl.BlockSpec((B, n1, 2 * g), lambda j: (0, 0, j)),
        out_shape=jax.ShapeDtypeStruct((B, n1, n2 * 2 * g), BF16),
        name="fourier_seq_stage_a",
    )(a2, cs, _seq_dft_stage_a(n1))
    t = t.reshape(B, n1 * n2, 2 * g)
    rows = S // n1
    y = pl.pallas_call(
        functools.partial(_fft_b_kernel, nb=B, scale=float((N * FOURIER_GROUP_W) ** -0.5)),
        grid=(n1,),
        in_specs=[
            pl.BlockSpec((B, n2, 2 * g), lambda k: (0, k, 0)),
            pl.BlockSpec((n2, 2 * n2), lambda k: (0, 0)),
        ],
        out_specs=pl.BlockSpec((B, rows, g), lambda k: (0, 0, k)),
        out_shape=jax.ShapeDtypeStruct((B, rows, n1 * g), BF16),
        name="fourier_seq_stage_b",
    )(t, _seq_dft_stage_b())
    y = y.reshape(B, S, g)
    if not with_ctx:
        return y
    ctx_tile = N // TILE
    return pl.pallas_call(
        functools.partial(_fft_ctx_kernel, scale=float((L * FOURIER_GROUP_W) ** -0.5)),
        grid=(B,),
        in_specs=[
            pl.BlockSpec((None, TILE, g), lambda b: (b, ctx_tile, 0)),
            pl.BlockSpec(cs.shape, lambda b: (0, 0)),
            pl.BlockSpec((L, 2 * L), lambda b: (0, 0)),
            pl.BlockSpec(memory_space=pl.ANY),
        ],
        out_specs=pl.BlockSpec((None, TILE, g), lambda b: (b, ctx_tile, 0)),
        out_shape=jax.ShapeDtypeStruct((B, S, g), BF16),
        input_output_aliases={3: 0},
        name="fourier_ctx",
    )(a, cs, _dense_dft(L), y)


POOL_HALO = 16


def _pool_bands():
    t = np.arange(TILE)[:, None]
    s = np.arange(TILE + 2 * POOL_HALO)[None, :] - POOL_HALO
    bands = [((s - t >= -(w // 2)) & (s - t < w // 2)).astype(np.float32) for w in POOL_WINDOWS]
    return jnp.asarray(np.stack(bands), dtype=BF16)


def _pool_kernel(prev_ref, x_ref, next_ref, band_ref, pw_ref, ps_ref, o_ref, *, n_lat, s_tot):
    i = pl.program_id(1)
    xm = x_ref[...]
    t0 = i * TILE
    is_ctx = t0 >= n_lat
    r0 = jnp.where(is_ctx, n_lat, 0)
    r1 = jnp.where(is_ctx, s_tot, n_lat)
    kw = TILE + 2 * POOL_HALO
    s_pos = t0 - POOL_HALO + lax.broadcasted_iota(I32, (kw, 1), 0)
    inside = jnp.where(s_pos >= r0, jnp.where(s_pos < r1, 1.0, 0.0), 0.0).astype(BF16)
    xc = jnp.concatenate([prev_ref[...], xm, next_ref[...]], axis=0) * inside
    t_col = t0 + lax.broadcasted_iota(I32, (TILE, 1), 0)
    group = lax.broadcasted_iota(I32, (TILE, MIX_GROUP_W), 1) // FOURIER_GROUP_W
    acc = jnp.zeros((TILE, MIX_GROUP_W), F32)
    for gi, w in enumerate(POOL_WINDOWS):
        lo = jnp.maximum(t_col - w // 2, r0)
        hi = jnp.minimum(t_col + w // 2, r1)
        mean = _dot(band_ref[gi], xc) / (hi - lo).astype(F32)
        acc = jnp.where(group == gi, mean, acc)
    pooled = acc - xm.astype(F32)
    o_ref[...] = (_dot(pooled.astype(BF16), pw_ref[...]) * ps_ref[...]).astype(BF16)


def _pool_mix(bp, pw_bd, pscale, *, B, N, L):
    S = N + L
    nt = S // TILE
    g = MIX_GROUP_W
    hb = TILE // POOL_HALO
    last = S // POOL_HALO - 1
    return pl.pallas_call(
        functools.partial(_pool_kernel, n_lat=N, s_tot=S),
        grid=(B, nt),
        in_specs=[
            pl.BlockSpec((None, POOL_HALO, g), lambda b, i: (b, jnp.maximum(i * hb - 1, 0), 0)),
            pl.BlockSpec((None, TILE, g), lambda b, i: (b, i, 0)),
            pl.BlockSpec((None, POOL_HALO, g), lambda b, i: (b, jnp.minimum((i + 1) * hb, last), 0)),
            pl.BlockSpec((len(POOL_WINDOWS), TILE, TILE + 2 * POOL_HALO), lambda b, i: (0, 0, 0)),
            pl.BlockSpec((g, g), lambda b, i: (0, 0)),
            pl.BlockSpec((1, g), lambda b, i: (0, 0)),
        ],
        out_specs=pl.BlockSpec((None, TILE, g), lambda b, i: (b, i, 0)),
        out_shape=jax.ShapeDtypeStruct((B, S, g), BF16),
        name="pool_mix",
    )(bp, bp, bp, _pool_bands(), pw_bd, pscale)


def _attn_window_bias(n_ctx):
    r = (np.arange(2 * TILE) % TILE)[:, None]
    c = np.arange(TILE + 2 * WINDOW)[None, :] - WINDOW
    band = np.where(np.abs(r - c) <= WINDOW, 0.0, NEG_INF)
    return jnp.asarray(np.concatenate([band, np.zeros((2 * TILE, n_ctx))], axis=1), dtype=F32)


def _attn_kernel(sink_ref, bias_ref, q_ref, kp_ref, kc_ref, kn_ref, kx_ref, vp_ref, vc_ref, vn_ref, vx_ref, o_ref,
                 *, n_lat):
    i = pl.program_id(1)
    t0 = i * TILE
    nband = TILE + 2 * WINDOW
    nkeys = nband + kx_ref.shape[0]
    col = lax.broadcasted_iota(I32, (1, nkeys), 1)
    kpos = t0 - WINDOW + col
    valid = (col >= nband) | ((kpos >= 0) & (kpos < n_lat) & (t0 < n_lat))
    bias = bias_ref[...]
    lane = lax.broadcasted_iota(I32, (TILE, 2 * HEAD_DIM), 1)
    low = lane < HEAD_DIM
    keep_low = jnp.where(low, 1.0, 0.0).astype(BF16)
    keep_high = jnp.where(low, 0.0, 1.0).astype(BF16)
    row1 = lax.broadcasted_iota(I32, (2 * TILE, 1), 0)
    outs = []
    for h in range(2):
        sl = slice(2 * HEAD_DIM * h, 2 * HEAD_DIM * (h + 1))
        qh = q_ref[:, sl]
        qs = jnp.concatenate([qh * keep_low, qh * keep_high], axis=0)
        kh = jnp.concatenate([kp_ref[:, sl], kc_ref[:, sl], kn_ref[:, sl], kx_ref[:, sl]], axis=0)
        vh = jnp.concatenate([vp_ref[:, sl], vc_ref[:, sl], vn_ref[:, sl], vx_ref[:, sl]], axis=0)
        s = jnp.where(valid, _dot_nt(qs, kh) + bias, NEG_INF)
        sink = jnp.where(row1 < TILE, sink_ref[2 * h], sink_ref[2 * h + 1])
        m = jnp.maximum(jnp.max(s, axis=-1, keepdims=True), sink)
        p = jnp.exp(s - m)
        den = jnp.sum(p, axis=-1, keepdims=True) + jnp.exp(sink - m)
        o = _dot(p.astype(BF16), vh) / den
        outs.append(jnp.where(low, o[:TILE], o[TILE:]))
    o_ref[...] = jnp.concatenate(outs, axis=-1).astype(BF16)


def _attention(sinks, q, k2, v2, *, B, N, L):
    S = N + L
    nt = S // TILE
    g = MIX_GROUP_W
    hb = TILE // WINDOW
    last = N // WINDOW - 1
    ctx_tile = N // TILE
    prev = pl.BlockSpec((None, WINDOW, g), lambda b, i: (b, jnp.clip(i * hb - 1, 0, last), 0))
    cur = pl.BlockSpec((None, TILE, g), lambda b, i: (b, i, 0))
    nxt = pl.BlockSpec((None, WINDOW, g), lambda b, i: (b, jnp.minimum((i + 1) * hb, last), 0))
    ctx = pl.BlockSpec((None, L, g), lambda b, i: (b, ctx_tile, 0))
    return pl.pallas_call(
        functools.partial(_attn_kernel, n_lat=N),
        grid=(B, nt),
        in_specs=[pl.BlockSpec(memory_space=pltpu.SMEM),
                  pl.BlockSpec((2 * TILE, TILE + 2 * WINDOW + L), lambda b, i: (0, 0)),
                  cur, prev, cur, nxt, ctx, prev, cur, nxt, ctx],
        out_specs=pl.BlockSpec((None, TILE, g), lambda b, i: (b, i, 0)),
        out_shape=jax.ShapeDtypeStruct((B, S, g), BF16),
        compiler_params=_cparams(vmem=VMEM_LIMIT),
        name="banded_attention",
    )(sinks, _attn_window_bias(L), q, k2, k2, k2, k2, v2, v2, v2, v2)


def _ret_kernel(*refs, reverse, nb):
    if reverse:
        dec_ref, q_ref, k_ref, v_ref, of_ref, g_ref, o_ref, r_scr = refs
    else:
        dec_ref, q_ref, k_ref, v_ref, o_ref, r_scr = refs
    c = RET_CHUNK
    w = MIX_GROUP_W
    dh = w // RET_HEADS

    @pl.when(pl.program_id(0) == 0)
    def _():
        r_scr[...] = jnp.zeros(r_scr.shape, F32)

    lg = -jnp.exp(dec_ref[...])
    idx = lax.broadcasted_iota(I32, (c, 1), 0).astype(F32)
    jdx = lax.broadcasted_iota(I32, (1, c), 1).astype(F32)
    if reverse:
        idx, jdx = (c - 1.0) - idx, (c - 1.0) - jdx
    diff = idx - jdx
    dmask = [jnp.where(diff >= 0, jnp.exp(jnp.maximum(diff, 0.0) * lg[:, dh * h:dh * h + 1]), 0.0)
             for h in range(RET_HEADS)]
    xi = jnp.exp((idx + 1.0) * lg)
    zeta = jnp.exp((c - 1.0 - idx) * lg)
    chunk_decay = jnp.exp(float(c) * lg)
    lane_head = lax.broadcasted_iota(I32, (1, w), 1) // dh
    keep_head = [jnp.where(lane_head == h, 1.0, 0.0).astype(BF16) for h in range(RET_HEADS)]
    same_head = (lax.broadcasted_iota(I32, (w, w), 0) // dh) == (lax.broadcasted_iota(I32, (w, w), 1) // dh)

    for b in range(nb):
        q = q_ref[b]
        k = k_ref[b] * jnp.asarray(dh ** -0.5, BF16)
        v = v_ref[b]
        state = r_scr[b]
        o = _dot(q, state.astype(BF16)) * xi
        for h in range(RET_HEADS):
            mh = lane_head == h
            s = _dot_nt(q * keep_head[h], k) * dmask[h]
            o = o + jnp.where(mh, _dot(s.astype(BF16), v), 0.0)
        kv = _dot_tn(k, (v.astype(F32) * zeta).astype(BF16))
        r_scr[b] = state * chunk_decay + jnp.where(same_head, kv, 0.0)
        if not reverse:
            o_ref[b] = o
        else:
            tot = o + of_ref[b]
            mu = jnp.zeros_like(tot)
            for h in range(RET_HEADS):
                mh = lane_head == h
                mu = jnp.where(mh, jnp.sum(jnp.where(mh, tot, 0.0), axis=-1, keepdims=True) / dh, mu)
            cen = tot - mu
            var = jnp.zeros_like(tot)
            for h in range(RET_HEADS):
                mh = lane_head == h
                var = jnp.where(mh, jnp.sum(jnp.where(mh, cen * cen, 0.0), axis=-1, keepdims=True) / dh, var)
            gate = g_ref[b].astype(F32)
            o_ref[b] = (gate * jax.nn.sigmoid(gate) * (cen * lax.rsqrt(var + LN_EPS))).astype(BF16)


def _retention(dec, rq, rk, rv, rg, *, B, N, L):
    S = N + L
    c = RET_CHUNK
    w = MIX_GROUP_W
    nc_lat, nc_ctx = N // c, L // c
    nc = nc_lat + nc_ctx

    def fwd_blk(j):
        return jnp.where(j < nc_ctx, nc_lat + j, j - nc_ctx)

    def bwd_blk(j):
        return nc - 1 - j

    def call(blk, reverse, extra):
        spec = pl.BlockSpec((B, c, w), lambda j: (0, blk(j), 0))
        dspec = pl.BlockSpec((1, w), lambda j: (0, 0))
        return pl.pallas_call(
            functools.partial(_ret_kernel, reverse=reverse, nb=B),
            grid=(nc,),
            in_specs=[dspec] + [spec] * (3 + len(extra)),
            out_specs=spec,
            out_shape=jax.ShapeDtypeStruct((B, S, w), BF16 if reverse else F32),
            scratch_shapes=[pltpu.VMEM((B, w, w), F32)],
            compiler_params=_cparams(sem=("arbitrary",), vmem=VMEM_LIMIT),
            name="retention_bwd" if reverse else "retention_fwd",
        )(dec[1:2] if reverse else dec[0:1], rq, rk, rv, *extra)

    o_f = call(fwd_blk, False, ())
    return call(bwd_blk, True, (o_f, rg))


def _merge_kernel(*refs, nb, split_input, n_lat_tiles, alpha):
    for b in range(nb):
        x, rest = _tile_inputs(refs, b, split_input, n_lat_tiles)
        (yf_ref, yp_ref, ya_ref, yr_ref, g1_ref, sc_ref, sh_ref, lg_ref, lb_ref, wo_ref, rw_ref, rb_ref,
         x1_ref, h2_ref) = rest[:14]
        row = _mod_row(b, n_lat_tiles, nb)
        ycat = jnp.concatenate([yf_ref[b], yp_ref[b], ya_ref[b], yr_ref[b]], axis=-1)
        y = _dot(ycat, wo_ref[...])
        x1 = _normalize(alpha * x + g1_ref[row] * y) * lg_ref[...] + lb_ref[...]
        x1_ref[b] = x1
        h2 = _normalize(x1) * (1.0 + sc_ref[row]) + sh_ref[row]
        _to_token_tiles(h2_ref.at[b], h2)
        logits = _dot_nt(rw_ref[...], h2.astype(BF16)) + rb_ref[...]
        _route(logits, rest[14 + b], rest[14 + nb + b])


def _route(logits, idx_ref, gate_ref):
    eid = lax.broadcasted_iota(I32, logits.shape, 0)
    orow = lax.broadcasted_iota(I32, (8, logits.shape[1]), 0)
    idx_out = jnp.zeros((8, logits.shape[1]), I32)
    val_out = jnp.zeros((8, logits.shape[1]), F32)
    top = None
    den = None
    for kk in range(TOP_K):
        m = jnp.max(logits, axis=0, keepdims=True)
        sel = jnp.min(jnp.where(logits == m, eid, N_EXPERTS), axis=0, keepdims=True)
        logits = jnp.where(eid == sel, -jnp.inf, logits)
        if kk == 0:
            top = m
        e = jnp.exp(m - top)
        den = e if kk == 0 else den + e
        idx_out = jnp.where(orow == kk, sel, idx_out)
        val_out = jnp.where(orow == kk, e, val_out)
    idx_ref[...] = idx_out
    gate_ref[...] = val_out / den


def _merge(xs, ys, g1, sc2, sh2, ln_g, ln_b, w_out, rw_t, rb, *, B, S, n_lat_tiles, alpha):
    split_input = len(xs) == 2
    nt = S // TILE
    g = MIX_GROUP_W
    r = ROWS_PER_TOKEN
    yspec = pl.BlockSpec((B, TILE, g), lambda i: (0, i, 0))
    vec = pl.BlockSpec((1, D_MODEL), lambda i: (0, 0))
    route_spec = pl.BlockSpec((8, TILE), lambda i: (0, i))
    outs = pl.pallas_call(
        functools.partial(_merge_kernel, nb=B, split_input=split_input, n_lat_tiles=n_lat_tiles, alpha=alpha),
        grid=(nt,),
        in_specs=_x_specs(B, split_input, n_lat_tiles) + [yspec] * 4 + [
            _mod_spec(), _mod_spec(), _mod_spec(), vec, vec,
            pl.BlockSpec((D_MODEL, D_MODEL), lambda i: (0, 0)),
            pl.BlockSpec((N_EXPERTS, D_MODEL), lambda i: (0, 0)),
            pl.BlockSpec((N_EXPERTS, 1), lambda i: (0, 0)),
        ],
        out_specs=[
            pl.BlockSpec((B, TILE, D_MODEL), lambda i: (0, i, 0)),
            pl.BlockSpec((B, TILE * r, 128), lambda i: (0, i, 0)),
        ] + [route_spec] * (2 * B),
        out_shape=[
            jax.ShapeDtypeStruct((B, S, D_MODEL), F32),
            jax.ShapeDtypeStruct((B, S * r, 128), F32),
        ] + [jax.ShapeDtypeStruct((8, S), I32)] * B + [jax.ShapeDtypeStruct((8, S), F32)] * B,
        compiler_params=_cparams(vmem=VMEM_LIMIT),
        name="out_proj_norm_router",
    )(*xs, *ys, g1, sc2, sh2, ln_g, ln_b, w_out, rw_t, rb)
    x1, h2t = outs[:2]
    idx = jnp.concatenate(outs[2:2 + B], axis=1)
    gates = jnp.concatenate(outs[2 + B:], axis=1)
    return x1, h2t.reshape(B * S * r, 128), idx, gates


def _rank_kernel(idx_ref, lpos_ref, ntab_ref, cbtab_ref, cnt_ref, carry_scr):
    tt = idx_ref.shape[1]

    @pl.when(pl.program_id(0) == 0)
    def _():
        carry_scr[...] = jnp.zeros(carry_scr.shape, F32)

    eid = lax.broadcasted_iota(I32, (N_EXPERTS, tt), 0)
    before = jnp.where(lax.broadcasted_iota(I32, (tt, tt), 0) < lax.broadcasted_iota(I32, (tt, tt), 1),
                       1.0, 0.0).astype(BF16)
    hits, cums, pres = [], [], []
    n_tile = jnp.zeros((N_EXPERTS, 1), F32)
    for kk in range(TOP_K):
        hit = eid == idx_ref[kk:kk + 1, :]
        hits.append(hit)
        cums.append(_dot(jnp.where(hit, 1.0, 0.0).astype(BF16), before))
        pres.append(n_tile)
        n_tile = n_tile + jnp.sum(jnp.where(hit, 1.0, 0.0), axis=1, keepdims=True)
    n16 = ((n_tile.astype(I32) + (RUN_ALIGN - 1)) >> RUN_SHIFT) << RUN_SHIFT
    lower = jnp.where(lax.broadcasted_iota(I32, (N_EXPERTS, N_EXPERTS), 1)
                      < lax.broadcasted_iota(I32, (N_EXPERTS, N_EXPERTS), 0), 1.0, 0.0).astype(BF16)
    hi = jnp.broadcast_to((n16 >> 8).astype(F32), (N_EXPERTS, 128)).astype(BF16)
    lo = jnp.broadcast_to((n16 & 255).astype(F32), (N_EXPERTS, 128)).astype(BF16)
    lstart = (256.0 * _dot(lower, hi) + _dot(lower, lo))[:, 0:1]
    orow = lax.broadcasted_iota(I32, (8, tt), 0)
    out = jnp.zeros((8, tt), I32)
    for kk in range(TOP_K):
        slot = jnp.sum(jnp.where(hits[kk], lstart + pres[kk] + cums[kk], 0.0), axis=0, keepdims=True)
        out = jnp.where(orow == kk, slot.astype(I32) * ROWS_PER_TOKEN, out)
    lpos_ref[...] = out
    carry = carry_scr[...]
    ntab_ref[...] = jnp.broadcast_to(n_tile, ntab_ref.shape).astype(I32)
    cbtab_ref[...] = carry.astype(I32)
    carry = carry + n_tile
    carry_scr[...] = carry
    cnt_ref[...] = carry


def _rank(idx, *, T):
    tt = MOE_TOK_TILE
    nt = T // tt
    tab = pl.BlockSpec((None, N_EXPERTS, 128), lambda i: (i, 0, 0))
    return pl.pallas_call(
        _rank_kernel,
        grid=(nt,),
        in_specs=[pl.BlockSpec((8, tt), lambda i: (0, i))],
        out_specs=[pl.BlockSpec((8, tt), lambda i: (0, i)), tab, tab,
                   pl.BlockSpec((N_EXPERTS, 128), lambda i: (0, 0))],
        out_shape=[jax.ShapeDtypeStruct((8, T), I32), jax.ShapeDtypeStruct((nt, N_EXPERTS, 128), I32),
                   jax.ShapeDtypeStruct((nt, N_EXPERTS, 128), I32), jax.ShapeDtypeStruct((N_EXPERTS, 128), F32)],
        scratch_shapes=[pltpu.VMEM((N_EXPERTS, 128), F32)],
        compiler_params=_cparams(sem=("arbitrary",)),
        name="moe_rank",
    )(idx)


def _run_chunk(src_ref, src_tok, dst_ref, dst_tok, sem):
    return pltpu.make_async_copy(src_ref.at[_token_rows(src_tok, RUN_ALIGN), :],
                                 dst_ref.at[_token_rows(dst_tok, RUN_ALIGN), :], sem)


def _wait_chunks(src_ref, dst_ref, sem, count):
    def body(_, carry):
        _run_chunk(src_ref, 0, dst_ref, 0, sem).wait()
        return carry
    lax.fori_loop(0, count, body, 0)


def _copy_runs(step, pstart_ref, ntab_ref, cbtab_ref, chunk):
    def per_expert(e, carry):
        src0, total = carry
        nch = (ntab_ref[step * N_EXPERTS + e] + (RUN_ALIGN - 1)) >> RUN_SHIFT
        dst0 = pstart_ref[e] + cbtab_ref[step * N_EXPERTS + e]

        def one(c, cc):
            chunk(src0 + c * RUN_ALIGN, dst0 + c * RUN_ALIGN).start()
            return cc
        lax.fori_loop(0, nch, one, 0)
        return src0 + nch * RUN_ALIGN, total + nch
    return lax.fori_loop(0, N_EXPERTS, per_expert, (0, 0))[1]


def _dispatch_kernel(pstart_ref, cnt_ref, pad_ref, ntab_ref, cbtab_ref, lpos_ref, h_ref, xs_ref,
                     stage, zero_scr, sems, sent):
    step = pl.program_id(0)

    @pl.when(step == 0)
    def _():
        sem = sems.at[0]
        zero_scr[...] = jnp.zeros(zero_scr.shape, F32)

        def per_expert(e, total):
            first = pstart_ref[e] + ((cnt_ref[e] >> RUN_SHIFT) << RUN_SHIFT)
            n_fill = (pstart_ref[e] + pad_ref[e] - first) >> RUN_SHIFT

            def fill(c, carry):
                _run_chunk(zero_scr, 0, xs_ref, first + c * RUN_ALIGN, sem).start()
                return carry
            lax.fori_loop(0, n_fill, fill, 0)
            return total + n_fill
        n_total = lax.fori_loop(0, N_EXPERTS, per_expert, 0)
        used = pstart_ref[N_EXPERTS - 1] + pad_ref[N_EXPERTS - 1]
        n_tail = (xs_ref.shape[0] // ROWS_PER_TOKEN - used) >> RUN_SHIFT

        def fill_tail(c, carry):
            _run_chunk(zero_scr, 0, xs_ref, used + c * RUN_ALIGN, sem).start()
            return carry
        lax.fori_loop(0, n_tail, fill_tail, 0)
        _wait_chunks(zero_scr, xs_ref, sem, n_total + n_tail)
        sent[0] = 0
        sent[1] = 0

    slot = step % 2
    buf = stage.at[slot]
    sem = sems.at[slot]

    def zero_last_group(e, src0):
        n_up = ((ntab_ref[step * N_EXPERTS + e] + (RUN_ALIGN - 1)) >> RUN_SHIFT) << RUN_SHIFT
        buf[_token_rows(jnp.maximum(src0 + n_up - RUN_ALIGN, 0), RUN_ALIGN), :] = jnp.zeros((CHUNK_ROWS, 128), F32)
        return src0 + n_up
    lax.fori_loop(0, N_EXPERTS, zero_last_group, 0)
    tt = MOE_TOK_TILE
    for kk in range(TOP_K):
        def place(t, carry, kk=kk):
            row = pl.multiple_of(lpos_ref[kk * tt + t], ROWS_PER_TOKEN)
            buf[pl.ds(row, ROWS_PER_TOKEN), :] = h_ref[_token_rows(t), :]
            return carry
        lax.fori_loop(0, tt, place, 0, unroll=8)
    _wait_chunks(stage.at[1 - slot], xs_ref, sems.at[1 - slot], sent[1 - slot])
    sent[1 - slot] = 0
    sent[slot] = _copy_runs(step, pstart_ref, ntab_ref, cbtab_ref,
                            lambda s, d: _run_chunk(buf, s, xs_ref, d, sem))

    @pl.when(step == pl.num_programs(0) - 1)
    def _():
        _wait_chunks(buf, xs_ref, sem, sent[slot])


def _smem_tile_spec(tt):
    return pl.BlockSpec((8 * tt,), lambda i, *_: (i,), memory_space=pltpu.SMEM)


def _tile_major(table):
    tt = MOE_TOK_TILE
    return table.reshape(table.shape[0], -1, tt).transpose(1, 0, 2).reshape(-1)


def _dispatch(pstart, cnt, padded, ntab, cbtab, lpos, h2t, *, T, P):
    tt = MOE_TOK_TILE
    r = ROWS_PER_TOKEN
    return pl.pallas_call(
        _dispatch_kernel,
        grid_spec=pltpu.PrefetchScalarGridSpec(
            num_scalar_prefetch=5,
            grid=(T // tt,),
            in_specs=[_smem_tile_spec(tt), pl.BlockSpec((tt * r, 128), lambda i, *_: (i, 0))],
            out_specs=pl.BlockSpec(memory_space=pl.ANY),
            scratch_shapes=[pltpu.VMEM((2, STAGE_TOKENS * r, 128), F32), pltpu.VMEM((CHUNK_ROWS, 128), F32),
                            pltpu.SemaphoreType.DMA((2,)), pltpu.SMEM((2,), I32)],
        ),
        out_shape=jax.ShapeDtypeStruct((P * r, 128), F32),
        compiler_params=_cparams(sem=("arbitrary",), vmem=VMEM_LIMIT),
        name="moe_dispatch",
    )(pstart, cnt, padded, ntab, cbtab, _tile_major(lpos), h2t)


def _collect_kernel(pstart_ref, ntab_ref, cbtab_ref, lpos_ref, gate_ref, ys_ref, f_ref, stage, sems, sent):
    step = pl.program_id(0)
    slot = step % 2

    def fetch(tile, b):
        sent[b] = _copy_runs(tile, pstart_ref, ntab_ref, cbtab_ref,
                             lambda s, d: _run_chunk(ys_ref, d, stage.at[b], s, sems.at[b]))

    @pl.when(step == 0)
    def _():
        fetch(0, 0)

    @pl.when(step + 1 < pl.num_programs(0))
    def _():
        fetch(step + 1, 1 - slot)

    buf = stage.at[slot]
    _wait_chunks(ys_ref, buf, sems.at[slot], sent[slot])

    tt = MOE_TOK_TILE

    def combine(t, carry):
        acc = None
        for kk in range(TOP_K):
            row = pl.multiple_of(lpos_ref[kk * tt + t], ROWS_PER_TOKEN)
            term = gate_ref[kk * tt + t] * buf[pl.ds(row, ROWS_PER_TOKEN), :]
            acc = term if acc is None else acc + term
        f_ref[_token_rows(t), :] = acc
        return carry
    lax.fori_loop(0, tt, combine, 0, unroll=4)


def _collect(pstart, ntab, cbtab, lpos, gates, ys, *, T):
    tt = MOE_TOK_TILE
    r = ROWS_PER_TOKEN
    return pl.pallas_call(
        _collect_kernel,
        grid_spec=pltpu.PrefetchScalarGridSpec(
            num_scalar_prefetch=3,
            grid=(T // tt,),
            in_specs=[_smem_tile_spec(tt), _smem_tile_spec(tt), pl.BlockSpec(memory_space=pl.ANY)],
            out_specs=pl.BlockSpec((tt * r, 128), lambda i, *_: (i, 0)),
            scratch_shapes=[pltpu.VMEM((2, STAGE_TOKENS * r, 128), F32), pltpu.SemaphoreType.DMA((2,)),
                            pltpu.SMEM((2,), I32)],
        ),
        out_shape=jax.ShapeDtypeStruct((T * r, 128), F32),
        compiler_params=_cparams(sem=("arbitrary",), vmem=VMEM_LIMIT),
        name="moe_collect",
    )(pstart, ntab, cbtab, _tile_major(lpos), _tile_major(gates), ys)


def _deinterleave_matrix():
    m = np.zeros((256, 256), np.float32)
    j = np.arange(128)
    m[2 * j, j] = 1.0
    m[2 * j + 1, 128 + j] = 1.0
    return jnp.asarray(m, dtype=BF16)


def _expert_kernel(be_ref, nu_ref, x_ref, w1_ref, b1_ref, w2_ref, b2_ref, perm_ref, y_ref, w1_scr, w2_scr):
    i = pl.program_id(0)

    @pl.when(i >= nu_ref[0])
    def _():
        y_ref[...] = jnp.zeros(y_ref.shape, F32)

    @pl.when(i < nu_ref[0])
    def _():
        @pl.when((i == 0) | (be_ref[i] != be_ref[jnp.maximum(i - 1, 0)]))
        def _():
            half = w1_scr.shape[1] // 2
            for m in range(w1_ref.shape[1] // 256):
                r = _dot(w1_ref[:, 256 * m:256 * (m + 1)].astype(BF16), perm_ref[...])
                w1_scr[:, 128 * m:128 * (m + 1)] = r[:, :128].astype(BF16)
                w1_scr[:, half + 128 * m:half + 128 * (m + 1)] = r[:, 128:].astype(BF16)
            w2_scr[...] = w2_ref[...].astype(BF16)

        x = _from_token_tiles(x_ref, MOE_BLK).astype(BF16)
        half = w1_scr.shape[1] // 2
        y = None
        for c0 in range(0, half, EXPERT_CHUNK):
            c1 = c0 + EXPERT_CHUNK
            glu = _dot(x, w1_scr[:, c0:c1]) + b1_ref[:, c0:c1]
            lin = _dot(x, w1_scr[:, half + c0:half + c1]) + b1_ref[:, half + c0:half + c1]
            glu = jnp.minimum(glu, SWIGLU_LIMIT)
            lin = jnp.clip(lin, -SWIGLU_LIMIT, SWIGLU_LIMIT)
            act = glu * jax.nn.sigmoid(SWIGLU_ALPHA * glu) * (lin + 1.0)
            part = _dot(act.astype(BF16), w2_scr[c0:c1, :])
            y = part + b2_ref[...] if y is None else y + part
        _to_token_tiles(y_ref, y)


def _experts(blk_e, n_used, xs, w1, b1, w2, b2, *, P, layer):
    nblk = P // MOE_BLK
    r = ROWS_PER_TOKEN
    d, two_f = w1.shape[2:]
    f = w2.shape[2]
    rows = lambda i, be, nu: (jnp.minimum(i, nu[0] - 1), 0)
    return pl.pallas_call(
        _expert_kernel,
        grid_spec=pltpu.PrefetchScalarGridSpec(
            num_scalar_prefetch=2,
            grid=(nblk,),
            in_specs=[
                pl.BlockSpec((MOE_BLK * r, 128), rows),
                pl.BlockSpec((None, None, d, two_f), lambda i, be, nu: (layer, be[i], 0, 0)),
                pl.BlockSpec((None, 1, two_f), lambda i, be, nu: (be[i], 0, 0)),
                pl.BlockSpec((None, None, f, d), lambda i, be, nu: (layer, be[i], 0, 0)),
                pl.BlockSpec((None, 1, d), lambda i, be, nu: (be[i], 0, 0)),
                pl.BlockSpec((256, 256), lambda i, be, nu: (0, 0)),
            ],
            out_specs=pl.BlockSpec((MOE_BLK * r, 128), lambda i, be, nu: (i, 0)),
            scratch_shapes=[pltpu.VMEM((d, two_f), BF16), pltpu.VMEM((f, d), BF16)],
        ),
        out_shape=jax.ShapeDtypeStruct((P * r, 128), F32),
        compiler_params=_cparams(sem=("arbitrary",), vmem=EXPERT_VMEM_LIMIT),
        name="moe_experts",
    )(blk_e, n_used, xs, w1, b1, w2, b2, _deinterleave_matrix())


def _moe(h2t, idx, gates, w1, b1, w2, b2, *, T, layer):
    lpos, ntab, cbtab, cnt = _rank(idx, T=T)
    counts = cnt[:, 0].astype(I32)
    padded = (counts + RUN_ALIGN + MOE_BLK - 1) // MOE_BLK * MOE_BLK
    pad_end = jnp.cumsum(padded)
    pstart = (pad_end - padded).astype(I32)
    nblk = TOP_K * T // MOE_BLK + 2 * N_EXPERTS
    P = nblk * MOE_BLK
    starts = jnp.arange(nblk, dtype=I32) * MOE_BLK
    blk_e = jnp.minimum(jnp.sum((pad_end[None, :] <= starts[:, None]).astype(I32), axis=1), N_EXPERTS - 1)
    n_used = (pad_end[-1:] // MOE_BLK).astype(I32)
    ntab = ntab[:, :, 0].reshape(-1)
    cbtab = cbtab[:, :, 0].reshape(-1)
    xs = _dispatch(pstart, counts, padded.astype(I32), ntab, cbtab, lpos, h2t, T=T, P=P)
    ys = _experts(blk_e.astype(I32), n_used, xs, w1, b1, w2, b2, P=P, layer=layer)
    return _collect(pstart, ntab, cbtab, lpos, gates, ys, T=T)


def _residual_kernel(f_ref, x_ref, g2_ref, lg_ref, lb_ref, o_ref, *, nb, n_lat_tiles, alpha):
    for b in range(nb):
        f = _from_token_tiles(f_ref.at[b], TILE)
        z = alpha * x_ref[b] + g2_ref[_mod_row(b, n_lat_tiles, nb)] * f
        o_ref[b] = _normalize(z) * lg_ref[...] + lb_ref[...]


def _residual(ft, x1, g2, ln_g, ln_b, *, B, S, n_out_tiles, n_lat_tiles, alpha):
    r = ROWS_PER_TOKEN
    vec = pl.BlockSpec((1, D_MODEL), lambda i: (0, 0))
    return pl.pallas_call(
        functools.partial(_residual_kernel, nb=B, n_lat_tiles=n_lat_tiles, alpha=alpha),
        grid=(n_out_tiles,),
        in_specs=[
            pl.BlockSpec((B, TILE * r, 128), lambda i: (0, i, 0)),
            pl.BlockSpec((B, TILE, D_MODEL), lambda i: (0, i, 0)),
            _mod_spec(), vec, vec,
        ],
        out_specs=pl.BlockSpec((B, TILE, D_MODEL), lambda i: (0, i, 0)),
        out_shape=jax.ShapeDtypeStruct((B, n_out_tiles * TILE, D_MODEL), F32),
        compiler_params=_cparams(vmem=VMEM_LIMIT),
        name="ffn_residual_norm",
    )(ft.reshape(B, S * r, 128), x1, g2, ln_g, ln_b)


def _rope_tables(n_lat, n_ctx):
    half = HEAD_DIM // 2
    pos = np.arange(n_lat)
    inv = 1.0 / (ROPE_BASE ** (np.arange(0, half, 2, dtype=np.float64) / half))
    ang_row = (pos // GRID_W)[:, None].astype(np.float64) * inv[None, :]
    ang_col = (pos % GRID_W)[:, None].astype(np.float64) * inv[None, :]
    ang = np.concatenate([ang_row, ang_row, ang_col, ang_col], axis=1)
    sign = np.concatenate([-np.ones(16), np.ones(16)] * 2)[None, :]
    cos = np.concatenate([np.cos(ang), np.ones((n_ctx, HEAD_DIM))], axis=0)
    sin = np.concatenate([np.sin(ang) * sign, np.zeros((n_ctx, HEAD_DIM))], axis=0)
    return (jnp.asarray(np.tile(cos, (1, 2)), dtype=F32), jnp.asarray(np.tile(sin, (1, 2)), dtype=F32))


def _extend_w_in(w):
    g = MIX_GROUP_W
    hd = HEAD_DIM
    a, bp, q = w[:, 0:g], w[:, g:2 * g], w[:, 2 * g:3 * g]
    k, v = w[:, 3 * g:3 * g + 2 * hd], w[:, 3 * g + 2 * hd:4 * g]
    rest = w[:, 4 * g:]
    dup = lambda t: jnp.concatenate([t[:, :hd], t[:, :hd], t[:, hd:], t[:, hd:]], axis=1)
    return jnp.concatenate([a, bp, q, dup(k), dup(v), rest], axis=1).astype(BF16)


def _block_diag(pw):
    n, c, _ = pw.shape
    out = jnp.zeros((n * c, n * c), pw.dtype)
    for gi in range(n):
        out = lax.dynamic_update_slice(out, pw[gi], (gi * c, gi * c))
    return out


def kernel(x, c, ctx, c_ctx, w_mod, b_mod, w_in, pool_w, pool_scale, attn_sink, ret_decay, w_out, ln_g, ln_b,
           router_w, router_b, exp_w1, exp_b1, exp_w2, exp_b2):
    B, N, D = x.shape
    L = ctx.shape[1]
    depth = w_mod.shape[0]
    assert D == D_MODEL and L == TILE and N % TILE == 0 and N % (FFT_RADIX2 * 16) == 0 and B < 8
    S = N + L
    n_lat_tiles = N // TILE
    T = B * S
    assert T % MOE_TOK_TILE == 0
    alpha = float((2 * depth) ** 0.25)

    cvec = jnp.zeros((8, D), F32).at[:B].set(c).at[B].set(c_ctx)
    mods = _modulation(cvec, w_mod, b_mod)
    cos, sin = _rope_tables(N, L)

    xs = (x, ctx)
    out = None
    for l in range(depth):
        last = l == depth - 1
        m = mods[l].reshape(8, 6, 1, D)
        sh1, sc1, g1, sh2, sc2, g2 = [m[:, j] for j in range(6)]
        proj = _projection(xs, sc1, sh1, _extend_w_in(w_in[l]), cos, sin, B=B, S=S, n_lat_tiles=n_lat_tiles)
        a, bp, q, k2, v2, rq, rk, rv, rg = proj
        y_f = _fourier_mix(a, B=B, N=N, L=L, with_ctx=not last)
        y_p = _pool_mix(bp, _block_diag(pool_w[l]).astype(BF16), pool_scale[l].reshape(1, -1), B=B, N=N, L=L)
        y_a = _attention(attn_sink[l], q, k2, v2, B=B, N=N, L=L)
        dec = jnp.repeat(ret_decay[l], MIX_GROUP_W // RET_HEADS, axis=1)
        y_r = _retention(dec, rq, rk, rv, rg, B=B, N=N, L=L)
        x1, h2t, idx, gates = _merge(
            xs, (y_f, y_p, y_a, y_r), g1, sc2, sh2, ln_g[l, 0:1], ln_b[l, 0:1], w_out[l].astype(BF16),
            router_w[l].T.astype(BF16), router_b[l].reshape(-1, 1), B=B, S=S, n_lat_tiles=n_lat_tiles, alpha=alpha)
        b1 = jnp.concatenate([exp_b1[l][:, 0::2], exp_b1[l][:, 1::2]], axis=-1)[:, None, :]
        ft = _moe(h2t, idx, gates, exp_w1, b1, exp_w2, exp_b2[l][:, None, :], T=T, layer=l)
        out = _residual(ft, x1, g2, ln_g[l, 1:2], ln_b[l, 1:2], B=B, S=S,
                        n_out_tiles=n_lat_tiles if last else S // TILE, n_lat_tiles=n_lat_tiles, alpha=alpha)
        xs = (out,)
    return out
```

```python
import functools

import numpy as np
import jax
import jax.numpy as jnp
from jax import lax
from jax.experimental import pallas as pl
from jax.experimental.pallas import tpu as pltpu

F32 = jnp.float32
BF16 = jnp.bfloat16
I32 = jnp.int32

D_MODEL = 1024
GRID_W = 64
FOURIER_GROUP_W = 64
POOL_WINDOWS = (2, 4, 8, 16)
HEAD_DIM = 64
WINDOW = 128
ROPE_BASE = 10000.0
RET_HEADS = 4
RET_CHUNK = 128
N_EXPERTS = 32
TOP_K = 4
SWIGLU_LIMIT = 7.0
SWIGLU_ALPHA = 1.702
LN_EPS = 1e-6
NEG_INF = -1e30
MIX_GROUP_W = 256
FFT_RADIX2 = 64

TILE = 256
MOE_BLK = 512
EXPERT_CHUNK = 1024
MOE_TOK_TILE = 512
ROWS_PER_TOKEN = D_MODEL // 128
RUN_ALIGN = 16
RUN_SHIFT = 4
CHUNK_ROWS = RUN_ALIGN * ROWS_PER_TOKEN
STAGE_TOKENS = TOP_K * MOE_TOK_TILE + N_EXPERTS * RUN_ALIGN
VMEM_LIMIT = 48 * 1024 * 1024
EXPERT_VMEM_LIMIT = 56 * 1024 * 1024


def _dot(a, b):
    return jnp.dot(a, b, preferred_element_type=F32)


def _dot_nt(a, b):
    return lax.dot_general(a, b, (((1,), (1,)), ((), ())), preferred_element_type=F32)


def _dot_tn(a, b):
    return lax.dot_general(a, b, (((0,), (0,)), ((), ())), preferred_element_type=F32)


def _normalize(x):
    mu = jnp.mean(x, axis=-1, keepdims=True)
    xc = x - mu
    var = jnp.mean(xc * xc, axis=-1, keepdims=True)
    return xc * lax.rsqrt(var + LN_EPS)


def _cparams(sem=None, vmem=None):
    kw = {}
    if sem is not None:
        kw["dimension_semantics"] = sem
    if vmem is not None:
        kw["vmem_limit_bytes"] = vmem
    return pltpu.CompilerParams(**kw)


def _to_token_tiles(ref, val):
    n = val.shape[0]
    for c in range(ROWS_PER_TOKEN):
        ref[pl.ds(c, n, stride=ROWS_PER_TOKEN), :] = val[:, 128 * c:128 * (c + 1)]


def _from_token_tiles(ref, n):
    return jnp.concatenate([ref[pl.ds(c, n, stride=ROWS_PER_TOKEN), :] for c in range(ROWS_PER_TOKEN)], axis=-1)


def _token_rows(tok, count=1):
    return pl.ds(pl.multiple_of(tok * ROWS_PER_TOKEN, ROWS_PER_TOKEN), count * ROWS_PER_TOKEN)


def _mod_kernel(c_ref, w_ref, b_ref, o_ref):
    c = c_ref[...]
    a = c * jax.nn.sigmoid(c)
    w = w_ref[...]
    a_hi = a.astype(BF16)
    a_lo = (a - a_hi.astype(F32)).astype(BF16)
    w_hi = w.astype(BF16)
    w_lo = (w - w_hi.astype(F32)).astype(BF16)
    o_ref[...] = _dot(a_hi, w_hi) + _dot(a_lo, w_hi) + _dot(a_hi, w_lo) + b_ref[...]


def _modulation(cvec, w_mod, b_mod):
    depth, d, six_d = w_mod.shape
    nb = 1536
    return pl.pallas_call(
        _mod_kernel,
        grid=(depth, six_d // nb),
        in_specs=[
            pl.BlockSpec((8, d), lambda l, j: (0, 0)),
            pl.BlockSpec((None, d, nb), lambda l, j: (l, 0, j)),
            pl.BlockSpec((None, 1, nb), lambda l, j: (l, 0, j)),
        ],
        out_specs=pl.BlockSpec((None, 8, nb), lambda l, j: (l, 0, j)),
        out_shape=jax.ShapeDtypeStruct((depth, 8, six_d), F32),
        compiler_params=_cparams(vmem=VMEM_LIMIT),
        name="modulation",
    )(cvec, w_mod, b_mod.reshape(depth, 1, six_d))


def _tile_inputs(refs, b, split_input, n_lat_tiles):
    if split_input:
        x = jnp.where(pl.program_id(0) == n_lat_tiles, refs[1][b], refs[0][b])
        return x, refs[2:]
    return refs[0][b], refs[1:]


def _mod_row(b, n_lat_tiles, ctx_row):
    return jnp.where(pl.program_id(0) == n_lat_tiles, ctx_row, b)


def _proj_kernel(*refs, nb, split_input, n_lat_tiles):
    g = MIX_GROUP_W
    for b in range(nb):
        x, rest = _tile_inputs(refs, b, split_input, n_lat_tiles)
        sc_ref, sh_ref, w_ref, cos_ref, sin_ref = rest[:5]
        a_ref, bp_ref, q_ref, k_ref, v_ref, rq_ref, rk_ref, rv_ref, rg_ref = rest[5:]
        row = _mod_row(b, n_lat_tiles, nb)
        h = _normalize(x) * (1.0 + sc_ref[row]) + sh_ref[row]
        p = _dot(h.astype(BF16), w_ref[...])
        a_ref[b] = p[:, 0:g].astype(BF16)
        bp_ref[b] = p[:, g:2 * g].astype(BF16)
        qk = p[:, 2 * g:4 * g]
        lane = lax.broadcasted_iota(I32, qk.shape, 1)
        first = (lane & 16) == 0
        partner = jnp.where(first, pltpu.roll(qk, 2 * g - 16, 1), pltpu.roll(qk, 16, 1))
        cos = jnp.concatenate([cos_ref[...]] * 4, axis=-1)
        sin = jnp.concatenate([sin_ref[...]] * 4, axis=-1)
        qk = qk * cos + partner * sin
        q_ref[b] = (qk[:, :g] * (HEAD_DIM ** -0.5)).astype(BF16)
        k_ref[b] = qk[:, g:].astype(BF16)
        v_ref[b] = p[:, 4 * g:5 * g].astype(BF16)
        rq_ref[b] = p[:, 5 * g:6 * g].astype(BF16)
        rk_ref[b] = p[:, 6 * g:7 * g].astype(BF16)
        rv_ref[b] = p[:, 7 * g:8 * g].astype(BF16)
        rg_ref[b] = p[:, 8 * g:9 * g].astype(BF16)


def _mod_spec():
    return pl.BlockSpec((8, 1, D_MODEL), lambda i: (0, 0, 0))


def _x_specs(B, split_input, n_lat_tiles):
    if split_input:
        return [
            pl.BlockSpec((B, TILE, D_MODEL), lambda i: (0, jnp.minimum(i, n_lat_tiles - 1), 0)),
            pl.BlockSpec((B, TILE, D_MODEL), lambda i: (0, 0, 0)),
        ]
    return [pl.BlockSpec((B, TILE, D_MODEL), lambda i: (0, i, 0))]


def _projection(xs, sc, sh, w_ext, cos, sin, *, B, S, n_lat_tiles):
    split_input = len(xs) == 2
    nt = S // TILE
    g = MIX_GROUP_W
    out_spec = pl.BlockSpec((B, TILE, g), lambda i: (0, i, 0))
    out_shape = jax.ShapeDtypeStruct((B, S, g), BF16)
    return pl.pallas_call(
        functools.partial(_proj_kernel, nb=B, split_input=split_input, n_lat_tiles=n_lat_tiles),
        grid=(nt,),
        in_specs=_x_specs(B, split_input, n_lat_tiles) + [
            _mod_spec(),
            _mod_spec(),
            pl.BlockSpec(w_ext.shape, lambda i: (0, 0)),
            pl.BlockSpec((TILE, 128), lambda i: (i, 0)),
            pl.BlockSpec((TILE, 128), lambda i: (i, 0)),
        ],
        out_specs=[out_spec] * 9,
        out_shape=[out_shape] * 9,
        compiler_params=_cparams(vmem=VMEM_LIMIT),
        name="ln_mod_in_proj",
    )(*xs, sc, sh, w_ext, cos, sin)


def _channel_dft_matrix():
    c = np.arange(FOURIER_GROUP_W)
    ang = 2.0 * np.pi * np.outer(c, c) / FOURIER_GROUP_W
    eye = np.eye(MIX_GROUP_W // FOURIER_GROUP_W)
    cs = np.concatenate([np.kron(eye, np.cos(ang)), -np.kron(eye, np.sin(ang))], axis=1)
    return jnp.asarray(cs, dtype=BF16)


def _seq_dft_stage_a(n1):
    n2 = FFT_RADIX2
    n = n1 * n2
    k1 = np.arange(n1)[None, :, None]
    i1 = np.arange(n1)[None, None, :]
    i2 = np.arange(n2)[:, None, None]
    th = 2.0 * np.pi * ((k1 * (n2 * i1 + i2)) % n) / n
    wr, wi = np.cos(th), -np.sin(th)
    wa = np.concatenate([np.concatenate([wr, -wi], axis=2), np.concatenate([wi, wr], axis=2)], axis=1)
    return jnp.asarray(wa, dtype=BF16)


def _seq_dft_stage_b():
    n2 = FFT_RADIX2
    k2 = np.arange(n2)
    ph = 2.0 * np.pi * (np.outer(k2, k2) % n2) / n2
    return jnp.asarray(np.concatenate([np.cos(ph), np.sin(ph)], axis=1), dtype=BF16)


def _dense_dft(length):
    k = np.arange(length)
    th = 2.0 * np.pi * (np.outer(k, k) % length) / length
    return jnp.asarray(np.concatenate([np.cos(th), np.sin(th)], axis=1), dtype=BF16)


def _fft_a_kernel(a_ref, cs_ref, wa_ref, t_ref, *, nb, n1):
    g = MIX_GROUP_W
    for b in range(nb):
        p = _dot(a_ref[b], cs_ref[...])
        pp = jnp.concatenate([p[:, :g], p[:, g:]], axis=0).astype(BF16)
        t = _dot(wa_ref[...], pp)
        t_ref[b] = jnp.concatenate([t[:n1], t[n1:]], axis=-1).astype(BF16)


def _fft_b_kernel(t_ref, vb_ref, y_ref, *, nb, scale):
    g = MIX_GROUP_W
    n2 = FFT_RADIX2
    for b in range(nb):
        t = t_ref[b]
        tt = jnp.concatenate([t[:, :g], t[:, g:]], axis=0)
        y = _dot(vb_ref[...], tt) * scale
        y_ref[b, 0:n2, :] = y.astype(BF16)
        y_ref[b, n2:, :] = jnp.zeros((y_ref.shape[1] - n2, g), BF16)


def _fft_ctx_kernel(a_ref, cs_ref, wc_ref, yin_ref, y_ref, *, scale):
    del yin_ref
    g = MIX_GROUP_W
    p = _dot(a_ref[...], cs_ref[...])
    pp = jnp.concatenate([p[:, :g], p[:, g:]], axis=0).astype(BF16)
    y_ref[...] = (_dot(wc_ref[...], pp) * scale).astype(BF16)


def _fourier_mix(a, *, B, N, L, with_ctx):
    S = N + L
    g = MIX_GROUP_W
    n2 = FFT_RADIX2
    n1 = N // n2
    cs = _channel_dft_matrix()
    a2 = a.reshape(B, S // n2, n2 * g)
    t = pl.pallas_call(
        functools.partial(_fft_a_kernel, nb=B, n1=n1),
        grid=(n2,),
        in_specs=[
            pl.BlockSpec((B, n1, g), lambda j: (0, 0, j)),
            pl.BlockSpec(cs.shape, lambda j: (0, 0)),
            pl.BlockSpec((None, 2 * n1, 2 * n1), lambda j: (j, 0, 0)),
        ],
        out_specs=pl.BlockSpec((B, n1, 2 * g), lambda j: (0, 0, j)),
        out_shape=jax.ShapeDtypeStruct((B, n1, n2 * 2 * g), BF16),
        name="fourier_seq_stage_a",
    )(a2, cs, _seq_dft_stage_a(n1))
    t = t.reshape(B, n1 * n2, 2 * g)
    rows = S // n1
    y = pl.pallas_call(
        functools.partial(_fft_b_kernel, nb=B, scale=float((N * FOURIER_GROUP_W) ** -0.5)),
        grid=(n1,),
        in_specs=[
            pl.BlockSpec((B, n2, 2 * g), lambda k: (0, k, 0)),
            pl.BlockSpec((n2, 2 * n2), lambda k: (0, 0)),
        ],
        out_specs=pl.BlockSpec((B, rows, g), lambda k: (0, 0, k)),
        out_shape=jax.ShapeDtypeStruct((B, rows, n1 * g), BF16),
        name="fourier_seq_stage_b",
    )(t, _seq_dft_stage_b())
    y = y.reshape(B, S, g)
    if not with_ctx:
        return y
    ctx_tile = N // TILE
    return pl.pallas_call(
        functools.partial(_fft_ctx_kernel, scale=float((L * FOURIER_GROUP_W) ** -0.5)),
        grid=(B,),
        in_specs=[
            pl.BlockSpec((None, TILE, g), lambda b: (b, ctx_tile, 0)),
            pl.BlockSpec(cs.shape, lambda b: (0, 0)),
            pl.BlockSpec((L, 2 * L), lambda b: (0, 0)),
            pl.BlockSpec(memory_space=pl.ANY),
        ],
        out_specs=pl.BlockSpec((None, TILE, g), lambda b: (b, ctx_tile, 0)),
        out_shape=jax.ShapeDtypeStruct((B, S, g), BF16),
        input_output_aliases={3: 0},
        name="fourier_ctx",
    )(a, cs, _dense_dft(L), y)


POOL_HALO = 16


def _pool_bands():
    t = np.arange(TILE)[:, None]
    s = np.arange(TILE + 2 * POOL_HALO)[None, :] - POOL_HALO
    bands = [((s - t >= -(w // 2)) & (s - t < w // 2)).astype(np.float32) for w in POOL_WINDOWS]
    return jnp.asarray(np.stack(bands), dtype=BF16)


def _pool_kernel(prev_ref, x_ref, next_ref, band_ref, pw_ref, ps_ref, o_ref, *, nb, n_lat, s_tot):
    i = pl.program_id(0)
    t0 = i * TILE
    is_ctx = t0 >= n_lat
    r0 = jnp.where(is_ctx, n_lat, 0)
    r1 = jnp.where(is_ctx, s_tot, n_lat)
    kw = TILE + 2 * POOL_HALO
    s_pos = t0 - POOL_HALO + lax.broadcasted_iota(I32, (kw, 1), 0)
    inside = jnp.where(s_pos >= r0, jnp.where(s_pos < r1, 1.0, 0.0), 0.0).astype(BF16)
    xc = jnp.concatenate([jnp.concatenate([prev_ref[b], x_ref[b], next_ref[b]], axis=0) * inside
                          for b in range(nb)], axis=-1)
    g = MIX_GROUP_W
    t_col = t0 + lax.broadcasted_iota(I32, (TILE, 1), 0)
    group = (lax.broadcasted_iota(I32, (TILE, nb * g), 1) // FOURIER_GROUP_W) % len(POOL_WINDOWS)
    acc = jnp.zeros((TILE, nb * g), F32)
    for gi, w in enumerate(POOL_WINDOWS):
        lo = jnp.maximum(t_col - w // 2, r0)
        hi = jnp.minimum(t_col + w // 2, r1)
        mean = _dot(band_ref[gi], xc) / (hi - lo).astype(F32)
        acc = jnp.where(group == gi, mean, acc)
    for b in range(nb):
        pooled = acc[:, g * b:g * (b + 1)] - x_ref[b].astype(F32)
        o_ref[b] = (_dot(pooled.astype(BF16), pw_ref[...]) * ps_ref[...]).astype(BF16)


def _pool_mix(bp, pw_bd, pscale, *, B, N, L):
    S = N + L
    nt = S // TILE
    g = MIX_GROUP_W
    hb = TILE // POOL_HALO
    last = S // POOL_HALO - 1
    return pl.pallas_call(
        functools.partial(_pool_kernel, nb=B, n_lat=N, s_tot=S),
        grid=(nt,),
        in_specs=[
            pl.BlockSpec((B, POOL_HALO, g), lambda i: (0, jnp.maximum(i * hb - 1, 0), 0)),
            pl.BlockSpec((B, TILE, g), lambda i: (0, i, 0)),
            pl.BlockSpec((B, POOL_HALO, g), lambda i: (0, jnp.minimum((i + 1) * hb, last), 0)),
            pl.BlockSpec((len(POOL_WINDOWS), TILE, TILE + 2 * POOL_HALO), lambda i: (0, 0, 0)),
            pl.BlockSpec((g, g), lambda i: (0, 0)),
            pl.BlockSpec((1, g), lambda i: (0, 0)),
        ],
        out_specs=pl.BlockSpec((B, TILE, g), lambda i: (0, i, 0)),
        out_shape=jax.ShapeDtypeStruct((B, S, g), BF16),
        name="pool_mix",
    )(bp, bp, bp, _pool_bands(), pw_bd, pscale)


def _attn_window_bias(n_ctx):
    r = (np.arange(2 * TILE) % TILE)[:, None]
    c = np.arange(TILE + 2 * WINDOW)[None, :] - WINDOW
    band = np.where(np.abs(r - c) <= WINDOW, 0.0, NEG_INF)
    return jnp.asarray(np.concatenate([band, np.zeros((2 * TILE, n_ctx))], axis=1), dtype=F32)


def _attn_kernel(sink_ref, bias_ref, q_ref, kp_ref, kc_ref, kn_ref, kx_ref, vp_ref, vc_ref, vn_ref, vx_ref, o_ref,
                 *, nb, n_lat):
    i = pl.program_id(0)
    t0 = i * TILE
    nband = TILE + 2 * WINDOW
    nkeys = nband + kx_ref.shape[1]
    col = lax.broadcasted_iota(I32, (1, nkeys), 1)
    kpos = t0 - WINDOW + col
    valid = (col >= nband) | ((kpos >= 0) & (kpos < n_lat) & (t0 < n_lat))
    bias = bias_ref[...]
    lane = lax.broadcasted_iota(I32, (TILE, 2 * HEAD_DIM), 1)
    low = lane < HEAD_DIM
    keep_low = jnp.where(low, 1.0, 0.0).astype(BF16)
    keep_high = jnp.where(low, 0.0, 1.0).astype(BF16)
    row1 = lax.broadcasted_iota(I32, (2 * TILE, 1), 0)
    for b in range(nb):
        outs = []
        for h in range(2):
            sl = slice(2 * HEAD_DIM * h, 2 * HEAD_DIM * (h + 1))
            qh = q_ref[b, :, sl]
            qs = jnp.concatenate([qh * keep_low, qh * keep_high], axis=0)
            kh = jnp.concatenate([kp_ref[b, :, sl], kc_ref[b, :, sl], kn_ref[b, :, sl], kx_ref[b, :, sl]], axis=0)
            vh = jnp.concatenate([vp_ref[b, :, sl], vc_ref[b, :, sl], vn_ref[b, :, sl], vx_ref[b, :, sl]], axis=0)
            s = jnp.where(valid, _dot_nt(qs, kh) + bias, NEG_INF)
            sink = jnp.where(row1 < TILE, sink_ref[2 * h], sink_ref[2 * h + 1])
            m = jnp.maximum(jnp.max(s, axis=-1, keepdims=True), sink)
            p = jnp.exp(s - m)
            den = jnp.sum(p, axis=-1, keepdims=True) + jnp.exp(sink - m)
            o = _dot(p.astype(BF16), vh) / den
            outs.append(jnp.where(low, o[:TILE], o[TILE:]))
        o_ref[b] = jnp.concatenate(outs, axis=-1).astype(BF16)


def _attention(sinks, q, k2, v2, *, B, N, L):
    S = N + L
    nt = S // TILE
    g = MIX_GROUP_W
    hb = TILE // WINDOW
    last = N // WINDOW - 1
    ctx_tile = N // TILE
    prev = pl.BlockSpec((B, WINDOW, g), lambda i: (0, jnp.clip(i * hb - 1, 0, last), 0))
    cur = pl.BlockSpec((B, TILE, g), lambda i: (0, i, 0))
    nxt = pl.BlockSpec((B, WINDOW, g), lambda i: (0, jnp.minimum((i + 1) * hb, last), 0))
    ctx = pl.BlockSpec((B, L, g), lambda i: (0, ctx_tile, 0))
    return pl.pallas_call(
        functools.partial(_attn_kernel, nb=B, n_lat=N),
        grid=(nt,),
        in_specs=[pl.BlockSpec(memory_space=pltpu.SMEM),
                  pl.BlockSpec((2 * TILE, TILE + 2 * WINDOW + L), lambda i: (0, 0)),
                  cur, prev, cur, nxt, ctx, prev, cur, nxt, ctx],
        out_specs=pl.BlockSpec((B, TILE, g), lambda i: (0, i, 0)),
        out_shape=jax.ShapeDtypeStruct((B, S, g), BF16),
        compiler_params=_cparams(vmem=VMEM_LIMIT),
        name="banded_attention",
    )(sinks, _attn_window_bias(L), q, k2, k2, k2, k2, v2, v2, v2, v2)


def _ret_kernel(*refs, reverse, nb):
    if reverse:
        dec_ref, q_ref, k_ref, v_ref, of_ref, g_ref, o_ref, r_scr = refs
    else:
        dec_ref, q_ref, k_ref, v_ref, o_ref, r_scr = refs
    c = RET_CHUNK
    w = MIX_GROUP_W
    dh = w // RET_HEADS

    @pl.when(pl.program_id(0) == 0)
    def _():
        r_scr[...] = jnp.zeros(r_scr.shape, F32)

    lg = -jnp.exp(dec_ref[...])
    idx = lax.broadcasted_iota(I32, (c, 1), 0).astype(F32)
    jdx = lax.broadcasted_iota(I32, (1, c), 1).astype(F32)
    if reverse:
        idx, jdx = (c - 1.0) - idx, (c - 1.0) - jdx
    diff = idx - jdx
    dmask_all = jnp.concatenate(
        [jnp.where(diff >= 0, jnp.exp(jnp.maximum(diff, 0.0) * lg[:, dh * h:dh * h + 1]), 0.0)
         for h in range(RET_HEADS)], axis=1)
    xi = jnp.exp((idx + 1.0) * lg)
    zeta = jnp.exp((c - 1.0 - idx) * lg)
    chunk_decay = jnp.exp(float(c) * lg)
    lane_head = lax.broadcasted_iota(I32, (1, w), 1) // dh
    keep_head = [jnp.where(lane_head == h, 1.0, 0.0).astype(BF16) for h in range(RET_HEADS)]
    same_head = (lax.broadcasted_iota(I32, (w, w), 0) // dh) == (lax.broadcasted_iota(I32, (w, w), 1) // dh)

    for b in range(nb):
        q = q_ref[b]
        k = k_ref[b] * jnp.asarray(dh ** -0.5, BF16)
        v = v_ref[b]
        state = r_scr[b]
        k_heads = jnp.concatenate([k * keep_head[h] for h in range(RET_HEADS)], axis=0)
        v_heads = jnp.concatenate([v * keep_head[h] for h in range(RET_HEADS)], axis=0)
        s = _dot_nt(q, k_heads) * dmask_all
        o = _dot(q, state.astype(BF16)) * xi + _dot(s.astype(BF16), v_heads)
        kv = _dot_tn(k, (v.astype(F32) * zeta).astype(BF16))
        r_scr[b] = state * chunk_decay + jnp.where(same_head, kv, 0.0)
        if not reverse:
            o_ref[b] = o
        else:
            tot = o + of_ref[b]
            mu = jnp.zeros_like(tot)
            for h in range(RET_HEADS):
                mh = lane_head == h
                mu = jnp.where(mh, jnp.sum(jnp.where(mh, tot, 0.0), axis=-1, keepdims=True) / dh, mu)
            cen = tot - mu
            var = jnp.zeros_like(tot)
            for h in range(RET_HEADS):
                mh = lane_head == h
                var = jnp.where(mh, jnp.sum(jnp.where(mh, cen * cen, 0.0), axis=-1, keepdims=True) / dh, var)
            gate = g_ref[b].astype(F32)
            o_ref[b] = (gate * jax.nn.sigmoid(gate) * (cen * lax.rsqrt(var + LN_EPS))).astype(BF16)


def _retention(dec, rq, rk, rv, rg, *, B, N, L):
    S = N + L
    c = RET_CHUNK
    w = MIX_GROUP_W
    nc_lat, nc_ctx = N // c, L // c
    nc = nc_lat + nc_ctx

    def fwd_blk(j):
        return jnp.where(j < nc_ctx, nc_lat + j, j - nc_ctx)

    def bwd_blk(j):
        return nc - 1 - j

    def call(blk, reverse, extra):
        spec = pl.BlockSpec((B, c, w), lambda j: (0, blk(j), 0))
        dspec = pl.BlockSpec((1, w), lambda j: (0, 0))
        return pl.pallas_call(
            functools.partial(_ret_kernel, reverse=reverse, nb=B),
            grid=(nc,),
            in_specs=[dspec] + [spec] * (3 + len(extra)),
            out_specs=spec,
            out_shape=jax.ShapeDtypeStruct((B, S, w), BF16 if reverse else F32),
            scratch_shapes=[pltpu.VMEM((B, w, w), F32)],
            compiler_params=_cparams(sem=("arbitrary",), vmem=VMEM_LIMIT),
            name="retention_bwd" if reverse else "retention_fwd",
        )(dec[1:2] if reverse else dec[0:1], rq, rk, rv, *extra)

    o_f = call(fwd_blk, False, ())
    return call(bwd_blk, True, (o_f, rg))


def _merge_kernel(*refs, nb, split_input, n_lat_tiles, alpha):
    for b in range(nb):
        x, rest = _tile_inputs(refs, b, split_input, n_lat_tiles)
        (yf_ref, yp_ref, ya_ref, yr_ref, g1_ref, sc_ref, sh_ref, lg_ref, lb_ref, wo_ref, rw_ref, rb_ref,
         x1_ref, h2_ref) = rest[:14]
        row = _mod_row(b, n_lat_tiles, nb)
        ycat = jnp.concatenate([yf_ref[b], yp_ref[b], ya_ref[b], yr_ref[b]], axis=-1)
        y = _dot(ycat, wo_ref[...])
        x1 = _normalize(alpha * x + g1_ref[row] * y) * lg_ref[...] + lb_ref[...]
        x1_ref[b] = x1
        h2 = _normalize(x1) * (1.0 + sc_ref[row]) + sh_ref[row]
        _to_token_tiles(h2_ref.at[b], h2)
        logits = _dot_nt(rw_ref[...], h2.astype(BF16)) + rb_ref[...]
        _route(logits, rest[14 + b], rest[14 + nb + b])


def _route(logits, idx_ref, gate_ref):
    eid = lax.broadcasted_iota(I32, logits.shape, 0)
    orow = lax.broadcasted_iota(I32, (8, logits.shape[1]), 0)
    idx_out = jnp.zeros((8, logits.shape[1]), I32)
    val_out = jnp.zeros((8, logits.shape[1]), F32)
    top = None
    den = None
    for kk in range(TOP_K):
        m = jnp.max(logits, axis=0, keepdims=True)
        sel = jnp.min(jnp.where(logits == m, eid, N_EXPERTS), axis=0, keepdims=True)
        logits = jnp.where(eid == sel, -jnp.inf, logits)
        if kk == 0:
            top = m
        e = jnp.exp(m - top)
        den = e if kk == 0 else den + e
        idx_out = jnp.where(orow == kk, sel, idx_out)
        val_out = jnp.where(orow == kk, e, val_out)
    idx_ref[...] = idx_out
    gate_ref[...] = val_out / den


def _merge(xs, ys, g1, sc2, sh2, ln_g, ln_b, w_out, rw_t, rb, *, B, S, n_lat_tiles, alpha):
    split_input = len(xs) == 2
    nt = S // TILE
    g = MIX_GROUP_W
    r = ROWS_PER_TOKEN
    yspec = pl.BlockSpec((B, TILE, g), lambda i: (0, i, 0))
    vec = pl.BlockSpec((1, D_MODEL), lambda i: (0, 0))
    route_spec = pl.BlockSpec((8, TILE), lambda i: (0, i))
    outs = pl.pallas_call(
        functools.partial(_merge_kernel, nb=B, split_input=split_input, n_lat_tiles=n_lat_tiles, alpha=alpha),
        grid=(nt,),
        in_specs=_x_specs(B, split_input, n_lat_tiles) + [yspec] * 4 + [
            _mod_spec(), _mod_spec(), _mod_spec(), vec, vec,
            pl.BlockSpec((D_MODEL, D_MODEL), lambda i: (0, 0)),
            pl.BlockSpec((N_EXPERTS, D_MODEL), lambda i: (0, 0)),
            pl.BlockSpec((N_EXPERTS, 1), lambda i: (0, 0)),
        ],
        out_specs=[
            pl.BlockSpec((B, TILE, D_MODEL), lambda i: (0, i, 0)),
            pl.BlockSpec((B, TILE * r, 128), lambda i: (0, i, 0)),
        ] + [route_spec] * (2 * B),
        out_shape=[
            jax.ShapeDtypeStruct((B, S, D_MODEL), F32),
            jax.ShapeDtypeStruct((B, S * r, 128), F32),
        ] + [jax.ShapeDtypeStruct((8, S), I32)] * B + [jax.ShapeDtypeStruct((8, S), F32)] * B,
        compiler_params=_cparams(vmem=VMEM_LIMIT),
        name="out_proj_norm_router",
    )(*xs, *ys, g1, sc2, sh2, ln_g, ln_b, w_out, rw_t, rb)
    x1, h2t = outs[:2]
    idx = jnp.concatenate(outs[2:2 + B], axis=1)
    gates = jnp.concatenate(outs[2 + B:], axis=1)
    return x1, h2t.reshape(B * S * r, 128), idx, gates


def _rank_kernel(idx_ref, lpos_ref, ntab_ref, cbtab_ref, cnt_ref, carry_scr):
    tt = idx_ref.shape[1]

    @pl.when(pl.program_id(0) == 0)
    def _():
        carry_scr[...] = jnp.zeros(carry_scr.shape, F32)

    eid = lax.broadcasted_iota(I32, (N_EXPERTS, tt), 0)
    before = jnp.where(lax.broadcasted_iota(I32, (tt, tt), 0) < lax.broadcasted_iota(I32, (tt, tt), 1),
                       1.0, 0.0).astype(BF16)
    hits, cums, pres = [], [], []
    n_tile = jnp.zeros((N_EXPERTS, 1), F32)
    for kk in range(TOP_K):
        hit = eid == idx_ref[kk:kk + 1, :]
        hits.append(hit)
        cums.append(_dot(jnp.where(hit, 1.0, 0.0).astype(BF16), before))
        pres.append(n_tile)
        n_tile = n_tile + jnp.sum(jnp.where(hit, 1.0, 0.0), axis=1, keepdims=True)
    n16 = ((n_tile.astype(I32) + (RUN_ALIGN - 1)) >> RUN_SHIFT) << RUN_SHIFT
    lower = jnp.where(lax.broadcasted_iota(I32, (N_EXPERTS, N_EXPERTS), 1)
                      < lax.broadcasted_iota(I32, (N_EXPERTS, N_EXPERTS), 0), 1.0, 0.0).astype(BF16)
    hi = jnp.broadcast_to((n16 >> 8).astype(F32), (N_EXPERTS, 128)).astype(BF16)
    lo = jnp.broadcast_to((n16 & 255).astype(F32), (N_EXPERTS, 128)).astype(BF16)
    lstart = (256.0 * _dot(lower, hi) + _dot(lower, lo))[:, 0:1]
    orow = lax.broadcasted_iota(I32, (8, tt), 0)
    out = jnp.zeros((8, tt), I32)
    for kk in range(TOP_K):
        slot = jnp.sum(jnp.where(hits[kk], lstart + pres[kk] + cums[kk], 0.0), axis=0, keepdims=True)
        out = jnp.where(orow == kk, slot.astype(I32) * ROWS_PER_TOKEN, out)
    lpos_ref[...] = out
    carry = carry_scr[...]
    ntab_ref[...] = jnp.broadcast_to(n_tile, ntab_ref.shape).astype(I32)
    cbtab_ref[...] = carry.astype(I32)
    carry = carry + n_tile
    carry_scr[...] = carry
    cnt_ref[...] = carry


def _rank(idx, *, T):
    tt = MOE_TOK_TILE
    nt = T // tt
    tab = pl.BlockSpec((None, N_EXPERTS, 128), lambda i: (i, 0, 0))
    return pl.pallas_call(
        _rank_kernel,
        grid=(nt,),
        in_specs=[pl.BlockSpec((8, tt), lambda i: (0, i))],
        out_specs=[pl.BlockSpec((8, tt), lambda i: (0, i)), tab, tab,
                   pl.BlockSpec((N_EXPERTS, 128), lambda i: (0, 0))],
        out_shape=[jax.ShapeDtypeStruct((8, T), I32), jax.ShapeDtypeStruct((nt, N_EXPERTS, 128), I32),
                   jax.ShapeDtypeStruct((nt, N_EXPERTS, 128), I32), jax.ShapeDtypeStruct((N_EXPERTS, 128), F32)],
        scratch_shapes=[pltpu.VMEM((N_EXPERTS, 128), F32)],
        compiler_params=_cparams(sem=("arbitrary",)),
        name="moe_rank",
    )(idx)


def _run_chunk(src_ref, src_tok, dst_ref, dst_tok, sem):
    return pltpu.make_async_copy(src_ref.at[_token_rows(src_tok, RUN_ALIGN), :],
                                 dst_ref.at[_token_rows(dst_tok, RUN_ALIGN), :], sem)


def _wait_chunks(src_ref, dst_ref, sem, count):
    def body(_, carry):
        _run_chunk(src_ref, 0, dst_ref, 0, sem).wait()
        return carry
    lax.fori_loop(0, count, body, 0)


def _copy_runs(step, pstart_ref, ntab_ref, cbtab_ref, chunk):
    def per_expert(e, carry):
        src0, total = carry
        nch = (ntab_ref[step * N_EXPERTS + e] + (RUN_ALIGN - 1)) >> RUN_SHIFT
        dst0 = pstart_ref[e] + cbtab_ref[step * N_EXPERTS + e]

        def one(c, cc):
            chunk(src0 + c * RUN_ALIGN, dst0 + c * RUN_ALIGN).start()
            return cc
        lax.fori_loop(0, nch, one, 0)
        return src0 + nch * RUN_ALIGN, total + nch
    return lax.fori_loop(0, N_EXPERTS, per_expert, (0, 0))[1]


def _dispatch_kernel(pstart_ref, cnt_ref, pad_ref, ntab_ref, cbtab_ref, lpos_ref, h_ref, xs_ref,
                     stage, zero_scr, sems, sent):
    step = pl.program_id(0)

    @pl.when(step == 0)
    def _():
        sem = sems.at[0]
        zero_scr[...] = jnp.zeros(zero_scr.shape, F32)

        def per_expert(e, total):
            first = pstart_ref[e] + ((cnt_ref[e] >> RUN_SHIFT) << RUN_SHIFT)
            n_fill = (pstart_ref[e] + pad_ref[e] - first) >> RUN_SHIFT

            def fill(c, carry):
                _run_chunk(zero_scr, 0, xs_ref, first + c * RUN_ALIGN, sem).start()
                return carry
            lax.fori_loop(0, n_fill, fill, 0)
            return total + n_fill
        n_total = lax.fori_loop(0, N_EXPERTS, per_expert, 0)
        used = pstart_ref[N_EXPERTS - 1] + pad_ref[N_EXPERTS - 1]
        n_tail = (xs_ref.shape[0] // ROWS_PER_TOKEN - used) >> RUN_SHIFT

        def fill_tail(c, carry):
            _run_chunk(zero_scr, 0, xs_ref, used + c * RUN_ALIGN, sem).start()
            return carry
        lax.fori_loop(0, n_tail, fill_tail, 0)
        _wait_chunks(zero_scr, xs_ref, sem, n_total + n_tail)
        sent[0] = 0
        sent[1] = 0

    slot = step % 2
    buf = stage.at[slot]
    sem = sems.at[slot]

    def zero_last_group(e, src0):
        n_up = ((ntab_ref[step * N_EXPERTS + e] + (RUN_ALIGN - 1)) >> RUN_SHIFT) << RUN_SHIFT
        buf[_token_rows(jnp.maximum(src0 + n_up - RUN_ALIGN, 0), RUN_ALIGN), :] = jnp.zeros((CHUNK_ROWS, 128), F32)
        return src0 + n_up
    lax.fori_loop(0, N_EXPERTS, zero_last_group, 0)
    tt = MOE_TOK_TILE
    for kk in range(TOP_K):
        def place(t, carry, kk=kk):
            row = pl.multiple_of(lpos_ref[kk * tt + t], ROWS_PER_TOKEN)
            buf[pl.ds(row, ROWS_PER_TOKEN), :] = h_ref[_token_rows(t), :]
            return carry
        lax.fori_loop(0, tt, place, 0, unroll=16)
    _wait_chunks(stage.at[1 - slot], xs_ref, sems.at[1 - slot], sent[1 - slot])
    sent[1 - slot] = 0
    sent[slot] = _copy_runs(step, pstart_ref, ntab_ref, cbtab_ref,
                            lambda s, d: _run_chunk(buf, s, xs_ref, d, sem))

    @pl.when(step == pl.num_programs(0) - 1)
    def _():
        _wait_chunks(buf, xs_ref, sem, sent[slot])


def _smem_tile_spec(tt):
    return pl.BlockSpec((8 * tt,), lambda i, *_: (i,), memory_space=pltpu.SMEM)


def _tile_major(table):
    tt = MOE_TOK_TILE
    return table.reshape(table.shape[0], -1, tt).transpose(1, 0, 2).reshape(-1)


def _dispatch(pstart, cnt, padded, ntab, cbtab, lpos, h2t, *, T, P):
    tt = MOE_TOK_TILE
    r = ROWS_PER_TOKEN
    return pl.pallas_call(
        _dispatch_kernel,
        grid_spec=pltpu.PrefetchScalarGridSpec(
            num_scalar_prefetch=5,
            grid=(T // tt,),
            in_specs=[_smem_tile_spec(tt), pl.BlockSpec((tt * r, 128), lambda i, *_: (i, 0))],
            out_specs=pl.BlockSpec(memory_space=pl.ANY),
            scratch_shapes=[pltpu.VMEM((2, STAGE_TOKENS * r, 128), F32), pltpu.VMEM((CHUNK_ROWS, 128), F32),
                            pltpu.SemaphoreType.DMA((2,)), pltpu.SMEM((2,), I32)],
        ),
        out_shape=jax.ShapeDtypeStruct((P * r, 128), F32),
        compiler_params=_cparams(sem=("arbitrary",), vmem=VMEM_LIMIT),
        name="moe_dispatch",
    )(pstart, cnt, padded, ntab, cbtab, _tile_major(lpos), h2t)


def _collect_kernel(pstart_ref, ntab_ref, cbtab_ref, lpos_ref, gate_ref, ys_ref, f_ref, stage, sems, sent):
    step = pl.program_id(0)
    slot = step % 2

    def fetch(tile, b):
        sent[b] = _copy_runs(tile, pstart_ref, ntab_ref, cbtab_ref,
                             lambda s, d: _run_chunk(ys_ref, d, stage.at[b], s, sems.at[b]))

    @pl.when(step == 0)
    def _():
        fetch(0, 0)

    @pl.when(step + 1 < pl.num_programs(0))
    def _():
        fetch(step + 1, 1 - slot)

    buf = stage.at[slot]
    _wait_chunks(ys_ref, buf, sems.at[slot], sent[slot])

    tt = MOE_TOK_TILE

    def combine(t, carry):
        acc = None
        for kk in range(TOP_K):
            row = pl.multiple_of(lpos_ref[kk * tt + t], ROWS_PER_TOKEN)
            term = gate_ref[kk * tt + t] * buf[pl.ds(row, ROWS_PER_TOKEN), :]
            acc = term if acc is None else acc + term
        f_ref[_token_rows(t), :] = acc
        return carry
    lax.fori_loop(0, tt, combine, 0, unroll=4)


def _collect(pstart, ntab, cbtab, lpos, gates, ys, *, T):
    tt = MOE_TOK_TILE
    r = ROWS_PER_TOKEN
    return pl.pallas_call(
        _collect_kernel,
        grid_spec=pltpu.PrefetchScalarGridSpec(
            num_scalar_prefetch=3,
            grid=(T // tt,),
            in_specs=[_smem_tile_spec(tt), _smem_tile_spec(tt), pl.BlockSpec(memory_space=pl.ANY)],
            out_specs=pl.BlockSpec((tt * r, 128), lambda i, *_: (i, 0)),
            scratch_shapes=[pltpu.VMEM((2, STAGE_TOKENS * r, 128), F32), pltpu.SemaphoreType.DMA((2,)),
                            pltpu.SMEM((2,), I32)],
        ),
        out_shape=jax.ShapeDtypeStruct((T * r, 128), F32),
        compiler_params=_cparams(sem=("arbitrary",), vmem=VMEM_LIMIT),
        name="moe_collect",
    )(pstart, ntab, cbtab, _tile_major(lpos), _tile_major(gates), ys)


def _deinterleave_matrix():
    m = np.zeros((256, 256), np.float32)
    j = np.arange(128)
    m[2 * j, j] = 1.0
    m[2 * j + 1, 128 + j] = 1.0
    return jnp.asarray(m, dtype=BF16)


def _expert_kernel(be_ref, nu_ref, x_ref, w1_ref, b1_ref, w2_ref, b2_ref, perm_ref, y_ref, w1_scr, w2_scr):
    i = pl.program_id(0)

    @pl.when(i >= nu_ref[0])
    def _():
        y_ref[...] = jnp.zeros(y_ref.shape, F32)

    @pl.when(i < nu_ref[0])
    def _():
        @pl.when((i == 0) | (be_ref[i] != be_ref[jnp.maximum(i - 1, 0)]))
        def _():
            half = w1_scr.shape[1] // 2
            for m in range(w1_ref.shape[1] // 256):
                r = _dot(w1_ref[:, 256 * m:256 * (m + 1)].astype(BF16), perm_ref[...])
                w1_scr[:, 128 * m:128 * (m + 1)] = r[:, :128].astype(BF16)
                w1_scr[:, half + 128 * m:half + 128 * (m + 1)] = r[:, 128:].astype(BF16)
            w2_scr[...] = w2_ref[...].astype(BF16)

        x = _from_token_tiles(x_ref, MOE_BLK).astype(BF16)
        half = w1_scr.shape[1] // 2
        y = None
        for c0 in range(0, half, EXPERT_CHUNK):
            c1 = c0 + EXPERT_CHUNK
            glu = _dot(x, w1_scr[:, c0:c1]) + b1_ref[:, c0:c1]
            lin = _dot(x, w1_scr[:, half + c0:half + c1]) + b1_ref[:, half + c0:half + c1]
            glu = jnp.minimum(glu, SWIGLU_LIMIT)
            lin = jnp.clip(lin, -SWIGLU_LIMIT, SWIGLU_LIMIT)
            act = glu * jax.nn.sigmoid(SWIGLU_ALPHA * glu) * (lin + 1.0)
            part = _dot(act.astype(BF16), w2_scr[c0:c1, :])
            y = part + b2_ref[...] if y is None else y + part
        _to_token_tiles(y_ref, y)


def _experts(blk_e, n_used, xs, w1, b1, w2, b2, *, P, layer):
    nblk = P // MOE_BLK
    r = ROWS_PER_TOKEN
    d, two_f = w1.shape[2:]
    f = w2.shape[2]
    rows = lambda i, be, nu: (jnp.minimum(i, nu[0] - 1), 0)
    return pl.pallas_call(
        _expert_kernel,
        grid_spec=pltpu.PrefetchScalarGridSpec(
            num_scalar_prefetch=2,
            grid=(nblk,),
            in_specs=[
                pl.BlockSpec((MOE_BLK * r, 128), rows),
                pl.BlockSpec((None, None, d, two_f), lambda i, be, nu: (layer, be[i], 0, 0)),
                pl.BlockSpec((None, 1, two_f), lambda i, be, nu: (be[i], 0, 0)),
                pl.BlockSpec((None, None, f, d), lambda i, be, nu: (layer, be[i], 0, 0)),
                pl.BlockSpec((None, 1, d), lambda i, be, nu: (be[i], 0, 0)),
                pl.BlockSpec((256, 256), lambda i, be, nu: (0, 0)),
            ],
            out_specs=pl.BlockSpec((MOE_BLK * r, 128), lambda i, be, nu: (i, 0)),
            scratch_shapes=[pltpu.VMEM((d, two_f), BF16), pltpu.VMEM((f, d), BF16)],
        ),
        out_shape=jax.ShapeDtypeStruct((P * r, 128), F32),
        compiler_params=_cparams(sem=("arbitrary",), vmem=EXPERT_VMEM_LIMIT),
        name="moe_experts",
    )(blk_e, n_used, xs, w1, b1, w2, b2, _deinterleave_matrix())


def _moe(h2t, idx, gates, w1, b1, w2, b2, *, T, layer):
    lpos, ntab, cbtab, cnt = _rank(idx, T=T)
    counts = cnt[:, 0].astype(I32)
    padded = (counts + RUN_ALIGN + MOE_BLK - 1) // MOE_BLK * MOE_BLK
    pad_end = jnp.cumsum(padded)
    pstart = (pad_end - padded).astype(I32)
    nblk = TOP_K * T // MOE_BLK + 2 * N_EXPERTS
    P = nblk * MOE_BLK
    starts = jnp.arange(nblk, dtype=I32) * MOE_BLK
    blk_e = jnp.minimum(jnp.sum((pad_end[None, :] <= starts[:, None]).astype(I32), axis=1), N_EXPERTS - 1)
    n_used = (pad_end[-1:] // MOE_BLK).astype(I32)
    ntab = ntab[:, :, 0].reshape(-1)
    cbtab = cbtab[:, :, 0].reshape(-1)
    xs = _dispatch(pstart, counts, padded.astype(I32), ntab, cbtab, lpos, h2t, T=T, P=P)
    ys = _experts(blk_e.astype(I32), n_used, xs, w1, b1, w2, b2, P=P, layer=layer)
    return _collect(pstart, ntab, cbtab, lpos, gates, ys, T=T)


def _residual_kernel(f_ref, x_ref, g2_ref, lg_ref, lb_ref, o_ref, *, nb, n_lat_tiles, alpha):
    for b in range(nb):
        f = _from_token_tiles(f_ref.at[b], TILE)
        z = alpha * x_ref[b] + g2_ref[_mod_row(b, n_lat_tiles, nb)] * f
        o_ref[b] = _normalize(z) * lg_ref[...] + lb_ref[...]


def _residual(ft, x1, g2, ln_g, ln_b, *, B, S, n_out_tiles, n_lat_tiles, alpha):
    r = ROWS_PER_TOKEN
    vec = pl.BlockSpec((1, D_MODEL), lambda i: (0, 0))
    return pl.pallas_call(
        functools.partial(_residual_kernel, nb=B, n_lat_tiles=n_lat_tiles, alpha=alpha),
        grid=(n_out_tiles,),
        in_specs=[
            pl.BlockSpec((B, TILE * r, 128), lambda i: (0, i, 0)),
            pl.BlockSpec((B, TILE, D_MODEL), lambda i: (0, i, 0)),
            _mod_spec(), vec, vec,
        ],
        out_specs=pl.BlockSpec((B, TILE, D_MODEL), lambda i: (0, i, 0)),
        out_shape=jax.ShapeDtypeStruct((B, n_out_tiles * TILE, D_MODEL), F32),
        compiler_params=_cparams(vmem=VMEM_LIMIT),
        name="ffn_residual_norm",
    )(ft.reshape(B, S * r, 128), x1, g2, ln_g, ln_b)


def _rope_tables(n_lat, n_ctx):
    half = HEAD_DIM // 2
    pos = np.arange(n_lat)
    inv = 1.0 / (ROPE_BASE ** (np.arange(0, half, 2, dtype=np.float64) / half))
    ang_row = (pos // GRID_W)[:, None].astype(np.float64) * inv[None, :]
    ang_col = (pos % GRID_W)[:, None].astype(np.float64) * inv[None, :]
    ang = np.concatenate([ang_row, ang_row, ang_col, ang_col], axis=1)
    sign = np.concatenate([-np.ones(16), np.ones(16)] * 2)[None, :]
    cos = np.concatenate([np.cos(ang), np.ones((n_ctx, HEAD_DIM))], axis=0)
    sin = np.concatenate([np.sin(ang) * sign, np.zeros((n_ctx, HEAD_DIM))], axis=0)
    return (jnp.asarray(np.tile(cos, (1, 2)), dtype=F32), jnp.asarray(np.tile(sin, (1, 2)), dtype=F32))


def _extend_w_in(w):
    g = MIX_GROUP_W
    hd = HEAD_DIM
    a, bp, q = w[:, 0:g], w[:, g:2 * g], w[:, 2 * g:3 * g]
    k, v = w[:, 3 * g:3 * g + 2 * hd], w[:, 3 * g + 2 * hd:4 * g]
    rest = w[:, 4 * g:]
    dup = lambda t: jnp.concatenate([t[:, :hd], t[:, :hd], t[:, hd:], t[:, hd:]], axis=1)
    return jnp.concatenate([a, bp, q, dup(k), dup(v), rest], axis=1).astype(BF16)


def _block_diag(pw):
    n, c, _ = pw.shape
    out = jnp.zeros((n * c, n * c), pw.dtype)
    for gi in range(n):
        out = lax.dynamic_update_slice(out, pw[gi], (gi * c, gi * c))
    return out


def kernel(x, c, ctx, c_ctx, w_mod, b_mod, w_in, pool_w, pool_scale, attn_sink, ret_decay, w_out, ln_g, ln_b,
           router_w, router_b, exp_w1, exp_b1, exp_w2, exp_b2):
    B, N, D = x.shape
    L = ctx.shape[1]
    depth = w_mod.shape[0]
    assert D == D_MODEL and L == TILE and N % TILE == 0 and N % (FFT_RADIX2 * 16) == 0 and B < 8
    S = N + L
    n_lat_tiles = N // TILE
    T = B * S
    assert T % MOE_TOK_TILE == 0
    alpha = float((2 * depth) ** 0.25)

    cvec = jnp.zeros((8, D), F32).at[:B].set(c).at[B].set(c_ctx)
    mods = _modulation(cvec, w_mod, b_mod)
    cos, sin = _rope_tables(N, L)

    xs = (x, ctx)
    out = None
    for l in range(depth):
        last = l == depth - 1
        m = mods[l].reshape(8, 6, 1, D)
        sh1, sc1, g1, sh2, sc2, g2 = [m[:, j] for j in range(6)]
        proj = _projection(xs, sc1, sh1, _extend_w_in(w_in[l]), cos, sin, B=B, S=S, n_lat_tiles=n_lat_tiles)
        a, bp, q, k2, v2, rq, rk, rv, rg = proj
        y_f = _fourier_mix(a, B=B, N=N, L=L, with_ctx=not last)
        y_p = _pool_mix(bp, _block_diag(pool_w[l]).astype(BF16), pool_scale[l].reshape(1, -1), B=B, N=N, L=L)
        y_a = _attention(attn_sink[l], q, k2, v2, B=B, N=N, L=L)
        dec = jnp.repeat(ret_decay[l], MIX_GROUP_W // RET_HEADS, axis=1)
        y_r = _retention(dec, rq, rk, rv, rg, B=B, N=N, L=L)
        x1, h2t, idx, gates = _merge(
            xs, (y_f, y_p, y_a, y_r), g1, sc2, sh2, ln_g[l, 0:1], ln_b[l, 0:1], w_out[l].astype(BF16),
            router_w[l].T.astype(BF16), router_b[l].reshape(-1, 1), B=B, S=S, n_lat_tiles=n_lat_tiles, alpha=alpha)
        b1 = jnp.concatenate([exp_b1[l][:, 0::2], exp_b1[l][:, 1::2]], axis=-1)[:, None, :]
        ft = _moe(h2t, idx, gates, exp_w1, b1, exp_w2, exp_b2[l][:, None, :], T=T, layer=l)
        out = _residual(ft, x1, g2, ln_g[l, 1:2], ln_b[l, 1:2], B=B, S=S,
                        n_out_tiles=n_lat_tiles if last else S // TILE, n_lat_tiles=n_lat_tiles, alpha=alpha)
        xs = (out,)
    return out
```

```python
import functools

import numpy as np
import jax
import jax.numpy as jnp
from jax import lax
from jax.experimental import pallas as pl
from jax.experimental.pallas import tpu as pltpu

F32 = jnp.float32
BF16 = jnp.bfloat16
I32 = jnp.int32

D_MODEL = 1024
GRID_W = 64
FOURIER_GROUP_W = 64
POOL_WINDOWS = (2, 4, 8, 16)
HEAD_DIM = 64
WINDOW = 128
ROPE_BASE = 10000.0
RET_HEADS = 4
RET_CHUNK = 128
N_EXPERTS = 32
TOP_K = 4
SWIGLU_LIMIT = 7.0
SWIGLU_ALPHA = 1.702
LN_EPS = 1e-6
NEG_INF = -1e30
MIX_GROUP_W = 256
FFT_RADIX2 = 64

TILE = 256
MOE_BLK = 512
MOE_TOK_TILE = 1024
ROWS_PER_TOKEN = D_MODEL // 128
RUN_ALIGN = 16
RUN_SHIFT = 4
CHUNK_ROWS = RUN_ALIGN * ROWS_PER_TOKEN
STAGE_TOKENS = TOP_K * MOE_TOK_TILE + N_EXPERTS * RUN_ALIGN
VMEM_LIMIT = 48 * 1024 * 1024
EXPERT_VMEM_LIMIT = 56 * 1024 * 1024


def _dot(a, b):
    return jnp.dot(a, b, preferred_element_type=F32)


def _dot_nt(a, b):
    return lax.dot_general(a, b, (((1,), (1,)), ((), ())), preferred_element_type=F32)


def _dot_tn(a, b):
    return lax.dot_general(a, b, (((0,), (0,)), ((), ())), preferred_element_type=F32)


def _normalize(x):
    mu = jnp.mean(x, axis=-1, keepdims=True)
    xc = x - mu
    var = jnp.mean(xc * xc, axis=-1, keepdims=True)
    return xc * lax.rsqrt(var + LN_EPS)


def _cparams(sem=None, vmem=None):
    kw = {}
    if sem is not None:
        kw["dimension_semantics"] = sem
    if vmem is not None:
        kw["vmem_limit_bytes"] = vmem
    return pltpu.CompilerParams(**kw)


def _to_token_tiles(ref, val):
    n = val.shape[0]
    for c in range(ROWS_PER_TOKEN):
        ref[pl.ds(c, n, stride=ROWS_PER_TOKEN), :] = val[:, 128 * c:128 * (c + 1)]


def _from_token_tiles(ref, n):
    return jnp.concatenate([ref[pl.ds(c, n, stride=ROWS_PER_TOKEN), :] for c in range(ROWS_PER_TOKEN)], axis=-1)


def _token_rows(tok, count=1):
    return pl.ds(pl.multiple_of(tok * ROWS_PER_TOKEN, ROWS_PER_TOKEN), count * ROWS_PER_TOKEN)


def _mod_kernel(c_ref, w_ref, b_ref, o_ref):
    c = c_ref[...]
    a = c * jax.nn.sigmoid(c)
    w = w_ref[...]
    a_hi = a.astype(BF16)
    a_lo = (a - a_hi.astype(F32)).astype(BF16)
    w_hi = w.astype(BF16)
    w_lo = (w - w_hi.astype(F32)).astype(BF16)
    o_ref[...] = _dot(a_hi, w_hi) + _dot(a_lo, w_hi) + _dot(a_hi, w_lo) + b_ref[...]


def _modulation(cvec, w_mod, b_mod):
    depth, d, six_d = w_mod.shape
    nb = 1536
    return pl.pallas_call(
        _mod_kernel,
        grid=(depth, six_d // nb),
        in_specs=[
            pl.BlockSpec((8, d), lambda l, j: (0, 0)),
            pl.BlockSpec((None, d, nb), lambda l, j: (l, 0, j)),
            pl.BlockSpec((None, 1, nb), lambda l, j: (l, 0, j)),
        ],
        out_specs=pl.BlockSpec((None, 8, nb), lambda l, j: (l, 0, j)),
        out_shape=jax.ShapeDtypeStruct((depth, 8, six_d), F32),
        compiler_params=_cparams(vmem=VMEM_LIMIT),
        name="modulation",
    )(cvec, w_mod, b_mod.reshape(depth, 1, six_d))


def _tile_inputs(refs, b, split_input, n_lat_tiles):
    if split_input:
        x = jnp.where(pl.program_id(0) == n_lat_tiles, refs[1][b], refs[0][b])
        return x, refs[2:]
    return refs[0][b], refs[1:]


def _mod_row(b, n_lat_tiles, ctx_row):
    return jnp.where(pl.program_id(0) == n_lat_tiles, ctx_row, b)


def _proj_kernel(*refs, nb, split_input, n_lat_tiles):
    g = MIX_GROUP_W
    for b in range(nb):
        x, rest = _tile_inputs(refs, b, split_input, n_lat_tiles)
        sc_ref, sh_ref, w_ref, cos_ref, sin_ref = rest[:5]
        a_ref, bp_ref, q_ref, k_ref, v_ref, rq_ref, rk_ref, rv_ref, rg_ref = rest[5:]
        row = _mod_row(b, n_lat_tiles, nb)
        h = _normalize(x) * (1.0 + sc_ref[row]) + sh_ref[row]
        p = _dot(h.astype(BF16), w_ref[...])
        a_ref[b] = p[:, 0:g].astype(BF16)
        bp_ref[b] = p[:, g:2 * g].astype(BF16)
        qk = p[:, 2 * g:4 * g]
        lane = lax.broadcasted_iota(I32, qk.shape, 1)
        first = (lane & 16) == 0
        partner = jnp.where(first, pltpu.roll(qk, 2 * g - 16, 1), pltpu.roll(qk, 16, 1))
        cos = jnp.concatenate([cos_ref[...]] * 4, axis=-1)
        sin = jnp.concatenate([sin_ref[...]] * 4, axis=-1)
        qk = qk * cos + partner * sin
        q_ref[b] = (qk[:, :g] * (HEAD_DIM ** -0.5)).astype(BF16)
        k_ref[b] = qk[:, g:].astype(BF16)
        v_ref[b] = p[:, 4 * g:5 * g].astype(BF16)
        rq_ref[b] = p[:, 5 * g:6 * g].astype(BF16)
        rk_ref[b] = p[:, 6 * g:7 * g].astype(BF16)
        rv_ref[b] = p[:, 7 * g:8 * g].astype(BF16)
        rg_ref[b] = p[:, 8 * g:9 * g].astype(BF16)


def _mod_spec():
    return pl.BlockSpec((8, 1, D_MODEL), lambda i: (0, 0, 0))


def _x_specs(B, split_input, n_lat_tiles):
    if split_input:
        return [
            pl.BlockSpec((B, TILE, D_MODEL), lambda i: (0, jnp.minimum(i, n_lat_tiles - 1), 0)),
            pl.BlockSpec((B, TILE, D_MODEL), lambda i: (0, 0, 0)),
        ]
    return [pl.BlockSpec((B, TILE, D_MODEL), lambda i: (0, i, 0))]


def _projection(xs, sc, sh, w_ext, cos, sin, *, B, S, n_lat_tiles):
    split_input = len(xs) == 2
    nt = S // TILE
    g = MIX_GROUP_W
    out_spec = pl.BlockSpec((B, TILE, g), lambda i: (0, i, 0))
    out_shape = jax.ShapeDtypeStruct((B, S, g), BF16)
    return pl.pallas_call(
        functools.partial(_proj_kernel, nb=B, split_input=split_input, n_lat_tiles=n_lat_tiles),
        grid=(nt,),
        in_specs=_x_specs(B, split_input, n_lat_tiles) + [
            _mod_spec(),
            _mod_spec(),
            pl.BlockSpec(w_ext.shape, lambda i: (0, 0)),
            pl.BlockSpec((TILE, 128), lambda i: (i, 0)),
            pl.BlockSpec((TILE, 128), lambda i: (i, 0)),
        ],
        out_specs=[out_spec] * 9,
        out_shape=[out_shape] * 9,
        compiler_params=_cparams(vmem=VMEM_LIMIT),
        name="ln_mod_in_proj",
    )(*xs, sc, sh, w_ext, cos, sin)


def _channel_dft_matrix():
    c = np.arange(FOURIER_GROUP_W)
    ang = 2.0 * np.pi * np.outer(c, c) / FOURIER_GROUP_W
    eye = np.eye(MIX_GROUP_W // FOURIER_GROUP_W)
    cs = np.concatenate([np.kron(eye, np.cos(ang)), -np.kron(eye, np.sin(ang))], axis=1)
    return jnp.asarray(cs, dtype=BF16)


def _seq_dft_stage_a(n1):
    n2 = FFT_RADIX2
    n = n1 * n2
    k1 = np.arange(n1)[None, :, None]
    i1 = np.arange(n1)[None, None, :]
    i2 = np.arange(n2)[:, None, None]
    th = 2.0 * np.pi * ((k1 * (n2 * i1 + i2)) % n) / n
    wr, wi = np.cos(th), -np.sin(th)
    wa = np.concatenate([np.concatenate([wr, -wi], axis=2), np.concatenate([wi, wr], axis=2)], axis=1)
    return jnp.asarray(wa, dtype=BF16)


def _seq_dft_stage_b():
    n2 = FFT_RADIX2
    k2 = np.arange(n2)
    ph = 2.0 * np.pi * (np.outer(k2, k2) % n2) / n2
    return jnp.asarray(np.concatenate([np.cos(ph), np.sin(ph)], axis=1), dtype=BF16)


def _dense_dft(length):
    k = np.arange(length)
    th = 2.0 * np.pi * (np.outer(k, k) % length) / length
    return jnp.asarray(np.concatenate([np.cos(th), np.sin(th)], axis=1), dtype=BF16)


FFT_A_COLS = 4
FFT_B_COLS = 8


def _fft_a_kernel(a_ref, cs_ref, wa_ref, t_ref, *, nb, n1):
    g = MIX_GROUP_W
    for j in range(FFT_A_COLS):
        for b in range(nb):
            p = _dot(a_ref[b, :, g * j:g * (j + 1)], cs_ref[...])
            pp = jnp.concatenate([p[:, :g], p[:, g:]], axis=0).astype(BF16)
            t = _dot(wa_ref[j], pp)
            t_ref[b, :, 2 * g * j:2 * g * (j + 1)] = jnp.concatenate([t[:n1], t[n1:]], axis=-1).astype(BF16)


def _fft_b_kernel(t_ref, vb_ref, y_ref, *, nb, scale):
    g = MIX_GROUP_W
    n2 = FFT_RADIX2
    for j in range(FFT_B_COLS):
        for b in range(nb):
            t = t_ref[b, n2 * j:n2 * (j + 1), :]
            tt = jnp.concatenate([t[:, :g], t[:, g:]], axis=0)
            y = _dot(vb_ref[...], tt) * scale
            y_ref[b, 0:n2, g * j:g * (j + 1)] = y.astype(BF16)
            y_ref[b, n2:, g * j:g * (j + 1)] = jnp.zeros((y_ref.shape[1] - n2, g), BF16)


def _fft_ctx_kernel(a_ref, cs_ref, wc_ref, yin_ref, y_ref, *, scale):
    del yin_ref
    g = MIX_GROUP_W
    p = _dot(a_ref[...], cs_ref[...])
    pp = jnp.concatenate([p[:, :g], p[:, g:]], axis=0).astype(BF16)
    y_ref[...] = (_dot(wc_ref[...], pp) * scale).astype(BF16)


def _fourier_mix(a, *, B, N, L, with_ctx):
    S = N + L
    g = MIX_GROUP_W
    n2 = FFT_RADIX2
    n1 = N // n2
    cs = _channel_dft_matrix()
    a2 = a.reshape(B, S // n2, n2 * g)
    t = pl.pallas_call(
        functools.partial(_fft_a_kernel, nb=B, n1=n1),
        grid=(n2 // FFT_A_COLS,),
        in_specs=[
            pl.BlockSpec((B, n1, FFT_A_COLS * g), lambda j: (0, 0, j)),
            pl.BlockSpec(cs.shape, lambda j: (0, 0)),
            pl.BlockSpec((FFT_A_COLS, 2 * n1, 2 * n1), lambda j: (j, 0, 0)),
        ],
        out_specs=pl.BlockSpec((B, n1, FFT_A_COLS * 2 * g), lambda j: (0, 0, j)),
        out_shape=jax.ShapeDtypeStruct((B, n1, n2 * 2 * g), BF16),
        name="fourier_seq_stage_a",
    )(a2, cs, _seq_dft_stage_a(n1))
    t = t.reshape(B, n1 * n2, 2 * g)
    rows = S // n1
    y = pl.pallas_call(
        functools.partial(_fft_b_kernel, nb=B, scale=float((N * FOURIER_GROUP_W) ** -0.5)),
        grid=(n1 // FFT_B_COLS,),
        in_specs=[
            pl.BlockSpec((B, FFT_B_COLS * n2, 2 * g), lambda k: (0, k, 0)),
            pl.BlockSpec((n2, 2 * n2), lambda k: (0, 0)),
        ],
        out_specs=pl.BlockSpec((B, rows, FFT_B_COLS * g), lambda k: (0, 0, k)),
        out_shape=jax.ShapeDtypeStruct((B, rows, n1 * g), BF16),
        name="fourier_seq_stage_b",
    )(t, _seq_dft_stage_b())
    y = y.reshape(B, S, g)
    if not with_ctx:
        return y
    ctx_tile = N // TILE
    return pl.pallas_call(
        functools.partial(_fft_ctx_kernel, scale=float((L * FOURIER_GROUP_W) ** -0.5)),
        grid=(B,),
        in_specs=[
            pl.BlockSpec((None, TILE, g), lambda b: (b, ctx_tile, 0)),
            pl.BlockSpec(cs.shape, lambda b: (0, 0)),
            pl.BlockSpec((L, 2 * L), lambda b: (0, 0)),
            pl.BlockSpec(memory_space=pl.ANY),
        ],
        out_specs=pl.BlockSpec((None, TILE, g), lambda b: (b, ctx_tile, 0)),
        out_shape=jax.ShapeDtypeStruct((B, S, g), BF16),
        input_output_aliases={3: 0},
        name="fourier_ctx",
    )(a, cs, _dense_dft(L), y)


POOL_HALO = 16


def _pool_bands():
    t = np.arange(TILE)[:, None]
    s = np.arange(TILE + 2 * POOL_HALO)[None, :] - POOL_HALO
    bands = [((s - t >= -(w // 2)) & (s - t < w // 2)).astype(np.float32) for w in POOL_WINDOWS]
    return jnp.asarray(np.stack(bands), dtype=BF16)


def _pool_kernel(prev_ref, x_ref, next_ref, band_ref, pw_ref, ps_ref, o_ref, *, nb, n_lat, s_tot):
    i = pl.program_id(0)
    t0 = i * TILE
    is_ctx = t0 >= n_lat
    r0 = jnp.where(is_ctx, n_lat, 0)
    r1 = jnp.where(is_ctx, s_tot, n_lat)
    kw = TILE + 2 * POOL_HALO
    s_pos = t0 - POOL_HALO + lax.broadcasted_iota(I32, (kw, 1), 0)
    inside = jnp.where(s_pos >= r0, jnp.where(s_pos < r1, 1.0, 0.0), 0.0).astype(BF16)
    xc = jnp.concatenate([jnp.concatenate([prev_ref[b], x_ref[b], next_ref[b]], axis=0) * inside
                          for b in range(nb)], axis=-1)
    g = MIX_GROUP_W
    t_col = t0 + lax.broadcasted_iota(I32, (TILE, 1), 0)
    group = (lax.broadcasted_iota(I32, (TILE, nb * g), 1) // FOURIER_GROUP_W) % len(POOL_WINDOWS)
    acc = jnp.zeros((TILE, nb * g), F32)
    for gi, w in enumerate(POOL_WINDOWS):
        lo = jnp.maximum(t_col - w // 2, r0)
        hi = jnp.minimum(t_col + w // 2, r1)
        mean = _dot(band_ref[gi], xc) / (hi - lo).astype(F32)
        acc = jnp.where(group == gi, mean, acc)
    for b in range(nb):
        pooled = acc[:, g * b:g * (b + 1)] - x_ref[b].astype(F32)
        o_ref[b] = (_dot(pooled.astype(BF16), pw_ref[...]) * ps_ref[...]).astype(BF16)


def _pool_mix(bp, pw_bd, pscale, *, B, N, L):
    S = N + L
    nt = S // TILE
    g = MIX_GROUP_W
    hb = TILE // POOL_HALO
    last = S // POOL_HALO - 1
    return pl.pallas_call(
        functools.partial(_pool_kernel, nb=B, n_lat=N, s_tot=S),
        grid=(nt,),
        in_specs=[
            pl.BlockSpec((B, POOL_HALO, g), lambda i: (0, jnp.maximum(i * hb - 1, 0), 0)),
            pl.BlockSpec((B, TILE, g), lambda i: (0, i, 0)),
            pl.BlockSpec((B, POOL_HALO, g), lambda i: (0, jnp.minimum((i + 1) * hb, last), 0)),
            pl.BlockSpec((len(POOL_WINDOWS), TILE, TILE + 2 * POOL_HALO), lambda i: (0, 0, 0)),
            pl.BlockSpec((g, g), lambda i: (0, 0)),
            pl.BlockSpec((1, g), lambda i: (0, 0)),
        ],
        out_specs=pl.BlockSpec((B, TILE, g), lambda i: (0, i, 0)),
        out_shape=jax.ShapeDtypeStruct((B, S, g), BF16),
        name="pool_mix",
    )(bp, bp, bp, _pool_bands(), pw_bd, pscale)


def _attn_window_bias(n_ctx):
    r = (np.arange(2 * TILE) % TILE)[:, None]
    c = np.arange(TILE + 2 * WINDOW)[None, :] - WINDOW
    band = np.where(np.abs(r - c) <= WINDOW, 0.0, NEG_INF)
    return jnp.asarray(np.concatenate([band, np.zeros((2 * TILE, n_ctx))], axis=1), dtype=F32)


def _attn_kernel(sink_ref, bias_ref, q_ref, kp_ref, kc_ref, kn_ref, kx_ref, vp_ref, vc_ref, vn_ref, vx_ref, o_ref,
                 *, nb, n_lat):
    i = pl.program_id(0)
    t0 = i * TILE
    nband = TILE + 2 * WINDOW
    nkeys = nband + kx_ref.shape[1]
    col = lax.broadcasted_iota(I32, (1, nkeys), 1)
    kpos = t0 - WINDOW + col
    valid = (col >= nband) | ((kpos >= 0) & (kpos < n_lat) & (t0 < n_lat))
    bias = bias_ref[...]
    lane = lax.broadcasted_iota(I32, (TILE, 2 * HEAD_DIM), 1)
    low = lane < HEAD_DIM
    keep_low = jnp.where(low, 1.0, 0.0).astype(BF16)
    keep_high = jnp.where(low, 0.0, 1.0).astype(BF16)
    row1 = lax.broadcasted_iota(I32, (2 * TILE, 1), 0)
    for b in range(nb):
        outs = []
        for h in range(2):
            sl = slice(2 * HEAD_DIM * h, 2 * HEAD_DIM * (h + 1))
            qh = q_ref[b, :, sl]
            qs = jnp.concatenate([qh * keep_low, qh * keep_high], axis=0)
            kh = jnp.concatenate([kp_ref[b, :, sl], kc_ref[b, :, sl], kn_ref[b, :, sl], kx_ref[b, :, sl]], axis=0)
            vh = jnp.concatenate([vp_ref[b, :, sl], vc_ref[b, :, sl], vn_ref[b, :, sl], vx_ref[b, :, sl]], axis=0)
            s = jnp.where(valid, _dot_nt(qs, kh) + bias, NEG_INF)
            sink = jnp.where(row1 < TILE, sink_ref[2 * h], sink_ref[2 * h + 1])
            m = jnp.maximum(jnp.max(s, axis=-1, keepdims=True), sink)
            p = jnp.exp(s - m)
            den = jnp.sum(p, axis=-1, keepdims=True) + jnp.exp(sink - m)
            o = _dot(p.astype(BF16), vh) / den
            outs.append(jnp.where(low, o[:TILE], o[TILE:]))
        o_ref[b] = jnp.concatenate(outs, axis=-1).astype(BF16)


def _attention(sinks, q, k2, v2, *, B, N, L):
    S = N + L
    nt = S // TILE
    g = MIX_GROUP_W
    hb = TILE // WINDOW
    last = N // WINDOW - 1
    ctx_tile = N // TILE
    prev = pl.BlockSpec((B, WINDOW, g), lambda i: (0, jnp.clip(i * hb - 1, 0, last), 0))
    cur = pl.BlockSpec((B, TILE, g), lambda i: (0, i, 0))
    nxt = pl.BlockSpec((B, WINDOW, g), lambda i: (0, jnp.minimum((i + 1) * hb, last), 0))
    ctx = pl.BlockSpec((B, L, g), lambda i: (0, ctx_tile, 0))
    return pl.pallas_call(
        functools.partial(_attn_kernel, nb=B, n_lat=N),
        grid=(nt,),
        in_specs=[pl.BlockSpec(memory_space=pltpu.SMEM),
                  pl.BlockSpec((2 * TILE, TILE + 2 * WINDOW + L), lambda i: (0, 0)),
                  cur, prev, cur, nxt, ctx, prev, cur, nxt, ctx],
        out_specs=pl.BlockSpec((B, TILE, g), lambda i: (0, i, 0)),
        out_shape=jax.ShapeDtypeStruct((B, S, g), BF16),
        compiler_params=_cparams(vmem=VMEM_LIMIT),
        name="banded_attention",
    )(sinks, _attn_window_bias(L), q, k2, k2, k2, k2, v2, v2, v2, v2)


def _ret_kernel(*refs, reverse, nb):
    if reverse:
        dec_ref, q_ref, k_ref, v_ref, of_ref, g_ref, o_ref, r_scr = refs
    else:
        dec_ref, q_ref, k_ref, v_ref, o_ref, r_scr = refs
    c = RET_CHUNK
    w = MIX_GROUP_W
    dh = w // RET_HEADS

    @pl.when(pl.program_id(0) == 0)
    def _():
        r_scr[...] = jnp.zeros(r_scr.shape, F32)

    lg = -jnp.exp(dec_ref[...])
    idx = lax.broadcasted_iota(I32, (c, 1), 0).astype(F32)
    jdx = lax.broadcasted_iota(I32, (1, c), 1).astype(F32)
    if reverse:
        idx, jdx = (c - 1.0) - idx, (c - 1.0) - jdx
    diff = idx - jdx
    dmask_all = jnp.concatenate(
        [jnp.where(diff >= 0, jnp.exp(jnp.maximum(diff, 0.0) * lg[:, dh * h:dh * h + 1]), 0.0)
         for h in range(RET_HEADS)], axis=1)
    xi = jnp.exp((idx + 1.0) * lg)
    zeta = jnp.exp((c - 1.0 - idx) * lg)
    chunk_decay = jnp.exp(float(c) * lg)
    lane_head = lax.broadcasted_iota(I32, (1, w), 1) // dh
    keep_head = [jnp.where(lane_head == h, 1.0, 0.0).astype(BF16) for h in range(RET_HEADS)]
    same_head = (lax.broadcasted_iota(I32, (w, w), 0) // dh) == (lax.broadcasted_iota(I32, (w, w), 1) // dh)

    for b in range(nb):
        q = q_ref[b]
        k = k_ref[b] * jnp.asarray(dh ** -0.5, BF16)
        v = v_ref[b]
        state = r_scr[b]
        k_heads = jnp.concatenate([k * keep_head[h] for h in range(RET_HEADS)], axis=0)
        v_heads = jnp.concatenate([v * keep_head[h] for h in range(RET_HEADS)], axis=0)
        s = _dot_nt(q, k_heads) * dmask_all
        o = _dot(q, state.astype(BF16)) * xi + _dot(s.astype(BF16), v_heads)
        kv = _dot_tn(k, (v.astype(F32) * zeta).astype(BF16))
        r_scr[b] = state * chunk_decay + jnp.where(same_head, kv, 0.0)
        if not reverse:
            o_ref[b] = o
        else:
            tot = o + of_ref[b]
            mu = jnp.zeros_like(tot)
            for h in range(RET_HEADS):
                mh = lane_head == h
                mu = jnp.where(mh, jnp.sum(jnp.where(mh, tot, 0.0), axis=-1, keepdims=True) / dh, mu)
            cen = tot - mu
            var = jnp.zeros_like(tot)
            for h in range(RET_HEADS):
                mh = lane_head == h
                var = jnp.where(mh, jnp.sum(jnp.where(mh, cen * cen, 0.0), axis=-1, keepdims=True) / dh, var)
            gate = g_ref[b].astype(F32)
            o_ref[b] = (gate * jax.nn.sigmoid(gate) * (cen * lax.rsqrt(var + LN_EPS))).astype(BF16)


def _retention(dec, rq, rk, rv, rg, *, B, N, L):
    S = N + L
    c = RET_CHUNK
    w = MIX_GROUP_W
    nc_lat, nc_ctx = N // c, L // c
    nc = nc_lat + nc_ctx

    def fwd_blk(j):
        return jnp.where(j < nc_ctx, nc_lat + j, j - nc_ctx)

    def bwd_blk(j):
        return nc - 1 - j

    def call(blk, reverse, extra):
        spec = pl.BlockSpec((B, c, w), lambda j: (0, blk(j), 0))
        dspec = pl.BlockSpec((1, w), lambda j: (0, 0))
        return pl.pallas_call(
            functools.partial(_ret_kernel, reverse=reverse, nb=B),
            grid=(nc,),
            in_specs=[dspec] + [spec] * (3 + len(extra)),
            out_specs=spec,
            out_shape=jax.ShapeDtypeStruct((B, S, w), BF16 if reverse else F32),
            scratch_shapes=[pltpu.VMEM((B, w, w), F32)],
            compiler_params=_cparams(sem=("arbitrary",), vmem=VMEM_LIMIT),
            name="retention_bwd" if reverse else "retention_fwd",
        )(dec[1:2] if reverse else dec[0:1], rq, rk, rv, *extra)

    o_f = call(fwd_blk, False, ())
    return call(bwd_blk, True, (o_f, rg))


def _merge_kernel(*refs, nb, split_input, n_lat_tiles, alpha):
    for b in range(nb):
        x, rest = _tile_inputs(refs, b, split_input, n_lat_tiles)
        (yf_ref, yp_ref, ya_ref, yr_ref, g1_ref, sc_ref, sh_ref, lg_ref, lb_ref, wo_ref, rw_ref, rb_ref,
         x1_ref, h2_ref) = rest[:14]
        row = _mod_row(b, n_lat_tiles, nb)
        ycat = jnp.concatenate([yf_ref[b], yp_ref[b], ya_ref[b], yr_ref[b]], axis=-1)
        y = _dot(ycat, wo_ref[...])
        x1 = _normalize(alpha * x + g1_ref[row] * y) * lg_ref[...] + lb_ref[...]
        x1_ref[b] = x1
        h2 = _normalize(x1) * (1.0 + sc_ref[row]) + sh_ref[row]
        _to_token_tiles(h2_ref.at[b], h2)
        logits = _dot_nt(rw_ref[...], h2.astype(BF16)) + rb_ref[...]
        _route(logits, rest[14 + b], rest[14 + nb + b])


def _route(logits, idx_ref, gate_ref):
    eid = lax.broadcasted_iota(I32, logits.shape, 0)
    orow = lax.broadcasted_iota(I32, (8, logits.shape[1]), 0)
    idx_out = jnp.zeros((8, logits.shape[1]), I32)
    val_out = jnp.zeros((8, logits.shape[1]), F32)
    top = None
    den = None
    for kk in range(TOP_K):
        m = jnp.max(logits, axis=0, keepdims=True)
        sel = jnp.min(jnp.where(logits == m, eid, N_EXPERTS), axis=0, keepdims=True)
        logits = jnp.where(eid == sel, -jnp.inf, logits)
        if kk == 0:
            top = m
        e = jnp.exp(m - top)
        den = e if kk == 0 else den + e
        idx_out = jnp.where(orow == kk, sel, idx_out)
        val_out = jnp.where(orow == kk, e, val_out)
    idx_ref[...] = idx_out
    gate_ref[...] = val_out / den


def _merge(xs, ys, g1, sc2, sh2, ln_g, ln_b, w_out, rw_t, rb, *, B, S, n_lat_tiles, alpha):
    split_input = len(xs) == 2
    nt = S // TILE
    g = MIX_GROUP_W
    r = ROWS_PER_TOKEN
    yspec = pl.BlockSpec((B, TILE, g), lambda i: (0, i, 0))
    vec = pl.BlockSpec((1, D_MODEL), lambda i: (0, 0))
    route_spec = pl.BlockSpec((8, TILE), lambda i: (0, i))
    outs = pl.pallas_call(
        functools.partial(_merge_kernel, nb=B, split_input=split_input, n_lat_tiles=n_lat_tiles, alpha=alpha),
        grid=(nt,),
        in_specs=_x_specs(B, split_input, n_lat_tiles) + [yspec] * 4 + [
            _mod_spec(), _mod_spec(), _mod_spec(), vec, vec,
            pl.BlockSpec((D_MODEL, D_MODEL), lambda i: (0, 0)),
            pl.BlockSpec((N_EXPERTS, D_MODEL), lambda i: (0, 0)),
            pl.BlockSpec((N_EXPERTS, 1), lambda i: (0, 0)),
        ],
        out_specs=[
            pl.BlockSpec((B, TILE, D_MODEL), lambda i: (0, i, 0)),
            pl.BlockSpec((B, TILE * r, 128), lambda i: (0, i, 0)),
        ] + [route_spec] * (2 * B),
        out_shape=[
            jax.ShapeDtypeStruct((B, S, D_MODEL), F32),
            jax.ShapeDtypeStruct((B, S * r, 128), F32),
        ] + [jax.ShapeDtypeStruct((8, S), I32)] * B + [jax.ShapeDtypeStruct((8, S), F32)] * B,
        compiler_params=_cparams(vmem=VMEM_LIMIT),
        name="out_proj_norm_router",
    )(*xs, *ys, g1, sc2, sh2, ln_g, ln_b, w_out, rw_t, rb)
    x1, h2t = outs[:2]
    idx = jnp.concatenate(outs[2:2 + B], axis=1)
    gates = jnp.concatenate(outs[2 + B:], axis=1)
    return x1, h2t.reshape(B * S * r, 128), idx, gates


def _rank_kernel(idx_ref, lpos_ref, ntab_ref, cbtab_ref, cnt_ref, carry_scr):
    tt = idx_ref.shape[1]

    @pl.when(pl.program_id(0) == 0)
    def _():
        carry_scr[...] = jnp.zeros(carry_scr.shape, F32)

    eid = lax.broadcasted_iota(I32, (N_EXPERTS, tt), 0)
    before = jnp.where(lax.broadcasted_iota(I32, (tt, tt), 0) < lax.broadcasted_iota(I32, (tt, tt), 1),
                       1.0, 0.0).astype(BF16)
    hits, cums, pres = [], [], []
    n_tile = jnp.zeros((N_EXPERTS, 1), F32)
    for kk in range(TOP_K):
        hit = eid == idx_ref[kk:kk + 1, :]
        hits.append(hit)
        cums.append(_dot(jnp.where(hit, 1.0, 0.0).astype(BF16), before))
        pres.append(n_tile)
        n_tile = n_tile + jnp.sum(jnp.where(hit, 1.0, 0.0), axis=1, keepdims=True)
    n16 = ((n_tile.astype(I32) + (RUN_ALIGN - 1)) >> RUN_SHIFT) << RUN_SHIFT
    lower = jnp.where(lax.broadcasted_iota(I32, (N_EXPERTS, N_EXPERTS), 1)
                      < lax.broadcasted_iota(I32, (N_EXPERTS, N_EXPERTS), 0), 1.0, 0.0).astype(BF16)
    hi = jnp.broadcast_to((n16 >> 8).astype(F32), (N_EXPERTS, 128)).astype(BF16)
    lo = jnp.broadcast_to((n16 & 255).astype(F32), (N_EXPERTS, 128)).astype(BF16)
    lstart = (256.0 * _dot(lower, hi) + _dot(lower, lo))[:, 0:1]
    orow = lax.broadcasted_iota(I32, (8, tt), 0)
    out = jnp.zeros((8, tt), I32)
    for kk in range(TOP_K):
        slot = jnp.sum(jnp.where(hits[kk], lstart + pres[kk] + cums[kk], 0.0), axis=0, keepdims=True)
        out = jnp.where(orow == kk, slot.astype(I32) * ROWS_PER_TOKEN, out)
    lpos_ref[...] = out
    carry = carry_scr[...]
    ntab_ref[...] = jnp.broadcast_to(n_tile, ntab_ref.shape).astype(I32)
    cbtab_ref[...] = carry.astype(I32)
    carry = carry + n_tile
    carry_scr[...] = carry
    cnt_ref[...] = carry


def _rank(idx, *, T):
    tt = MOE_TOK_TILE
    nt = T // tt
    tab = pl.BlockSpec((None, N_EXPERTS, 128), lambda i: (i, 0, 0))
    return pl.pallas_call(
        _rank_kernel,
        grid=(nt,),
        in_specs=[pl.BlockSpec((8, tt), lambda i: (0, i))],
        out_specs=[pl.BlockSpec((8, tt), lambda i: (0, i)), tab, tab,
                   pl.BlockSpec((N_EXPERTS, 128), lambda i: (0, 0))],
        out_shape=[jax.ShapeDtypeStruct((8, T), I32), jax.ShapeDtypeStruct((nt, N_EXPERTS, 128), I32),
                   jax.ShapeDtypeStruct((nt, N_EXPERTS, 128), I32), jax.ShapeDtypeStruct((N_EXPERTS, 128), F32)],
        scratch_shapes=[pltpu.VMEM((N_EXPERTS, 128), F32)],
        compiler_params=_cparams(sem=("arbitrary",)),
        name="moe_rank",
    )(idx)


def _run_chunk(src_ref, src_tok, dst_ref, dst_tok, sem):
    return pltpu.make_async_copy(src_ref.at[_token_rows(src_tok, RUN_ALIGN), :],
                                 dst_ref.at[_token_rows(dst_tok, RUN_ALIGN), :], sem)


def _wait_chunks(src_ref, dst_ref, sem, count):
    def body(_, carry):
        _run_chunk(src_ref, 0, dst_ref, 0, sem).wait()
        return carry
    lax.fori_loop(0, count, body, 0)


def _copy_runs(step, pstart_ref, ntab_ref, cbtab_ref, chunk):
    def per_expert(e, carry):
        src0, total = carry
        nch = (ntab_ref[step * N_EXPERTS + e] + (RUN_ALIGN - 1)) >> RUN_SHIFT
        dst0 = pstart_ref[e] + cbtab_ref[step * N_EXPERTS + e]

        def one(c, cc):
            chunk(src0 + c * RUN_ALIGN, dst0 + c * RUN_ALIGN).start()
            return cc
        lax.fori_loop(0, nch, one, 0)
        return src0 + nch * RUN_ALIGN, total + nch
    return lax.fori_loop(0, N_EXPERTS, per_expert, (0, 0))[1]


def _dispatch_kernel(pstart_ref, cnt_ref, pad_ref, ntab_ref, cbtab_ref, lpos_ref, h_ref, xs_ref,
                     stage, zero_scr, sems, sent):
    step = pl.program_id(0)

    @pl.when(step == 0)
    def _():
        sem = sems.at[0]
        zero_scr[...] = jnp.zeros(zero_scr.shape, F32)

        def per_expert(e, total):
            first = pstart_ref[e] + ((cnt_ref[e] >> RUN_SHIFT) << RUN_SHIFT)
            n_fill = (pstart_ref[e] + pad_ref[e] - first) >> RUN_SHIFT

            def fill(c, carry):
                _run_chunk(zero_scr, 0, xs_ref, first + c * RUN_ALIGN, sem).start()
                return carry
            lax.fori_loop(0, n_fill, fill, 0)
            return total + n_fill
        n_total = lax.fori_loop(0, N_EXPERTS, per_expert, 0)
        used = pstart_ref[N_EXPERTS - 1] + pad_ref[N_EXPERTS - 1]
        n_tail = (xs_ref.shape[0] // ROWS_PER_TOKEN - used) >> RUN_SHIFT

        def fill_tail(c, carry):
            _run_chunk(zero_scr, 0, xs_ref, used + c * RUN_ALIGN, sem).start()
            return carry
        lax.fori_loop(0, n_tail, fill_tail, 0)
        _wait_chunks(zero_scr, xs_ref, sem, n_total + n_tail)
        sent[0] = 0
        sent[1] = 0

    slot = step % 2
    buf = stage.at[slot]
    sem = sems.at[slot]

    def zero_last_group(e, src0):
        n_up = ((ntab_ref[step * N_EXPERTS + e] + (RUN_ALIGN - 1)) >> RUN_SHIFT) << RUN_SHIFT
        buf[_token_rows(jnp.maximum(src0 + n_up - RUN_ALIGN, 0), RUN_ALIGN), :] = jnp.zeros((CHUNK_ROWS, 128), F32)
        return src0 + n_up
    lax.fori_loop(0, N_EXPERTS, zero_last_group, 0)
    tt = MOE_TOK_TILE
    for kk in range(TOP_K):
        def place(t, carry, kk=kk):
            row = pl.multiple_of(lpos_ref[kk * tt + t], ROWS_PER_TOKEN)
            buf[pl.ds(row, ROWS_PER_TOKEN), :] = h_ref[_token_rows(t), :]
            return carry
        lax.fori_loop(0, tt, place, 0, unroll=16)
    _wait_chunks(stage.at[1 - slot], xs_ref, sems.at[1 - slot], sent[1 - slot])
    sent[1 - slot] = 0
    sent[slot] = _copy_runs(step, pstart_ref, ntab_ref, cbtab_ref,
                            lambda s, d: _run_chunk(buf, s, xs_ref, d, sem))

    @pl.when(step == pl.num_programs(0) - 1)
    def _():
        _wait_chunks(buf, xs_ref, sem, sent[slot])


def _smem_tile_spec(tt):
    return pl.BlockSpec((8 * tt,), lambda i, *_: (i,), memory_space=pltpu.SMEM)


def _tile_major(table):
    tt = MOE_TOK_TILE
    return table.reshape(table.shape[0], -1, tt).transpose(1, 0, 2).reshape(-1)


def _dispatch(pstart, cnt, padded, ntab, cbtab, lpos, h2t, *, T, P):
    tt = MOE_TOK_TILE
    r = ROWS_PER_TOKEN
    return pl.pallas_call(
        _dispatch_kernel,
        grid_spec=pltpu.PrefetchScalarGridSpec(
            num_scalar_prefetch=5,
            grid=(T // tt,),
            in_specs=[_smem_tile_spec(tt), pl.BlockSpec((tt * r, 128), lambda i, *_: (i, 0))],
            out_specs=pl.BlockSpec(memory_space=pl.ANY),
            scratch_shapes=[pltpu.VMEM((2, STAGE_TOKENS * r, 128), F32), pltpu.VMEM((CHUNK_ROWS, 128), F32),
                            pltpu.SemaphoreType.DMA((2,)), pltpu.SMEM((2,), I32)],
        ),
        out_shape=jax.ShapeDtypeStruct((P * r, 128), F32),
        compiler_params=_cparams(sem=("arbitrary",), vmem=EXPERT_VMEM_LIMIT),
        name="moe_dispatch",
    )(pstart, cnt, padded, ntab, cbtab, _tile_major(lpos), h2t)


def _collect_kernel(pstart_ref, ntab_ref, cbtab_ref, lpos_ref, gate_ref, ys_ref, f_ref, stage, sems, sent):
    step = pl.program_id(0)
    slot = step % 2

    def fetch(tile, b):
        sent[b] = _copy_runs(tile, pstart_ref, ntab_ref, cbtab_ref,
                             lambda s, d: _run_chunk(ys_ref, d, stage.at[b], s, sems.at[b]))

    @pl.when(step == 0)
    def _():
        fetch(0, 0)

    @pl.when(step + 1 < pl.num_programs(0))
    def _():
        fetch(step + 1, 1 - slot)

    buf = stage.at[slot]
    _wait_chunks(ys_ref, buf, sems.at[slot], sent[slot])

    tt = MOE_TOK_TILE

    def combine(t, carry):
        acc = None
        for kk in range(TOP_K):
            row = pl.multiple_of(lpos_ref[kk * tt + t], ROWS_PER_TOKEN)
            term = gate_ref[kk * tt + t] * buf[pl.ds(row, ROWS_PER_TOKEN), :]
            acc = term if acc is None else acc + term
        f_ref[_token_rows(t), :] = acc
        return carry
    lax.fori_loop(0, tt, combine, 0, unroll=4)


def _collect(pstart, ntab, cbtab, lpos, gates, ys, *, T):
    tt = MOE_TOK_TILE
    r = ROWS_PER_TOKEN
    return pl.pallas_call(
        _collect_kernel,
        grid_spec=pltpu.PrefetchScalarGridSpec(
            num_scalar_prefetch=3,
            grid=(T // tt,),
            in_specs=[_smem_tile_spec(tt), _smem_tile_spec(tt), pl.BlockSpec(memory_space=pl.ANY)],
            out_specs=pl.BlockSpec((tt * r, 128), lambda i, *_: (i, 0)),
            scratch_shapes=[pltpu.VMEM((2, STAGE_TOKENS * r, 128), F32), pltpu.SemaphoreType.DMA((2,)),
                            pltpu.SMEM((2,), I32)],
        ),
        out_shape=jax.ShapeDtypeStruct((T * r, 128), F32),
        compiler_params=_cparams(sem=("arbitrary",), vmem=EXPERT_VMEM_LIMIT),
        name="moe_collect",
    )(pstart, ntab, cbtab, _tile_major(lpos), _tile_major(gates), ys)


def _deinterleave_matrix():
    m = np.zeros((256, 256), np.float32)
    j = np.arange(128)
    m[2 * j, j] = 1.0
    m[2 * j + 1, 128 + j] = 1.0
    return jnp.asarray(m, dtype=BF16)


def _expert_kernel(be_ref, nu_ref, x_ref, w1_ref, b1_ref, w2_ref, b2_ref, perm_ref, y_ref, w1_scr, w2_scr):
    i = pl.program_id(0)

    @pl.when(i >= nu_ref[0])
    def _():
        y_ref[...] = jnp.zeros(y_ref.shape, F32)

    @pl.when(i < nu_ref[0])
    def _():
        @pl.when((i == 0) | (be_ref[i] != be_ref[jnp.maximum(i - 1, 0)]))
        def _():
            half = w1_scr.shape[1] // 2
            for m in range(w1_ref.shape[1] // 256):
                r = _dot(w1_ref[:, 256 * m:256 * (m + 1)].astype(BF16), perm_ref[...])
                w1_scr[:, 128 * m:128 * (m + 1)] = r[:, :128].astype(BF16)
                w1_scr[:, half + 128 * m:half + 128 * (m + 1)] = r[:, 128:].astype(BF16)
            w2_scr[...] = w2_ref[...].astype(BF16)

        x = _from_token_tiles(x_ref, MOE_BLK).astype(BF16)
        u = _dot(x, w1_scr[...]) + b1_ref[...]
        half = u.shape[1] // 2
        glu = jnp.minimum(u[:, :half], SWIGLU_LIMIT)
        lin = jnp.clip(u[:, half:], -SWIGLU_LIMIT, SWIGLU_LIMIT)
        act = glu * jax.nn.sigmoid(SWIGLU_ALPHA * glu) * (lin + 1.0)
        _to_token_tiles(y_ref, _dot(act.astype(BF16), w2_scr[...]) + b2_ref[...])


def _experts(blk_e, n_used, xs, w1, b1, w2, b2, *, P, layer):
    nblk = P // MOE_BLK
    r = ROWS_PER_TOKEN
    d, two_f = w1.shape[2:]
    f = w2.shape[2]
    rows = lambda i, be, nu: (jnp.minimum(i, nu[0] - 1), 0)
    return pl.pallas_call(
        _expert_kernel,
        grid_spec=pltpu.PrefetchScalarGridSpec(
            num_scalar_prefetch=2,
            grid=(nblk,),
            in_specs=[
                pl.BlockSpec((MOE_BLK * r, 128), rows),
                pl.BlockSpec((None, None, d, two_f), lambda i, be, nu: (layer, be[i], 0, 0)),
                pl.BlockSpec((None, 1, two_f), lambda i, be, nu: (be[i], 0, 0)),
                pl.BlockSpec((None, None, f, d), lambda i, be, nu: (layer, be[i], 0, 0)),
                pl.BlockSpec((None, 1, d), lambda i, be, nu: (be[i], 0, 0)),
                pl.BlockSpec((256, 256), lambda i, be, nu: (0, 0)),
            ],
            out_specs=pl.BlockSpec((MOE_BLK * r, 128), lambda i, be, nu: (i, 0)),
            scratch_shapes=[pltpu.VMEM((d, two_f), BF16), pltpu.VMEM((f, d), BF16)],
        ),
        out_shape=jax.ShapeDtypeStruct((P * r, 128), F32),
        compiler_params=_cparams(sem=("arbitrary",), vmem=EXPERT_VMEM_LIMIT),
        name="moe_experts",
    )(blk_e, n_used, xs, w1, b1, w2, b2, _deinterleave_matrix())


def _moe(h2t, idx, gates, w1, b1, w2, b2, *, T, layer):
    lpos, ntab, cbtab, cnt = _rank(idx, T=T)
    counts = cnt[:, 0].astype(I32)
    padded = (counts + RUN_ALIGN + MOE_BLK - 1) // MOE_BLK * MOE_BLK
    pad_end = jnp.cumsum(padded)
    pstart = (pad_end - padded).astype(I32)
    nblk = TOP_K * T // MOE_BLK + 2 * N_EXPERTS
    P = nblk * MOE_BLK
    starts = jnp.arange(nblk, dtype=I32) * MOE_BLK
    blk_e = jnp.minimum(jnp.sum((pad_end[None, :] <= starts[:, None]).astype(I32), axis=1), N_EXPERTS - 1)
    n_used = (pad_end[-1:] // MOE_BLK).astype(I32)
    ntab = ntab[:, :, 0].reshape(-1)
    cbtab = cbtab[:, :, 0].reshape(-1)
    xs = _dispatch(pstart, counts, padded.astype(I32), ntab, cbtab, lpos, h2t, T=T, P=P)
    ys = _experts(blk_e.astype(I32), n_used, xs, w1, b1, w2, b2, P=P, layer=layer)
    return _collect(pstart, ntab, cbtab, lpos, gates, ys, T=T)


def _residual_kernel(f_ref, x_ref, g2_ref, lg_ref, lb_ref, o_ref, *, nb, n_lat_tiles, alpha):
    for b in range(nb):
        f = _from_token_tiles(f_ref.at[b], TILE)
        z = alpha * x_ref[b] + g2_ref[_mod_row(b, n_lat_tiles, nb)] * f
        o_ref[b] = _normalize(z) * lg_ref[...] + lb_ref[...]


def _residual(ft, x1, g2, ln_g, ln_b, *, B, S, n_out_tiles, n_lat_tiles, alpha):
    r = ROWS_PER_TOKEN
    vec = pl.BlockSpec((1, D_MODEL), lambda i: (0, 0))
    return pl.pallas_call(
        functools.partial(_residual_kernel, nb=B, n_lat_tiles=n_lat_tiles, alpha=alpha),
        grid=(n_out_tiles,),
        in_specs=[
            pl.BlockSpec((B, TILE * r, 128), lambda i: (0, i, 0)),
            pl.BlockSpec((B, TILE, D_MODEL), lambda i: (0, i, 0)),
            _mod_spec(), vec, vec,
        ],
        out_specs=pl.BlockSpec((B, TILE, D_MODEL), lambda i: (0, i, 0)),
        out_shape=jax.ShapeDtypeStruct((B, n_out_tiles * TILE, D_MODEL), F32),
        compiler_params=_cparams(vmem=VMEM_LIMIT),
        name="ffn_residual_norm",
    )(ft.reshape(B, S * r, 128), x1, g2, ln_g, ln_b)


def _rope_tables(n_lat, n_ctx):
    half = HEAD_DIM // 2
    pos = np.arange(n_lat)
    inv = 1.0 / (ROPE_BASE ** (np.arange(0, half, 2, dtype=np.float64) / half))
    ang_row = (pos // GRID_W)[:, None].astype(np.float64) * inv[None, :]
    ang_col = (pos % GRID_W)[:, None].astype(np.float64) * inv[None, :]
    ang = np.concatenate([ang_row, ang_row, ang_col, ang_col], axis=1)
    sign = np.concatenate([-np.ones(16), np.ones(16)] * 2)[None, :]
    cos = np.concatenate([np.cos(ang), np.ones((n_ctx, HEAD_DIM))], axis=0)
    sin = np.concatenate([np.sin(ang) * sign, np.zeros((n_ctx, HEAD_DIM))], axis=0)
    return (jnp.asarray(np.tile(cos, (1, 2)), dtype=F32), jnp.asarray(np.tile(sin, (1, 2)), dtype=F32))


def _extend_w_in(w):
    g = MIX_GROUP_W
    hd = HEAD_DIM
    a, bp, q = w[:, 0:g], w[:, g:2 * g], w[:, 2 * g:3 * g]
    k, v = w[:, 3 * g:3 * g + 2 * hd], w[:, 3 * g + 2 * hd:4 * g]
    rest = w[:, 4 * g:]
    dup = lambda t: jnp.concatenate([t[:, :hd], t[:, :hd], t[:, hd:], t[:, hd:]], axis=1)
    return jnp.concatenate([a, bp, q, dup(k), dup(v), rest], axis=1).astype(BF16)


def _block_diag(pw):
    n, c, _ = pw.shape
    out = jnp.zeros((n * c, n * c), pw.dtype)
    for gi in range(n):
        out = lax.dynamic_update_slice(out, pw[gi], (gi * c, gi * c))
    return out


def kernel(x, c, ctx, c_ctx, w_mod, b_mod, w_in, pool_w, pool_scale, attn_sink, ret_decay, w_out, ln_g, ln_b,
           router_w, router_b, exp_w1, exp_b1, exp_w2, exp_b2):
    B, N, D = x.shape
    L = ctx.shape[1]
    depth = w_mod.shape[0]
    assert D == D_MODEL and L == TILE and N % TILE == 0 and N % (FFT_RADIX2 * 16) == 0 and B < 8
    S = N + L
    n_lat_tiles = N // TILE
    T = B * S
    assert T % MOE_TOK_TILE == 0
    alpha = float((2 * depth) ** 0.25)

    cvec = jnp.zeros((8, D), F32).at[:B].set(c).at[B].set(c_ctx)
    mods = _modulation(cvec, w_mod, b_mod)
    cos, sin = _rope_tables(N, L)

    xs = (x, ctx)
    out = None
    for l in range(depth):
        last = l == depth - 1
        m = mods[l].reshape(8, 6, 1, D)
        sh1, sc1, g1, sh2, sc2, g2 = [m[:, j] for j in range(6)]
        proj = _projection(xs, sc1, sh1, _extend_w_in(w_in[l]), cos, sin, B=B, S=S, n_lat_tiles=n_lat_tiles)
        a, bp, q, k2, v2, rq, rk, rv, rg = proj
        y_f = _fourier_mix(a, B=B, N=N, L=L, with_ctx=not last)
        y_p = _pool_mix(bp, _block_diag(pool_w[l]).astype(BF16), pool_scale[l].reshape(1, -1), B=B, N=N, L=L)
        y_a = _attention(attn_sink[l], q, k2, v2, B=B, N=N, L=L)
        dec = jnp.repeat(ret_decay[l], MIX_GROUP_W // RET_HEADS, axis=1)
        y_r = _retention(dec, rq, rk, rv, rg, B=B, N=N, L=L)
        x1, h2t, idx, gates = _merge(
            xs, (y_f, y_p, y_a, y_r), g1, sc2, sh2, ln_g[l, 0:1], ln_b[l, 0:1], w_out[l].astype(BF16),
            router_w[l].T.astype(BF16), router_b[l].reshape(-1, 1), B=B, S=S, n_lat_tiles=n_lat_tiles, alpha=alpha)
        b1 = jnp.concatenate([exp_b1[l][:, 0::2], exp_b1[l][:, 1::2]], axis=-1)[:, None, :]
        ft = _moe(h2t, idx, gates, exp_w1, b1, exp_w2, exp_b2[l][:, None, :], T=T, layer=l)
        out = _residual(ft, x1, g2, ln_g[l, 1:2], ln_b[l, 1:2], B=B, S=S,
                        n_out_tiles=n_lat_tiles if last else S // TILE, n_lat_tiles=n_lat_tiles, alpha=alpha)
        xs = (out,)
    return out
```

```python
import functools

import numpy as np
import jax
import jax.numpy as jnp
from jax import lax
from jax.experimental import pallas as pl
from jax.experimental.pallas import tpu as pltpu

F32 = jnp.float32
BF16 = jnp.bfloat16
I32 = jnp.int32

D_MODEL = 1024
GRID_W = 64
FOURIER_GROUP_W = 64
POOL_WINDOWS = (2, 4, 8, 16)
HEAD_DIM = 64
WINDOW = 128
ROPE_BASE = 10000.0
RET_HEADS = 4
RET_CHUNK = 128
N_EXPERTS = 32
TOP_K = 4
SWIGLU_LIMIT = 7.0
SWIGLU_ALPHA = 1.702
LN_EPS = 1e-6
NEG_INF = -1e30
MIX_GROUP_W = 256
FFT_RADIX2 = 64

TILE = 256
MOE_BLK = 512
MOE_TOK_TILE = 1024
ROWS_PER_TOKEN = D_MODEL // 128
RUN_ALIGN = 16
RUN_SHIFT = 4
CHUNK_ROWS = RUN_ALIGN * ROWS_PER_TOKEN
STAGE_TOKENS = TOP_K * MOE_TOK_TILE + N_EXPERTS * RUN_ALIGN
VMEM_LIMIT = 48 * 1024 * 1024
EXPERT_VMEM_LIMIT = 56 * 1024 * 1024


def _dot(a, b):
    return jnp.dot(a, b, preferred_element_type=F32)


def _dot_nt(a, b):
    return lax.dot_general(a, b, (((1,), (1,)), ((), ())), preferred_element_type=F32)


def _dot_tn(a, b):
    return lax.dot_general(a, b, (((0,), (0,)), ((), ())), preferred_element_type=F32)


def _normalize(x):
    mu = jnp.mean(x, axis=-1, keepdims=True)
    xc = x - mu
    var = jnp.mean(xc * xc, axis=-1, keepdims=True)
    return xc * lax.rsqrt(var + LN_EPS)


def _cparams(sem=None, vmem=None):
    kw = {}
    if sem is not None:
        kw["dimension_semantics"] = sem
    if vmem is not None:
        kw["vmem_limit_bytes"] = vmem
    return pltpu.CompilerParams(**kw)


def _to_token_tiles(ref, val):
    n = val.shape[0]
    for c in range(ROWS_PER_TOKEN):
        ref[pl.ds(c, n, stride=ROWS_PER_TOKEN), :] = val[:, 128 * c:128 * (c + 1)]


def _from_token_tiles(ref, n):
    return jnp.concatenate([ref[pl.ds(c, n, stride=ROWS_PER_TOKEN), :] for c in range(ROWS_PER_TOKEN)], axis=-1)


def _token_rows(tok, count=1):
    return pl.ds(pl.multiple_of(tok * ROWS_PER_TOKEN, ROWS_PER_TOKEN), count * ROWS_PER_TOKEN)


def _mod_kernel(c_ref, w_ref, b_ref, o_ref):
    c = c_ref[...]
    a = c * jax.nn.sigmoid(c)
    w = w_ref[...]
    a_hi = a.astype(BF16)
    a_lo = (a - a_hi.astype(F32)).astype(BF16)
    w_hi = w.astype(BF16)
    w_lo = (w - w_hi.astype(F32)).astype(BF16)
    o_ref[...] = _dot(a_hi, w_hi) + _dot(a_lo, w_hi) + _dot(a_hi, w_lo) + b_ref[...]


def _modulation(cvec, w_mod, b_mod):
    depth, d, six_d = w_mod.shape
    nb = 1536
    return pl.pallas_call(
        _mod_kernel,
        grid=(depth, six_d // nb),
        in_specs=[
            pl.BlockSpec((8, d), lambda l, j: (0, 0)),
            pl.BlockSpec((None, d, nb), lambda l, j: (l, 0, j)),
            pl.BlockSpec((None, 1, nb), lambda l, j: (l, 0, j)),
        ],
        out_specs=pl.BlockSpec((None, 8, nb), lambda l, j: (l, 0, j)),
        out_shape=jax.ShapeDtypeStruct((depth, 8, six_d), F32),
        compiler_params=_cparams(vmem=VMEM_LIMIT),
        name="modulation",
    )(cvec, w_mod, b_mod.reshape(depth, 1, six_d))


def _tile_inputs(refs, b, split_input, n_lat_tiles):
    if split_input:
        x = jnp.where(pl.program_id(0) == n_lat_tiles, refs[1][b], refs[0][b])
        return x, refs[2:]
    return refs[0][b], refs[1:]


def _mod_row(b, n_lat_tiles, ctx_row):
    return jnp.where(pl.program_id(0) == n_lat_tiles, ctx_row, b)


def _proj_kernel(*refs, nb, split_input, n_lat_tiles):
    g = MIX_GROUP_W
    for b in range(nb):
        x, rest = _tile_inputs(refs, b, split_input, n_lat_tiles)
        sc_ref, sh_ref, w_ref, cos_ref, sin_ref = rest[:5]
        a_ref, bp_ref, q_ref, k_ref, v_ref, rq_ref, rk_ref, rv_ref, rg_ref = rest[5:]
        row = _mod_row(b, n_lat_tiles, nb)
        h = _normalize(x) * (1.0 + sc_ref[row]) + sh_ref[row]
        p = _dot(h.astype(BF16), w_ref[...])
        a_ref[b] = p[:, 0:g].astype(BF16)
        bp_ref[b] = p[:, g:2 * g].astype(BF16)
        qk = p[:, 2 * g:4 * g]
        lane = lax.broadcasted_iota(I32, qk.shape, 1)
        first = (lane & 16) == 0
        partner = jnp.where(first, pltpu.roll(qk, 2 * g - 16, 1), pltpu.roll(qk, 16, 1))
        cos = jnp.concatenate([cos_ref[...]] * 4, axis=-1)
        sin = jnp.concatenate([sin_ref[...]] * 4, axis=-1)
        qk = qk * cos + partner * sin
        q_ref[b] = (qk[:, :g] * (HEAD_DIM ** -0.5)).astype(BF16)
        k_ref[b] = qk[:, g:].astype(BF16)
        v_ref[b] = p[:, 4 * g:5 * g].astype(BF16)
        rq_ref[b] = p[:, 5 * g:6 * g].astype(BF16)
        rk_ref[b] = p[:, 6 * g:7 * g].astype(BF16)
        rv_ref[b] = p[:, 7 * g:8 * g].astype(BF16)
        rg_ref[b] = p[:, 8 * g:9 * g].astype(BF16)


def _mod_spec():
    return pl.BlockSpec((8, 1, D_MODEL), lambda i: (0, 0, 0))


def _x_specs(B, split_input, n_lat_tiles):
    if split_input:
        return [
            pl.BlockSpec((B, TILE, D_MODEL), lambda i: (0, jnp.minimum(i, n_lat_tiles - 1), 0)),
            pl.BlockSpec((B, TILE, D_MODEL), lambda i: (0, 0, 0)),
        ]
    return [pl.BlockSpec((B, TILE, D_MODEL), lambda i: (0, i, 0))]


def _projection(xs, sc, sh, w_ext, cos, sin, *, B, S, n_lat_tiles):
    split_input = len(xs) == 2
    nt = S // TILE
    g = MIX_GROUP_W
    out_spec = pl.BlockSpec((B, TILE, g), lambda i: (0, i, 0))
    out_shape = jax.ShapeDtypeStruct((B, S, g), BF16)
    return pl.pallas_call(
        functools.partial(_proj_kernel, nb=B, split_input=split_input, n_lat_tiles=n_lat_tiles),
        grid=(nt,),
        in_specs=_x_specs(B, split_input, n_lat_tiles) + [
            _mod_spec(),
            _mod_spec(),
            pl.BlockSpec(w_ext.shape, lambda i: (0, 0)),
            pl.BlockSpec((TILE, 128), lambda i: (i, 0)),
            pl.BlockSpec((TILE, 128), lambda i: (i, 0)),
        ],
        out_specs=[out_spec] * 9,
        out_shape=[out_shape] * 9,
        compiler_params=_cparams(vmem=VMEM_LIMIT),
        name="ln_mod_in_proj",
    )(*xs, sc, sh, w_ext, cos, sin)


def _channel_dft_matrix():
    c = np.arange(FOURIER_GROUP_W)
    ang = 2.0 * np.pi * np.outer(c, c) / FOURIER_GROUP_W
    eye = np.eye(MIX_GROUP_W // FOURIER_GROUP_W)
    cs = np.concatenate([np.kron(eye, np.cos(ang)), -np.kron(eye, np.sin(ang))], axis=1)
    return jnp.asarray(cs, dtype=BF16)


def _seq_dft_stage_a(n1):
    n2 = FFT_RADIX2
    n = n1 * n2
    k1 = np.arange(n1)[None, :, None]
    i1 = np.arange(n1)[None, None, :]
    i2 = np.arange(n2)[:, None, None]
    th = 2.0 * np.pi * ((k1 * (n2 * i1 + i2)) % n) / n
    wr, wi = np.cos(th), -np.sin(th)
    wa = np.concatenate([np.concatenate([wr, -wi], axis=2), np.concatenate([wi, wr], axis=2)], axis=1)
    return jnp.asarray(wa, dtype=BF16)


def _seq_dft_stage_b():
    n2 = FFT_RADIX2
    k2 = np.arange(n2)
    ph = 2.0 * np.pi * (np.outer(k2, k2) % n2) / n2
    return jnp.asarray(np.concatenate([np.cos(ph), np.sin(ph)], axis=1), dtype=BF16)


def _dense_dft(length):
    k = np.arange(length)
    th = 2.0 * np.pi * (np.outer(k, k) % length) / length
    return jnp.asarray(np.concatenate([np.cos(th), np.sin(th)], axis=1), dtype=BF16)


FFT_A_COLS = 4
FFT_B_COLS = 8


def _fft_a_kernel(a_ref, cs_ref, wa_ref, t_ref, *, nb, n1):
    g = MIX_GROUP_W
    for j in range(FFT_A_COLS):
        for b in range(nb):
            p = _dot(a_ref[b, :, g * j:g * (j + 1)], cs_ref[...])
            pp = jnp.concatenate([p[:, :g], p[:, g:]], axis=0).astype(BF16)
            t = _dot(wa_ref[j], pp)
            t_ref[b, :, 2 * g * j:2 * g * (j + 1)] = jnp.concatenate([t[:n1], t[n1:]], axis=-1).astype(BF16)


def _fft_b_kernel(t_ref, vb_ref, y_ref, *, nb, scale):
    g = MIX_GROUP_W
    n2 = FFT_RADIX2
    for j in range(FFT_B_COLS):
        for b in range(nb):
            t = t_ref[b, n2 * j:n2 * (j + 1), :]
            tt = jnp.concatenate([t[:, :g], t[:, g:]], axis=0)
            y = _dot(vb_ref[...], tt) * scale
            y_ref[b, 0:n2, g * j:g * (j + 1)] = y.astype(BF16)
            y_ref[b, n2:, g * j:g * (j + 1)] = jnp.zeros((y_ref.shape[1] - n2, g), BF16)


def _fft_ctx_kernel(a_ref, cs_ref, wc_ref, yin_ref, y_ref, *, scale):
    del yin_ref
    g = MIX_GROUP_W
    p = _dot(a_ref[...], cs_ref[...])
    pp = jnp.concatenate([p[:, :g], p[:, g:]], axis=0).astype(BF16)
    y_ref[...] = (_dot(wc_ref[...], pp) * scale).astype(BF16)


def _fourier_mix(a, *, B, N, L, with_ctx):
    S = N + L
    g = MIX_GROUP_W
    n2 = FFT_RADIX2
    n1 = N // n2
    cs = _channel_dft_matrix()
    a2 = a.reshape(B, S // n2, n2 * g)
    t = pl.pallas_call(
        functools.partial(_fft_a_kernel, nb=B, n1=n1),
        grid=(n2 // FFT_A_COLS,),
        in_specs=[
            pl.BlockSpec((B, n1, FFT_A_COLS * g), lambda j: (0, 0, j)),
            pl.BlockSpec(cs.shape, lambda j: (0, 0)),
            pl.BlockSpec((FFT_A_COLS, 2 * n1, 2 * n1), lambda j: (j, 0, 0)),
        ],
        out_specs=pl.BlockSpec((B, n1, FFT_A_COLS * 2 * g), lambda j: (0, 0, j)),
        out_shape=jax.ShapeDtypeStruct((B, n1, n2 * 2 * g), BF16),
        name="fourier_seq_stage_a",
    )(a2, cs, _seq_dft_stage_a(n1))
    t = t.reshape(B, n1 * n2, 2 * g)
    rows = S // n1
    y = pl.pallas_call(
        functools.partial(_fft_b_kernel, nb=B, scale=float((N * FOURIER_GROUP_W) ** -0.5)),
        grid=(n1 // FFT_B_COLS,),
        in_specs=[
            pl.BlockSpec((B, FFT_B_COLS * n2, 2 * g), lambda k: (0, k, 0)),
            pl.BlockSpec((n2, 2 * n2), lambda k: (0, 0)),
        ],
        out_specs=pl.BlockSpec((B, rows, FFT_B_COLS * g), lambda k: (0, 0, k)),
        out_shape=jax.ShapeDtypeStruct((B, rows, n1 * g), BF16),
        name="fourier_seq_stage_b",
    )(t, _seq_dft_stage_b())
    y = y.reshape(B, S, g)
    if not with_ctx:
        return y
    ctx_tile = N // TILE
    return pl.pallas_call(
        functools.partial(_fft_ctx_kernel, scale=float((L * FOURIER_GROUP_W) ** -0.5)),
        grid=(B,),
        in_specs=[
            pl.BlockSpec((None, TILE, g), lambda b: (b, ctx_tile, 0)),
            pl.BlockSpec(cs.shape, lambda b: (0, 0)),
            pl.BlockSpec((L, 2 * L), lambda b: (0, 0)),
            pl.BlockSpec(memory_space=pl.ANY),
        ],
        out_specs=pl.BlockSpec((None, TILE, g), lambda b: (b, ctx_tile, 0)),
        out_shape=jax.ShapeDtypeStruct((B, S, g), BF16),
        input_output_aliases={3: 0},
        name="fourier_ctx",
    )(a, cs, _dense_dft(L), y)


POOL_HALO = 16


def _pool_bands():
    t = np.arange(TILE)[:, None]
    s = np.arange(TILE + 2 * POOL_HALO)[None, :] - POOL_HALO
    bands = [((s - t >= -(w // 2)) & (s - t < w // 2)).astype(np.float32) for w in POOL_WINDOWS]
    return jnp.asarray(np.stack(bands), dtype=BF16)


def _pool_kernel(prev_ref, x_ref, next_ref, band_ref, pw_ref, ps_ref, o_ref, *, nb, n_lat, s_tot):
    i = pl.program_id(0)
    t0 = i * TILE
    is_ctx = t0 >= n_lat
    r0 = jnp.where(is_ctx, n_lat, 0)
    r1 = jnp.where(is_ctx, s_tot, n_lat)
    kw = TILE + 2 * POOL_HALO
    s_pos = t0 - POOL_HALO + lax.broadcasted_iota(I32, (kw, 1), 0)
    inside = jnp.where(s_pos >= r0, jnp.where(s_pos < r1, 1.0, 0.0), 0.0).astype(BF16)
    xc = jnp.concatenate([jnp.concatenate([prev_ref[b], x_ref[b], next_ref[b]], axis=0) * inside
                          for b in range(nb)], axis=-1)
    g = MIX_GROUP_W
    t_col = t0 + lax.broadcasted_iota(I32, (TILE, 1), 0)
    group = (lax.broadcasted_iota(I32, (TILE, nb * g), 1) // FOURIER_GROUP_W) % len(POOL_WINDOWS)
    acc = jnp.zeros((TILE, nb * g), F32)
    for gi, w in enumerate(POOL_WINDOWS):
        lo = jnp.maximum(t_col - w // 2, r0)
        hi = jnp.minimum(t_col + w // 2, r1)
        mean = _dot(band_ref[gi], xc) / (hi - lo).astype(F32)
        acc = jnp.where(group == gi, mean, acc)
    for b in range(nb):
        pooled = acc[:, g * b:g * (b + 1)] - x_ref[b].astype(F32)
        o_ref[b] = (_dot(pooled.astype(BF16), pw_ref[...]) * ps_ref[...]).astype(BF16)


def _pool_mix(bp, pw_bd, pscale, *, B, N, L):
    S = N + L
    nt = S // TILE
    g = MIX_GROUP_W
    hb = TILE // POOL_HALO
    last = S // POOL_HALO - 1
    return pl.pallas_call(
        functools.partial(_pool_kernel, nb=B, n_lat=N, s_tot=S),
        grid=(nt,),
        in_specs=[
            pl.BlockSpec((B, POOL_HALO, g), lambda i: (0, jnp.maximum(i * hb - 1, 0), 0)),
            pl.BlockSpec((B, TILE, g), lambda i: (0, i, 0)),
            pl.BlockSpec((B, POOL_HALO, g), lambda i: (0, jnp.minimum((i + 1) * hb, last), 0)),
            pl.BlockSpec((len(POOL_WINDOWS), TILE, TILE + 2 * POOL_HALO), lambda i: (0, 0, 0)),
            pl.BlockSpec((g, g), lambda i: (0, 0)),
            pl.BlockSpec((1, g), lambda i: (0, 0)),
        ],
        out_specs=pl.BlockSpec((B, TILE, g), lambda i: (0, i, 0)),
        out_shape=jax.ShapeDtypeStruct((B, S, g), BF16),
        name="pool_mix",
    )(bp, bp, bp, _pool_bands(), pw_bd, pscale)


def _attn_window_bias(n_ctx):
    r = (np.arange(2 * TILE) % TILE)[:, None]
    c = np.arange(TILE + 2 * WINDOW)[None, :] - WINDOW
    band = np.where(np.abs(r - c) <= WINDOW, 0.0, NEG_INF)
    return jnp.asarray(np.concatenate([band, np.zeros((2 * TILE, n_ctx))], axis=1), dtype=F32)


def _attn_kernel(sink_ref, bias_ref, q_ref, kp_ref, kc_ref, kn_ref, kx_ref, vp_ref, vc_ref, vn_ref, vx_ref, o_ref,
                 *, nb, n_lat):
    i = pl.program_id(0)
    t0 = i * TILE
    nband = TILE + 2 * WINDOW
    nkeys = nband + kx_ref.shape[1]
    col = lax.broadcasted_iota(I32, (1, nkeys), 1)
    kpos = t0 - WINDOW + col
    valid = (col >= nband) | ((kpos >= 0) & (kpos < n_lat) & (t0 < n_lat))
    bias = bias_ref[...]
    lane = lax.broadcasted_iota(I32, (TILE, 2 * HEAD_DIM), 1)
    low = lane < HEAD_DIM
    keep_low = jnp.where(low, 1.0, 0.0).astype(BF16)
    keep_high = jnp.where(low, 0.0, 1.0).astype(BF16)
    row1 = lax.broadcasted_iota(I32, (2 * TILE, 1), 0)
    for b in range(nb):
        outs = []
        for h in range(2):
            sl = slice(2 * HEAD_DIM * h, 2 * HEAD_DIM * (h + 1))
            qh = q_ref[b, :, sl]
            qs = jnp.concatenate([qh * keep_low, qh * keep_high], axis=0)
            kh = jnp.concatenate([kp_ref[b, :, sl], kc_ref[b, :, sl], kn_ref[b, :, sl], kx_ref[b, :, sl]], axis=0)
            vh = jnp.concatenate([vp_ref[b, :, sl], vc_ref[b, :, sl], vn_ref[b, :, sl], vx_ref[b, :, sl]], axis=0)
            s = jnp.where(valid, _dot_nt(qs, kh) + bias, NEG_INF)
            sink = jnp.where(row1 < TILE, sink_ref[2 * h], sink_ref[2 * h + 1])
            m = jnp.maximum(jnp.max(s, axis=-1, keepdims=True), sink)
            p = jnp.exp(s - m)
            den = jnp.sum(p, axis=-1, keepdims=True) + jnp.exp(sink - m)
            o = _dot(p.astype(BF16), vh) / den
            outs.append(jnp.where(low, o[:TILE], o[TILE:]))
        o_ref[b] = jnp.concatenate(outs, axis=-1).astype(BF16)


def _attention(sinks, q, k2, v2, *, B, N, L):
    S = N + L
    nt = S // TILE
    g = MIX_GROUP_W
    hb = TILE // WINDOW
    last = N // WINDOW - 1
    ctx_tile = N // TILE
    prev = pl.BlockSpec((B, WINDOW, g), lambda i: (0, jnp.clip(i * hb - 1, 0, last), 0))
    cur = pl.BlockSpec((B, TILE, g), lambda i: (0, i, 0))
    nxt = pl.BlockSpec((B, WINDOW, g), lambda i: (0, jnp.minimum((i + 1) * hb, last), 0))
    ctx = pl.BlockSpec((B, L, g), lambda i: (0, ctx_tile, 0))
    return pl.pallas_call(
        functools.partial(_attn_kernel, nb=B, n_lat=N),
        grid=(nt,),
        in_specs=[pl.BlockSpec(memory_space=pltpu.SMEM),
                  pl.BlockSpec((2 * TILE, TILE + 2 * WINDOW + L), lambda i: (0, 0)),
                  cur, prev, cur, nxt, ctx, prev, cur, nxt, ctx],
        out_specs=pl.BlockSpec((B, TILE, g), lambda i: (0, i, 0)),
        out_shape=jax.ShapeDtypeStruct((B, S, g), BF16),
        compiler_params=_cparams(vmem=VMEM_LIMIT),
        name="banded_attention",
    )(sinks, _attn_window_bias(L), q, k2, k2, k2, k2, v2, v2, v2, v2)


def _ret_kernel(*refs, reverse, nb):
    if reverse:
        dec_ref, q_ref, k_ref, v_ref, of_ref, g_ref, o_ref, r_scr = refs
    else:
        dec_ref, q_ref, k_ref, v_ref, o_ref, r_scr = refs
    c = RET_CHUNK
    w = MIX_GROUP_W
    dh = w // RET_HEADS

    @pl.when(pl.program_id(0) == 0)
    def _():
        r_scr[...] = jnp.zeros(r_scr.shape, F32)

    lg = -jnp.exp(dec_ref[...])
    idx = lax.broadcasted_iota(I32, (c, 1), 0).astype(F32)
    jdx = lax.broadcasted_iota(I32, (1, c), 1).astype(F32)
    if reverse:
        idx, jdx = (c - 1.0) - idx, (c - 1.0) - jdx
    diff = idx - jdx
    dmask_all = jnp.concatenate(
        [jnp.where(diff >= 0, jnp.exp(jnp.maximum(diff, 0.0) * lg[:, dh * h:dh * h + 1]), 0.0)
         for h in range(RET_HEADS)], axis=1)
    xi = jnp.exp((idx + 1.0) * lg)
    zeta = jnp.exp((c - 1.0 - idx) * lg)
    chunk_decay = jnp.exp(float(c) * lg)
    lane_head = lax.broadcasted_iota(I32, (1, w), 1) // dh
    keep_head = [jnp.where(lane_head == h, 1.0, 0.0).astype(BF16) for h in range(RET_HEADS)]
    same_head = (lax.broadcasted_iota(I32, (w, w), 0) // dh) == (lax.broadcasted_iota(I32, (w, w), 1) // dh)

    for b in range(nb):
        q = q_ref[b]
        k = k_ref[b] * jnp.asarray(dh ** -0.5, BF16)
        v = v_ref[b]
        state = r_scr[b]
        k_heads = jnp.concatenate([k * keep_head[h] for h in range(RET_HEADS)], axis=0)
        v_heads = jnp.concatenate([v * keep_head[h] for h in range(RET_HEADS)], axis=0)
        s = _dot_nt(q, k_heads) * dmask_all
        o = _dot(q, state.astype(BF16)) * xi + _dot(s.astype(BF16), v_heads)
        kv = _dot_tn(k, (v.astype(F32) * zeta).astype(BF16))
        r_scr[b] = state * chunk_decay + jnp.where(same_head, kv, 0.0)
        if not reverse:
            o_ref[b] = o
        else:
            tot = o + of_ref[b]
            mu = jnp.zeros_like(tot)
            for h in range(RET_HEADS):
                mh = lane_head == h
                mu = jnp.where(mh, jnp.sum(jnp.where(mh, tot, 0.0), axis=-1, keepdims=True) / dh, mu)
            cen = tot - mu
            var = jnp.zeros_like(tot)
            for h in range(RET_HEADS):
                mh = lane_head == h
                var = jnp.where(mh, jnp.sum(jnp.where(mh, cen * cen, 0.0), axis=-1, keepdims=True) / dh, var)
            gate = g_ref[b].astype(F32)
            o_ref[b] = (gate * jax.nn.sigmoid(gate) * (cen * lax.rsqrt(var + LN_EPS))).astype(BF16)


def _retention(dec, rq, rk, rv, rg, *, B, N, L):
    S = N + L
    c = RET_CHUNK
    w = MIX_GROUP_W
    nc_lat, nc_ctx = N // c, L // c
    nc = nc_lat + nc_ctx

    def fwd_blk(j):
        return jnp.where(j < nc_ctx, nc_lat + j, j - nc_ctx)

    def bwd_blk(j):
        return nc - 1 - j

    def call(blk, reverse, extra):
        spec = pl.BlockSpec((B, c, w), lambda j: (0, blk(j), 0))
        dspec = pl.BlockSpec((1, w), lambda j: (0, 0))
        return pl.pallas_call(
            functools.partial(_ret_kernel, reverse=reverse, nb=B),
            grid=(nc,),
            in_specs=[dspec] + [spec] * (3 + len(extra)),
            out_specs=spec,
            out_shape=jax.ShapeDtypeStruct((B, S, w), BF16 if reverse else F32),
            scratch_shapes=[pltpu.VMEM((B, w, w), F32)],
            compiler_params=_cparams(sem=("arbitrary",), vmem=VMEM_LIMIT),
            name="retention_bwd" if reverse else "retention_fwd",
        )(dec[1:2] if reverse else dec[0:1], rq, rk, rv, *extra)

    o_f = call(fwd_blk, False, ())
    return call(bwd_blk, True, (o_f, rg))


def _merge_kernel(*refs, nb, split_input, n_lat_tiles, alpha):
    for b in range(nb):
        x, rest = _tile_inputs(refs, b, split_input, n_lat_tiles)
        (yf_ref, yp_ref, ya_ref, yr_ref, g1_ref, sc_ref, sh_ref, lg_ref, lb_ref, wo_ref, rw_ref, rb_ref,
         x1_ref, h2_ref) = rest[:14]
        row = _mod_row(b, n_lat_tiles, nb)
        ycat = jnp.concatenate([yf_ref[b], yp_ref[b], ya_ref[b], yr_ref[b]], axis=-1)
        y = _dot(ycat, wo_ref[...])
        x1 = _normalize(alpha * x + g1_ref[row] * y) * lg_ref[...] + lb_ref[...]
        x1_ref[b] = x1
        h2 = _normalize(x1) * (1.0 + sc_ref[row]) + sh_ref[row]
        _to_token_tiles(h2_ref.at[b], h2)
        logits = _dot_nt(rw_ref[...], h2.astype(BF16)) + rb_ref[...]
        _route(logits, rest[14 + b], rest[14 + nb + b])


def _route(logits, idx_ref, gate_ref):
    eid = lax.broadcasted_iota(I32, logits.shape, 0)
    orow = lax.broadcasted_iota(I32, (8, logits.shape[1]), 0)
    idx_out = jnp.zeros((8, logits.shape[1]), I32)
    val_out = jnp.zeros((8, logits.shape[1]), F32)
    top = None
    den = None
    for kk in range(TOP_K):
        m = jnp.max(logits, axis=0, keepdims=True)
        sel = jnp.min(jnp.where(logits == m, eid, N_EXPERTS), axis=0, keepdims=True)
        logits = jnp.where(eid == sel, -jnp.inf, logits)
        if kk == 0:
            top = m
        e = jnp.exp(m - top)
        den = e if kk == 0 else den + e
        idx_out = jnp.where(orow == kk, sel, idx_out)
        val_out = jnp.where(orow == kk, e, val_out)
    idx_ref[...] = idx_out
    gate_ref[...] = val_out / den


def _merge(xs, ys, g1, sc2, sh2, ln_g, ln_b, w_out, rw_t, rb, *, B, S, n_lat_tiles, alpha):
    split_input = len(xs) == 2
    nt = S // TILE
    g = MIX_GROUP_W
    r = ROWS_PER_TOKEN
    yspec = pl.BlockSpec((B, TILE, g), lambda i: (0, i, 0))
    vec = pl.BlockSpec((1, D_MODEL), lambda i: (0, 0))
    route_spec = pl.BlockSpec((8, TILE), lambda i: (0, i))
    outs = pl.pallas_call(
        functools.partial(_merge_kernel, nb=B, split_input=split_input, n_lat_tiles=n_lat_tiles, alpha=alpha),
        grid=(nt,),
        in_specs=_x_specs(B, split_input, n_lat_tiles) + [yspec] * 4 + [
            _mod_spec(), _mod_spec(), _mod_spec(), vec, vec,
            pl.BlockSpec((D_MODEL, D_MODEL), lambda i: (0, 0)),
            pl.BlockSpec((N_EXPERTS, D_MODEL), lambda i: (0, 0)),
            pl.BlockSpec((N_EXPERTS, 1), lambda i: (0, 0)),
        ],
        out_specs=[
            pl.BlockSpec((B, TILE, D_MODEL), lambda i: (0, i, 0)),
            pl.BlockSpec((B, TILE * r, 128), lambda i: (0, i, 0)),
        ] + [route_spec] * (2 * B),
        out_shape=[
            jax.ShapeDtypeStruct((B, S, D_MODEL), F32),
            jax.ShapeDtypeStruct((B, S * r, 128), F32),
        ] + [jax.ShapeDtypeStruct((8, S), I32)] * B + [jax.ShapeDtypeStruct((8, S), F32)] * B,
        compiler_params=_cparams(vmem=VMEM_LIMIT),
        name="out_proj_norm_router",
    )(*xs, *ys, g1, sc2, sh2, ln_g, ln_b, w_out, rw_t, rb)
    x1, h2t = outs[:2]
    idx = jnp.concatenate(outs[2:2 + B], axis=1)
    gates = jnp.concatenate(outs[2 + B:], axis=1)
    return x1, h2t.reshape(B * S * r, 128), idx, gates


def _rank_kernel(idx_ref, lpos_ref, ntab_ref, cbtab_ref, cnt_ref, carry_scr):
    tt = idx_ref.shape[1]

    @pl.when(pl.program_id(0) == 0)
    def _():
        carry_scr[...] = jnp.zeros(carry_scr.shape, F32)

    eid = lax.broadcasted_iota(I32, (N_EXPERTS, tt), 0)
    before = jnp.where(lax.broadcasted_iota(I32, (tt, tt), 0) < lax.broadcasted_iota(I32, (tt, tt), 1),
                       1.0, 0.0).astype(BF16)
    hits, cums, pres = [], [], []
    n_tile = jnp.zeros((N_EXPERTS, 1), F32)
    for kk in range(TOP_K):
        hit = eid == idx_ref[kk:kk + 1, :]
        hits.append(hit)
        cums.append(_dot(jnp.where(hit, 1.0, 0.0).astype(BF16), before))
        pres.append(n_tile)
        n_tile = n_tile + jnp.sum(jnp.where(hit, 1.0, 0.0), axis=1, keepdims=True)
    n16 = ((n_tile.astype(I32) + (RUN_ALIGN - 1)) >> RUN_SHIFT) << RUN_SHIFT
    lower = jnp.where(lax.broadcasted_iota(I32, (N_EXPERTS, N_EXPERTS), 1)
                      < lax.broadcasted_iota(I32, (N_EXPERTS, N_EXPERTS), 0), 1.0, 0.0).astype(BF16)
    hi = jnp.broadcast_to((n16 >> 8).astype(F32), (N_EXPERTS, 128)).astype(BF16)
    lo = jnp.broadcast_to((n16 & 255).astype(F32), (N_EXPERTS, 128)).astype(BF16)
    lstart = (256.0 * _dot(lower, hi) + _dot(lower, lo))[:, 0:1]
    orow = lax.broadcasted_iota(I32, (8, tt), 0)
    out = jnp.zeros((8, tt), I32)
    for kk in range(TOP_K):
        slot = jnp.sum(jnp.where(hits[kk], lstart + pres[kk] + cums[kk], 0.0), axis=0, keepdims=True)
        out = jnp.where(orow == kk, slot.astype(I32) * ROWS_PER_TOKEN, out)
    lpos_ref[...] = out
    carry = carry_scr[...]
    ntab_ref[...] = jnp.broadcast_to(n_tile, ntab_ref.shape).astype(I32)
    cbtab_ref[...] = carry.astype(I32)
    carry = carry + n_tile
    carry_scr[...] = carry
    cnt_ref[...] = carry


def _rank(idx, *, T):
    tt = MOE_TOK_TILE
    nt = T // tt
    tab = pl.BlockSpec((None, N_EXPERTS, 128), lambda i: (i, 0, 0))
    return pl.pallas_call(
        _rank_kernel,
        grid=(nt,),
        in_specs=[pl.BlockSpec((8, tt), lambda i: (0, i))],
        out_specs=[pl.BlockSpec((8, tt), lambda i: (0, i)), tab, tab,
                   pl.BlockSpec((N_EXPERTS, 128), lambda i: (0, 0))],
        out_shape=[jax.ShapeDtypeStruct((8, T), I32), jax.ShapeDtypeStruct((nt, N_EXPERTS, 128), I32),
                   jax.ShapeDtypeStruct((nt, N_EXPERTS, 128), I32), jax.ShapeDtypeStruct((N_EXPERTS, 128), F32)],
        scratch_shapes=[pltpu.VMEM((N_EXPERTS, 128), F32)],
        compiler_params=_cparams(sem=("arbitrary",)),
        name="moe_rank",
    )(idx)


def _run_chunk(src_ref, src_tok, dst_ref, dst_tok, sem):
    return pltpu.make_async_copy(src_ref.at[_token_rows(src_tok, RUN_ALIGN), :],
                                 dst_ref.at[_token_rows(dst_tok, RUN_ALIGN), :], sem)


def _wait_chunks(src_ref, dst_ref, sem, count):
    def body(_, carry):
        _run_chunk(src_ref, 0, dst_ref, 0, sem).wait()
        return carry
    lax.fori_loop(0, count, body, 0)


def _copy_runs(step, pstart_ref, ntab_ref, cbtab_ref, chunk):
    def per_expert(e, carry):
        src0, total = carry
        nch = (ntab_ref[step * N_EXPERTS + e] + (RUN_ALIGN - 1)) >> RUN_SHIFT
        dst0 = pstart_ref[e] + cbtab_ref[step * N_EXPERTS + e]

        def one(c, cc):
            chunk(src0 + c * RUN_ALIGN, dst0 + c * RUN_ALIGN).start()
            return cc
        lax.fori_loop(0, nch, one, 0)
        return src0 + nch * RUN_ALIGN, total + nch
    return lax.fori_loop(0, N_EXPERTS, per_expert, (0, 0))[1]


def _dispatch_kernel(pstart_ref, cnt_ref, pad_ref, ntab_ref, cbtab_ref, lpos_ref, h_ref, xs_ref,
                     stage, zero_scr, sems, sent):
    step = pl.program_id(0)

    @pl.when(step == 0)
    def _():
        sem = sems.at[0]
        zero_scr[...] = jnp.zeros(zero_scr.shape, F32)

        def per_expert(e, total):
            first = pstart_ref[e] + ((cnt_ref[e] >> RUN_SHIFT) << RUN_SHIFT)
            n_fill = (pstart_ref[e] + pad_ref[e] - first) >> RUN_SHIFT

            def fill(c, carry):
                _run_chunk(zero_scr, 0, xs_ref, first + c * RUN_ALIGN, sem).start()
                return carry
            lax.fori_loop(0, n_fill, fill, 0)
            return total + n_fill
        n_total = lax.fori_loop(0, N_EXPERTS, per_expert, 0)
        used = pstart_ref[N_EXPERTS - 1] + pad_ref[N_EXPERTS - 1]
        n_tail = (xs_ref.shape[0] // ROWS_PER_TOKEN - used) >> RUN_SHIFT

        def fill_tail(c, carry):
            _run_chunk(zero_scr, 0, xs_ref, used + c * RUN_ALIGN, sem).start()
            return carry
        lax.fori_loop(0, n_tail, fill_tail, 0)
        _wait_chunks(zero_scr, xs_ref, sem, n_total + n_tail)
        sent[0] = 0
        sent[1] = 0

    slot = step % 2
    buf = stage.at[slot]
    sem = sems.at[slot]

    def zero_last_group(e, src0):
        n_up = ((ntab_ref[step * N_EXPERTS + e] + (RUN_ALIGN - 1)) >> RUN_SHIFT) << RUN_SHIFT
        buf[_token_rows(jnp.maximum(src0 + n_up - RUN_ALIGN, 0), RUN_ALIGN), :] = jnp.zeros((CHUNK_ROWS, 128), F32)
        return src0 + n_up
    lax.fori_loop(0, N_EXPERTS, zero_last_group, 0)
    tt = MOE_TOK_TILE
    for kk in range(TOP_K):
        def place(t, carry, kk=kk):
            row = pl.multiple_of(lpos_ref[kk * tt + t], ROWS_PER_TOKEN)
            buf[pl.ds(row, ROWS_PER_TOKEN), :] = h_ref[_token_rows(t), :]
            return carry
        lax.fori_loop(0, tt, place, 0, unroll=16)
    _wait_chunks(stage.at[1 - slot], xs_ref, sems.at[1 - slot], sent[1 - slot])
    sent[1 - slot] = 0
    sent[slot] = _copy_runs(step, pstart_ref, ntab_ref, cbtab_ref,
                            lambda s, d: _run_chunk(buf, s, xs_ref, d, sem))

    @pl.when(step == pl.num_programs(0) - 1)
    def _():
        _wait_chunks(buf, xs_ref, sem, sent[slot])


def _smem_tile_spec(tt):
    return pl.BlockSpec((8 * tt,), lambda i, *_: (i,), memory_space=pltpu.SMEM)


def _tile_major(table):
    tt = MOE_TOK_TILE
    return table.reshape(table.shape[0], -1, tt).transpose(1, 0, 2).reshape(-1)


def _dispatch(pstart, cnt, padded, ntab, cbtab, lpos, h2t, *, T, P):
    tt = MOE_TOK_TILE
    r = ROWS_PER_TOKEN
    return pl.pallas_call(
        _dispatch_kernel,
        grid_spec=pltpu.PrefetchScalarGridSpec(
            num_scalar_prefetch=5,
            grid=(T // tt,),
            in_specs=[_smem_tile_spec(tt), pl.BlockSpec((tt * r, 128), lambda i, *_: (i, 0))],
            out_specs=pl.BlockSpec(memory_space=pl.ANY),
            scratch_shapes=[pltpu.VMEM((2, STAGE_TOKENS * r, 128), F32), pltpu.VMEM((CHUNK_ROWS, 128), F32),
                            pltpu.SemaphoreType.DMA((2,)), pltpu.SMEM((2,), I32)],
        ),
        out_shape=jax.ShapeDtypeStruct((P * r, 128), F32),
        compiler_params=_cparams(sem=("arbitrary",), vmem=EXPERT_VMEM_LIMIT),
        name="moe_dispatch",
    )(pstart, cnt, padded, ntab, cbtab, _tile_major(lpos), h2t)


def _collect_kernel(pstart_ref, ntab_ref, cbtab_ref, lpos_ref, gate_ref, ys_ref, f_ref, stage, sems, sent):
    step = pl.program_id(0)
    slot = step % 2

    def fetch(tile, b):
        sent[b] = _copy_runs(tile, pstart_ref, ntab_ref, cbtab_ref,
                             lambda s, d: _run_chunk(ys_ref, d, stage.at[b], s, sems.at[b]))

    @pl.when(step == 0)
    def _():
        fetch(0, 0)

    @pl.when(step + 1 < pl.num_programs(0))
    def _():
        fetch(step + 1, 1 - slot)

    buf = stage.at[slot]
    _wait_chunks(ys_ref, buf, sems.at[slot], sent[slot])

    tt = MOE_TOK_TILE

    def combine(t, carry):
        acc = None
        for kk in range(TOP_K):
            row = pl.multiple_of(lpos_ref[kk * tt + t], ROWS_PER_TOKEN)
            term = gate_ref[kk * tt + t] * buf[pl.ds(row, ROWS_PER_TOKEN), :]
            acc = term if acc is None else acc + term
        f_ref[_token_rows(t), :] = acc
        return carry
    lax.fori_loop(0, tt, combine, 0, unroll=8)


def _collect(pstart, ntab, cbtab, lpos, gates, ys, *, T):
    tt = MOE_TOK_TILE
    r = ROWS_PER_TOKEN
    return pl.pallas_call(
        _collect_kernel,
        grid_spec=pltpu.PrefetchScalarGridSpec(
            num_scalar_prefetch=3,
            grid=(T // tt,),
            in_specs=[_smem_tile_spec(tt), _smem_tile_spec(tt), pl.BlockSpec(memory_space=pl.ANY)],
            out_specs=pl.BlockSpec((tt * r, 128), lambda i, *_: (i, 0)),
            scratch_shapes=[pltpu.VMEM((2, STAGE_TOKENS * r, 128), F32), pltpu.SemaphoreType.DMA((2,)),
                            pltpu.SMEM((2,), I32)],
        ),
        out_shape=jax.ShapeDtypeStruct((T * r, 128), F32),
        compiler_params=_cparams(sem=("arbitrary",), vmem=EXPERT_VMEM_LIMIT),
        name="moe_collect",
    )(pstart, ntab, cbtab, _tile_major(lpos), _tile_major(gates), ys)


def _deinterleave_matrix():
    m = np.zeros((256, 256), np.float32)
    j = np.arange(128)
    m[2 * j, j] = 1.0
    m[2 * j + 1, 128 + j] = 1.0
    return jnp.asarray(m, dtype=BF16)


def _expert_kernel(be_ref, nu_ref, x_ref, w1_ref, b1_ref, w2_ref, b2_ref, perm_ref, y_ref, w1_scr, w2_scr):
    i = pl.program_id(0)

    @pl.when(i >= nu_ref[0])
    def _():
        y_ref[...] = jnp.zeros(y_ref.shape, F32)

    @pl.when(i < nu_ref[0])
    def _():
        @pl.when((i == 0) | (be_ref[i] != be_ref[jnp.maximum(i - 1, 0)]))
        def _():
            half = w1_scr.shape[1] // 2
            for m in range(w1_ref.shape[1] // 256):
                r = _dot(w1_ref[:, 256 * m:256 * (m + 1)].astype(BF16), perm_ref[...])
                w1_scr[:, 128 * m:128 * (m + 1)] = r[:, :128].astype(BF16)
                w1_scr[:, half + 128 * m:half + 128 * (m + 1)] = r[:, 128:].astype(BF16)
            w2_scr[...] = w2_ref[...].astype(BF16)

        x = _from_token_tiles(x_ref, MOE_BLK).astype(BF16)
        u = _dot(x, w1_scr[...]) + b1_ref[...]
        half = u.shape[1] // 2
        glu = jnp.minimum(u[:, :half], SWIGLU_LIMIT)
        lin = jnp.clip(u[:, half:], -SWIGLU_LIMIT, SWIGLU_LIMIT)
        act = glu * jax.nn.sigmoid(SWIGLU_ALPHA * glu) * (lin + 1.0)
        _to_token_tiles(y_ref, _dot(act.astype(BF16), w2_scr[...]) + b2_ref[...])


def _experts(blk_e, n_used, xs, w1, b1, w2, b2, *, P, layer):
    nblk = P // MOE_BLK
    r = ROWS_PER_TOKEN
    d, two_f = w1.shape[2:]
    f = w2.shape[2]
    rows = lambda i, be, nu: (jnp.minimum(i, nu[0] - 1), 0)
    return pl.pallas_call(
        _expert_kernel,
        grid_spec=pltpu.PrefetchScalarGridSpec(
            num_scalar_prefetch=2,
            grid=(nblk,),
            in_specs=[
                pl.BlockSpec((MOE_BLK * r, 128), rows),
                pl.BlockSpec((None, None, d, two_f), lambda i, be, nu: (layer, be[i], 0, 0)),
                pl.BlockSpec((None, 1, two_f), lambda i, be, nu: (be[i], 0, 0)),
                pl.BlockSpec((None, None, f, d), lambda i, be, nu: (layer, be[i], 0, 0)),
                pl.BlockSpec((None, 1, d), lambda i, be, nu: (be[i], 0, 0)),
                pl.BlockSpec((256, 256), lambda i, be, nu: (0, 0)),
            ],
            out_specs=pl.BlockSpec((MOE_BLK * r, 128), lambda i, be, nu: (i, 0)),
            scratch_shapes=[pltpu.VMEM((d, two_f), BF16), pltpu.VMEM((f, d), BF16)],
        ),
        out_shape=jax.ShapeDtypeStruct((P * r, 128), F32),
        compiler_params=_cparams(sem=("arbitrary",), vmem=EXPERT_VMEM_LIMIT),
        name="moe_experts",
    )(blk_e, n_used, xs, w1, b1, w2, b2, _deinterleave_matrix())


def _moe(h2t, idx, gates, w1, b1, w2, b2, *, T, layer):
    lpos, ntab, cbtab, cnt = _rank(idx, T=T)
    counts = cnt[:, 0].astype(I32)
    padded = (counts + RUN_ALIGN + MOE_BLK - 1) // MOE_BLK * MOE_BLK
    pad_end = jnp.cumsum(padded)
    pstart = (pad_end - padded).astype(I32)
    nblk = TOP_K * T // MOE_BLK + 2 * N_EXPERTS
    P = nblk * MOE_BLK
    starts = jnp.arange(nblk, dtype=I32) * MOE_BLK
    blk_e = jnp.minimum(jnp.sum((pad_end[None, :] <= starts[:, None]).astype(I32), axis=1), N_EXPERTS - 1)
    n_used = (pad_end[-1:] // MOE_BLK).astype(I32)
    ntab = ntab[:, :, 0].reshape(-1)
    cbtab = cbtab[:, :, 0].reshape(-1)
    xs = _dispatch(pstart, counts, padded.astype(I32), ntab, cbtab, lpos, h2t, T=T, P=P)
    ys = _experts(blk_e.astype(I32), n_used, xs, w1, b1, w2, b2, P=P, layer=layer)
    return _collect(pstart, ntab, cbtab, lpos, gates, ys, T=T)


def _residual_kernel(f_ref, x_ref, g2_ref, lg_ref, lb_ref, o_ref, *, nb, n_lat_tiles, alpha):
    for b in range(nb):
        f = _from_token_tiles(f_ref.at[b], TILE)
        z = alpha * x_ref[b] + g2_ref[_mod_row(b, n_lat_tiles, nb)] * f
        o_ref[b] = _normalize(z) * lg_ref[...] + lb_ref[...]


def _residual(ft, x1, g2, ln_g, ln_b, *, B, S, n_out_tiles, n_lat_tiles, alpha):
    r = ROWS_PER_TOKEN
    vec = pl.BlockSpec((1, D_MODEL), lambda i: (0, 0))
    return pl.pallas_call(
        functools.partial(_residual_kernel, nb=B, n_lat_tiles=n_lat_tiles, alpha=alpha),
        grid=(n_out_tiles,),
        in_specs=[
            pl.BlockSpec((B, TILE * r, 128), lambda i: (0, i, 0)),
            pl.BlockSpec((B, TILE, D_MODEL), lambda i: (0, i, 0)),
            _mod_spec(), vec, vec,
        ],
        out_specs=pl.BlockSpec((B, TILE, D_MODEL), lambda i: (0, i, 0)),
        out_shape=jax.ShapeDtypeStruct((B, n_out_tiles * TILE, D_MODEL), F32),
        compiler_params=_cparams(vmem=VMEM_LIMIT),
        name="ffn_residual_norm",
    )(ft.reshape(B, S * r, 128), x1, g2, ln_g, ln_b)


def _rope_tables(n_lat, n_ctx):
    half = HEAD_DIM // 2
    pos = np.arange(n_lat)
    inv = 1.0 / (ROPE_BASE ** (np.arange(0, half, 2, dtype=np.float64) / half))
    ang_row = (pos // GRID_W)[:, None].astype(np.float64) * inv[None, :]
    ang_col = (pos % GRID_W)[:, None].astype(np.float64) * inv[None, :]
    ang = np.concatenate([ang_row, ang_row, ang_col, ang_col], axis=1)
    sign = np.concatenate([-np.ones(16), np.ones(16)] * 2)[None, :]
    cos = np.concatenate([np.cos(ang), np.ones((n_ctx, HEAD_DIM))], axis=0)
    sin = np.concatenate([np.sin(ang) * sign, np.zeros((n_ctx, HEAD_DIM))], axis=0)
    return (jnp.asarray(np.tile(cos, (1, 2)), dtype=F32), jnp.asarray(np.tile(sin, (1, 2)), dtype=F32))


def _extend_w_in(w):
    g = MIX_GROUP_W
    hd = HEAD_DIM
    a, bp, q = w[:, 0:g], w[:, g:2 * g], w[:, 2 * g:3 * g]
    k, v = w[:, 3 * g:3 * g + 2 * hd], w[:, 3 * g + 2 * hd:4 * g]
    rest = w[:, 4 * g:]
    dup = lambda t: jnp.concatenate([t[:, :hd], t[:, :hd], t[:, hd:], t[:, hd:]], axis=1)
    return jnp.concatenate([a, bp, q, dup(k), dup(v), rest], axis=1).astype(BF16)


def _block_diag(pw):
    n, c, _ = pw.shape
    out = jnp.zeros((n * c, n * c), pw.dtype)
    for gi in range(n):
        out = lax.dynamic_update_slice(out, pw[gi], (gi * c, gi * c))
    return out


def kernel(x, c, ctx, c_ctx, w_mod, b_mod, w_in, pool_w, pool_scale, attn_sink, ret_decay, w_out, ln_g, ln_b,
           router_w, router_b, exp_w1, exp_b1, exp_w2, exp_b2):
    B, N, D = x.shape
    L = ctx.shape[1]
    depth = w_mod.shape[0]
    assert D == D_MODEL and L == TILE and N % TILE == 0 and N % (FFT_RADIX2 * 16) == 0 and B < 8
    S = N + L
    n_lat_tiles = N // TILE
    assert (B * S) % MOE_TOK_TILE == 0 and (B * N) % MOE_TOK_TILE == 0
    alpha = float((2 * depth) ** 0.25)

    cvec = jnp.zeros((8, D), F32).at[:B].set(c).at[B].set(c_ctx)
    mods = _modulation(cvec, w_mod, b_mod)
    cos, sin = _rope_tables(N, L)

    xs = (x, ctx)
    out = None
    for l in range(depth):
        last = l == depth - 1
        m = mods[l].reshape(8, 6, 1, D)
        sh1, sc1, g1, sh2, sc2, g2 = [m[:, j] for j in range(6)]
        proj = _projection(xs, sc1, sh1, _extend_w_in(w_in[l]), cos, sin, B=B, S=S, n_lat_tiles=n_lat_tiles)
        a, bp, q, k2, v2, rq, rk, rv, rg = proj
        y_f = _fourier_mix(a, B=B, N=N, L=L, with_ctx=not last)
        y_p = _pool_mix(bp, _block_diag(pool_w[l]).astype(BF16), pool_scale[l].reshape(1, -1), B=B, N=N, L=L)
        y_a = _attention(attn_sink[l], q, k2, v2, B=B, N=N, L=L)
        dec = jnp.repeat(ret_decay[l], MIX_GROUP_W // RET_HEADS, axis=1)
        y_r = _retention(dec, rq, rk, rv, rg, B=B, N=N, L=L)
        rows = N if last else S
        x1, h2t, idx, gates = _merge(
            xs, (y_f, y_p, y_a, y_r), g1, sc2, sh2, ln_g[l, 0:1], ln_b[l, 0:1], w_out[l].astype(BF16),
            router_w[l].T.astype(BF16), router_b[l].reshape(-1, 1), B=B, S=rows, n_lat_tiles=n_lat_tiles, alpha=alpha)
        b1 = jnp.concatenate([exp_b1[l][:, 0::2], exp_b1[l][:, 1::2]], axis=-1)[:, None, :]
        ft = _moe(h2t, idx, gates, exp_w1, b1, exp_w2, exp_b2[l][:, None, :], T=B * rows, layer=l)
        out = _residual(ft, x1, g2, ln_g[l, 1:2], ln_b[l, 1:2], B=B, S=rows,
                        n_out_tiles=rows // TILE, n_lat_tiles=n_lat_tiles, alpha=alpha)
        xs = (out,)
    return out
```

```python
import functools

import numpy as np
import jax
import jax.numpy as jnp
from jax import lax
from jax.experimental import pallas as pl
from jax.experimental.pallas import tpu as pltpu

F32 = jnp.float32
BF16 = jnp.bfloat16
I32 = jnp.int32

D_MODEL = 1024
GRID_W = 64
FOURIER_GROUP_W = 64
POOL_WINDOWS = (2, 4, 8, 16)
HEAD_DIM = 64
WINDOW = 128
ROPE_BASE = 10000.0
RET_HEADS = 4
RET_CHUNK = 128
N_EXPERTS = 32
TOP_K = 4
SWIGLU_LIMIT = 7.0
SWIGLU_ALPHA = 1.702
LN_EPS = 1e-6
NEG_INF = -1e30
MIX_GROUP_W = 256
FFT_RADIX2 = 64

TILE = 256
MOE_BLK = 512
MOE_TOK_TILE = 1024
ROWS_PER_TOKEN = D_MODEL // 128
RUN_ALIGN = 16
RUN_SHIFT = 4
CHUNK_ROWS = RUN_ALIGN * ROWS_PER_TOKEN
STAGE_TOKENS = TOP_K * MOE_TOK_TILE + N_EXPERTS * RUN_ALIGN
VMEM_LIMIT = 48 * 1024 * 1024
EXPERT_VMEM_LIMIT = 56 * 1024 * 1024


def _dot(a, b):
    return jnp.dot(a, b, preferred_element_type=F32)


def _dot_nt(a, b):
    return lax.dot_general(a, b, (((1,), (1,)), ((), ())), preferred_element_type=F32)


def _dot_tn(a, b):
    return lax.dot_general(a, b, (((0,), (0,)), ((), ())), preferred_element_type=F32)


def _normalize(x):
    mu = jnp.mean(x, axis=-1, keepdims=True)
    xc = x - mu
    var = jnp.mean(xc * xc, axis=-1, keepdims=True)
    return xc * lax.rsqrt(var + LN_EPS)


def _cparams(sem=None, vmem=None):
    kw = {}
    if sem is not None:
        kw["dimension_semantics"] = sem
    if vmem is not None:
        kw["vmem_limit_bytes"] = vmem
    return pltpu.CompilerParams(**kw)


def _to_token_tiles(ref, val):
    n = val.shape[0]
    for c in range(ROWS_PER_TOKEN):
        ref[pl.ds(c, n, stride=ROWS_PER_TOKEN), :] = val[:, 128 * c:128 * (c + 1)]


def _from_token_tiles(ref, n):
    return jnp.concatenate([ref[pl.ds(c, n, stride=ROWS_PER_TOKEN), :] for c in range(ROWS_PER_TOKEN)], axis=-1)


def _token_rows(tok, count=1):
    return pl.ds(pl.multiple_of(tok * ROWS_PER_TOKEN, ROWS_PER_TOKEN), count * ROWS_PER_TOKEN)


def _mod_kernel(c_ref, w_ref, b_ref, o_ref):
    c = c_ref[...]
    a = c * jax.nn.sigmoid(c)
    w = w_ref[...]
    a_hi = a.astype(BF16)
    a_lo = (a - a_hi.astype(F32)).astype(BF16)
    w_hi = w.astype(BF16)
    w_lo = (w - w_hi.astype(F32)).astype(BF16)
    o_ref[...] = _dot(a_hi, w_hi) + _dot(a_lo, w_hi) + _dot(a_hi, w_lo) + b_ref[...]


def _modulation(cvec, w_mod, b_mod):
    depth, d, six_d = w_mod.shape
    nb = 1536
    return pl.pallas_call(
        _mod_kernel,
        grid=(depth, six_d // nb),
        in_specs=[
            pl.BlockSpec((8, d), lambda l, j: (0, 0)),
            pl.BlockSpec((None, d, nb), lambda l, j: (l, 0, j)),
            pl.BlockSpec((None, 1, nb), lambda l, j: (l, 0, j)),
        ],
        out_specs=pl.BlockSpec((None, 8, nb), lambda l, j: (l, 0, j)),
        out_shape=jax.ShapeDtypeStruct((depth, 8, six_d), F32),
        compiler_params=_cparams(vmem=VMEM_LIMIT),
        name="modulation",
    )(cvec, w_mod, b_mod.reshape(depth, 1, six_d))


def _tile_inputs(refs, b, split_input, n_lat_tiles):
    if split_input:
        x = jnp.where(pl.program_id(0) == n_lat_tiles, refs[1][b], refs[0][b])
        return x, refs[2:]
    return refs[0][b], refs[1:]


def _mod_row(b, n_lat_tiles, ctx_row):
    return jnp.where(pl.program_id(0) == n_lat_tiles, ctx_row, b)


def _proj_kernel(*refs, nb, split_input, n_lat_tiles):
    g = MIX_GROUP_W
    for b in range(nb):
        x, rest = _tile_inputs(refs, b, split_input, n_lat_tiles)
        sc_ref, sh_ref, w_ref, cos_ref, sin_ref = rest[:5]
        a_ref, bp_ref, q_ref, k_ref, v_ref, rq_ref, rk_ref, rv_ref, rg_ref = rest[5:]
        row = _mod_row(b, n_lat_tiles, nb)
        h = _normalize(x) * (1.0 + sc_ref[row]) + sh_ref[row]
        p = _dot(h.astype(BF16), w_ref[...])
        a_ref[b] = p[:, 0:g].astype(BF16)
        bp_ref[b] = p[:, g:2 * g].astype(BF16)
        qk = p[:, 2 * g:4 * g]
        lane = lax.broadcasted_iota(I32, qk.shape, 1)
        first = (lane & 16) == 0
        partner = jnp.where(first, pltpu.roll(qk, 2 * g - 16, 1), pltpu.roll(qk, 16, 1))
        cos = jnp.concatenate([cos_ref[...]] * 4, axis=-1)
        sin = jnp.concatenate([sin_ref[...]] * 4, axis=-1)
        qk = qk * cos + partner * sin
        q_ref[b] = (qk[:, :g] * (HEAD_DIM ** -0.5)).astype(BF16)
        k_ref[b] = qk[:, g:].astype(BF16)
        v_ref[b] = p[:, 4 * g:5 * g].astype(BF16)
        rq_ref[b] = p[:, 5 * g:6 * g].astype(BF16)
        rk_ref[b] = p[:, 6 * g:7 * g].astype(BF16)
        rv_ref[b] = p[:, 7 * g:8 * g].astype(BF16)
        rg_ref[b] = p[:, 8 * g:9 * g].astype(BF16)


def _mod_spec():
    return pl.BlockSpec((8, 1, D_MODEL), lambda i: (0, 0, 0))


def _x_specs(B, split_input, n_lat_tiles):
    if split_input:
        return [
            pl.BlockSpec((B, TILE, D_MODEL), lambda i: (0, jnp.minimum(i, n_lat_tiles - 1), 0)),
            pl.BlockSpec((B, TILE, D_MODEL), lambda i: (0, 0, 0)),
        ]
    return [pl.BlockSpec((B, TILE, D_MODEL), lambda i: (0, i, 0))]


def _projection(xs, sc, sh, w_ext, cos, sin, *, B, S, n_lat_tiles):
    split_input = len(xs) == 2
    nt = S // TILE
    g = MIX_GROUP_W
    out_spec = pl.BlockSpec((B, TILE, g), lambda i: (0, i, 0))
    out_shape = jax.ShapeDtypeStruct((B, S, g), BF16)
    return pl.pallas_call(
        functools.partial(_proj_kernel, nb=B, split_input=split_input, n_lat_tiles=n_lat_tiles),
        grid=(nt,),
        in_specs=_x_specs(B, split_input, n_lat_tiles) + [
            _mod_spec(),
            _mod_spec(),
            pl.BlockSpec(w_ext.shape, lambda i: (0, 0)),
            pl.BlockSpec((TILE, 128), lambda i: (i, 0)),
            pl.BlockSpec((TILE, 128), lambda i: (i, 0)),
        ],
        out_specs=[out_spec] * 9,
        out_shape=[out_shape] * 9,
        compiler_params=_cparams(vmem=VMEM_LIMIT),
        name="ln_mod_in_proj",
    )(*xs, sc, sh, w_ext, cos, sin)


def _channel_dft_matrix():
    c = np.arange(FOURIER_GROUP_W)
    ang = 2.0 * np.pi * np.outer(c, c) / FOURIER_GROUP_W
    eye = np.eye(MIX_GROUP_W // FOURIER_GROUP_W)
    cs = np.concatenate([np.kron(eye, np.cos(ang)), -np.kron(eye, np.sin(ang))], axis=1)
    return jnp.asarray(cs, dtype=BF16)


def _seq_dft_stage_a(n1):
    n2 = FFT_RADIX2
    n = n1 * n2
    k1 = np.arange(n1)[None, :, None]
    i1 = np.arange(n1)[None, None, :]
    i2 = np.arange(n2)[:, None, None]
    th = 2.0 * np.pi * ((k1 * (n2 * i1 + i2)) % n) / n
    wr, wi = np.cos(th), -np.sin(th)
    wa = np.concatenate([np.concatenate([wr, -wi], axis=2), np.concatenate([wi, wr], axis=2)], axis=1)
    return jnp.asarray(wa, dtype=BF16)


def _seq_dft_stage_b():
    n2 = FFT_RADIX2
    k2 = np.arange(n2)
    ph = 2.0 * np.pi * (np.outer(k2, k2) % n2) / n2
    return jnp.asarray(np.concatenate([np.cos(ph), np.sin(ph)], axis=1), dtype=BF16)


def _dense_dft(length):
    k = np.arange(length)
    th = 2.0 * np.pi * (np.outer(k, k) % length) / length
    return jnp.asarray(np.concatenate([np.cos(th), np.sin(th)], axis=1), dtype=BF16)


FFT_A_COLS = 8
FFT_B_COLS = 16


def _fft_a_kernel(a_ref, cs_ref, wa_ref, t_ref, *, nb, n1):
    g = MIX_GROUP_W
    for j in range(FFT_A_COLS):
        for b in range(nb):
            p = _dot(a_ref[b, :, g * j:g * (j + 1)], cs_ref[...])
            pp = jnp.concatenate([p[:, :g], p[:, g:]], axis=0).astype(BF16)
            t = _dot(wa_ref[j], pp)
            t_ref[b, :, 2 * g * j:2 * g * (j + 1)] = jnp.concatenate([t[:n1], t[n1:]], axis=-1).astype(BF16)


def _fft_b_kernel(t_ref, vb_ref, y_ref, *, nb, scale):
    g = MIX_GROUP_W
    n2 = FFT_RADIX2
    for j in range(FFT_B_COLS):
        for b in range(nb):
            t = t_ref[b, n2 * j:n2 * (j + 1), :]
            tt = jnp.concatenate([t[:, :g], t[:, g:]], axis=0)
            y = _dot(vb_ref[...], tt) * scale
            y_ref[b, 0:n2, g * j:g * (j + 1)] = y.astype(BF16)
            y_ref[b, n2:, g * j:g * (j + 1)] = jnp.zeros((y_ref.shape[1] - n2, g), BF16)


def _fft_ctx_kernel(a_ref, cs_ref, wc_ref, yin_ref, y_ref, *, scale):
    del yin_ref
    g = MIX_GROUP_W
    p = _dot(a_ref[...], cs_ref[...])
    pp = jnp.concatenate([p[:, :g], p[:, g:]], axis=0).astype(BF16)
    y_ref[...] = (_dot(wc_ref[...], pp) * scale).astype(BF16)


def _fourier_mix(a, *, B, N, L, with_ctx):
    S = N + L
    g = MIX_GROUP_W
    n2 = FFT_RADIX2
    n1 = N // n2
    cs = _channel_dft_matrix()
    a2 = a.reshape(B, S // n2, n2 * g)
    t = pl.pallas_call(
        functools.partial(_fft_a_kernel, nb=B, n1=n1),
        grid=(n2 // FFT_A_COLS,),
        in_specs=[
            pl.BlockSpec((B, n1, FFT_A_COLS * g), lambda j: (0, 0, j)),
            pl.BlockSpec(cs.shape, lambda j: (0, 0)),
            pl.BlockSpec((FFT_A_COLS, 2 * n1, 2 * n1), lambda j: (j, 0, 0)),
        ],
        out_specs=pl.BlockSpec((B, n1, FFT_A_COLS * 2 * g), lambda j: (0, 0, j)),
        out_shape=jax.ShapeDtypeStruct((B, n1, n2 * 2 * g), BF16),
        name="fourier_seq_stage_a",
    )(a2, cs, _seq_dft_stage_a(n1))
    t = t.reshape(B, n1 * n2, 2 * g)
    rows = S // n1
    y = pl.pallas_call(
        functools.partial(_fft_b_kernel, nb=B, scale=float((N * FOURIER_GROUP_W) ** -0.5)),
        grid=(n1 // FFT_B_COLS,),
        in_specs=[
            pl.BlockSpec((B, FFT_B_COLS * n2, 2 * g), lambda k: (0, k, 0)),
            pl.BlockSpec((n2, 2 * n2), lambda k: (0, 0)),
        ],
        out_specs=pl.BlockSpec((B, rows, FFT_B_COLS * g), lambda k: (0, 0, k)),
        out_shape=jax.ShapeDtypeStruct((B, rows, n1 * g), BF16),
        name="fourier_seq_stage_b",
    )(t, _seq_dft_stage_b())
    y = y.reshape(B, S, g)
    if not with_ctx:
        return y
    ctx_tile = N // TILE
    return pl.pallas_call(
        functools.partial(_fft_ctx_kernel, scale=float((L * FOURIER_GROUP_W) ** -0.5)),
        grid=(B,),
        in_specs=[
            pl.BlockSpec((None, TILE, g), lambda b: (b, ctx_tile, 0)),
            pl.BlockSpec(cs.shape, lambda b: (0, 0)),
            pl.BlockSpec((L, 2 * L), lambda b: (0, 0)),
            pl.BlockSpec(memory_space=pl.ANY),
        ],
        out_specs=pl.BlockSpec((None, TILE, g), lambda b: (b, ctx_tile, 0)),
        out_shape=jax.ShapeDtypeStruct((B, S, g), BF16),
        input_output_aliases={3: 0},
        name="fourier_ctx",
    )(a, cs, _dense_dft(L), y)


POOL_HALO = 16


def _pool_bands():
    t = np.arange(TILE)[:, None]
    s = np.arange(TILE + 2 * POOL_HALO)[None, :] - POOL_HALO
    bands = [((s - t >= -(w // 2)) & (s - t < w // 2)).astype(np.float32) for w in POOL_WINDOWS]
    return jnp.asarray(np.stack(bands), dtype=BF16)


def _pool_kernel(prev_ref, x_ref, next_ref, band_ref, pw_ref, ps_ref, o_ref, *, nb, n_lat, s_tot):
    i = pl.program_id(0)
    t0 = i * TILE
    is_ctx = t0 >= n_lat
    r0 = jnp.where(is_ctx, n_lat, 0)
    r1 = jnp.where(is_ctx, s_tot, n_lat)
    kw = TILE + 2 * POOL_HALO
    s_pos = t0 - POOL_HALO + lax.broadcasted_iota(I32, (kw, 1), 0)
    inside = jnp.where(s_pos >= r0, jnp.where(s_pos < r1, 1.0, 0.0), 0.0).astype(BF16)
    xc = jnp.concatenate([jnp.concatenate([prev_ref[b], x_ref[b], next_ref[b]], axis=0) * inside
                          for b in range(nb)], axis=-1)
    g = MIX_GROUP_W
    t_col = t0 + lax.broadcasted_iota(I32, (TILE, 1), 0)
    group = (lax.broadcasted_iota(I32, (TILE, nb * g), 1) // FOURIER_GROUP_W) % len(POOL_WINDOWS)
    acc = jnp.zeros((TILE, nb * g), F32)
    for gi, w in enumerate(POOL_WINDOWS):
        lo = jnp.maximum(t_col - w // 2, r0)
        hi = jnp.minimum(t_col + w // 2, r1)
        mean = _dot(band_ref[gi], xc) / (hi - lo).astype(F32)
        acc = jnp.where(group == gi, mean, acc)
    for b in range(nb):
        pooled = acc[:, g * b:g * (b + 1)] - x_ref[b].astype(F32)
        o_ref[b] = (_dot(pooled.astype(BF16), pw_ref[...]) * ps_ref[...]).astype(BF16)


def _pool_mix(bp, pw_bd, pscale, *, B, N, L):
    S = N + L
    nt = S // TILE
    g = MIX_GROUP_W
    hb = TILE // POOL_HALO
    last = S // POOL_HALO - 1
    return pl.pallas_call(
        functools.partial(_pool_kernel, nb=B, n_lat=N, s_tot=S),
        grid=(nt,),
        in_specs=[
            pl.BlockSpec((B, POOL_HALO, g), lambda i: (0, jnp.maximum(i * hb - 1, 0), 0)),
            pl.BlockSpec((B, TILE, g), lambda i: (0, i, 0)),
            pl.BlockSpec((B, POOL_HALO, g), lambda i: (0, jnp.minimum((i + 1) * hb, last), 0)),
            pl.BlockSpec((len(POOL_WINDOWS), TILE, TILE + 2 * POOL_HALO), lambda i: (0, 0, 0)),
            pl.BlockSpec((g, g), lambda i: (0, 0)),
            pl.BlockSpec((1, g), lambda i: (0, 0)),
        ],
        out_specs=pl.BlockSpec((B, TILE, g), lambda i: (0, i, 0)),
        out_shape=jax.ShapeDtypeStruct((B, S, g), BF16),
        name="pool_mix",
    )(bp, bp, bp, _pool_bands(), pw_bd, pscale)


def _attn_window_bias(n_ctx):
    r = (np.arange(2 * TILE) % TILE)[:, None]
    c = np.arange(TILE + 2 * WINDOW)[None, :] - WINDOW
    band = np.where(np.abs(r - c) <= WINDOW, 0.0, NEG_INF)
    return jnp.asarray(np.concatenate([band, np.zeros((2 * TILE, n_ctx))], axis=1), dtype=F32)


def _attn_kernel(sink_ref, bias_ref, q_ref, kp_ref, kc_ref, kn_ref, kx_ref, vp_ref, vc_ref, vn_ref, vx_ref, o_ref,
                 *, nb, n_lat):
    i = pl.program_id(0)
    t0 = i * TILE
    nband = TILE + 2 * WINDOW
    nkeys = nband + kx_ref.shape[1]
    col = lax.broadcasted_iota(I32, (1, nkeys), 1)
    kpos = t0 - WINDOW + col
    valid = (col >= nband) | ((kpos >= 0) & (kpos < n_lat) & (t0 < n_lat))
    bias = bias_ref[...]
    lane = lax.broadcasted_iota(I32, (TILE, 2 * HEAD_DIM), 1)
    low = lane < HEAD_DIM
    keep_low = jnp.where(low, 1.0, 0.0).astype(BF16)
    keep_high = jnp.where(low, 0.0, 1.0).astype(BF16)
    row1 = lax.broadcasted_iota(I32, (2 * TILE, 1), 0)
    for b in range(nb):
        outs = []
        for h in range(2):
            sl = slice(2 * HEAD_DIM * h, 2 * HEAD_DIM * (h + 1))
            qh = q_ref[b, :, sl]
            qs = jnp.concatenate([qh * keep_low, qh * keep_high], axis=0)
            kh = jnp.concatenate([kp_ref[b, :, sl], kc_ref[b, :, sl], kn_ref[b, :, sl], kx_ref[b, :, sl]], axis=0)
            vh = jnp.concatenate([vp_ref[b, :, sl], vc_ref[b, :, sl], vn_ref[b, :, sl], vx_ref[b, :, sl]], axis=0)
            s = jnp.where(valid, _dot_nt(qs, kh) + bias, NEG_INF)
            sink = jnp.where(row1 < TILE, sink_ref[2 * h], sink_ref[2 * h + 1])
            m = jnp.maximum(jnp.max(s, axis=-1, keepdims=True), sink)
            p = jnp.exp(s - m)
            den = jnp.sum(p, axis=-1, keepdims=True) + jnp.exp(sink - m)
            o = _dot(p.astype(BF16), vh) / den
            outs.append(jnp.where(low, o[:TILE], o[TILE:]))
        o_ref[b] = jnp.concatenate(outs, axis=-1).astype(BF16)


def _attention(sinks, q, k2, v2, *, B, N, L):
    S = N + L
    nt = S // TILE
    g = MIX_GROUP_W
    hb = TILE // WINDOW
    last = N // WINDOW - 1
    ctx_tile = N // TILE
    prev = pl.BlockSpec((B, WINDOW, g), lambda i: (0, jnp.clip(i * hb - 1, 0, last), 0))
    cur = pl.BlockSpec((B, TILE, g), lambda i: (0, i, 0))
    nxt = pl.BlockSpec((B, WINDOW, g), lambda i: (0, jnp.minimum((i + 1) * hb, last), 0))
    ctx = pl.BlockSpec((B, L, g), lambda i: (0, ctx_tile, 0))
    return pl.pallas_call(
        functools.partial(_attn_kernel, nb=B, n_lat=N),
        grid=(nt,),
        in_specs=[pl.BlockSpec(memory_space=pltpu.SMEM),
                  pl.BlockSpec((2 * TILE, TILE + 2 * WINDOW + L), lambda i: (0, 0)),
                  cur, prev, cur, nxt, ctx, prev, cur, nxt, ctx],
        out_specs=pl.BlockSpec((B, TILE, g), lambda i: (0, i, 0)),
        out_shape=jax.ShapeDtypeStruct((B, S, g), BF16),
        compiler_params=_cparams(vmem=VMEM_LIMIT),
        name="banded_attention",
    )(sinks, _attn_window_bias(L), q, k2, k2, k2, k2, v2, v2, v2, v2)


def _ret_kernel(*refs, reverse, nb):
    if reverse:
        dec_ref, q_ref, k_ref, v_ref, of_ref, g_ref, o_ref, r_scr = refs
    else:
        dec_ref, q_ref, k_ref, v_ref, o_ref, r_scr = refs
    c = RET_CHUNK
    w = MIX_GROUP_W
    dh = w // RET_HEADS

    @pl.when(pl.program_id(0) == 0)
    def _():
        r_scr[...] = jnp.zeros(r_scr.shape, F32)

    lg = -jnp.exp(dec_ref[...])
    idx = lax.broadcasted_iota(I32, (c, 1), 0).astype(F32)
    jdx = lax.broadcasted_iota(I32, (1, c), 1).astype(F32)
    if reverse:
        idx, jdx = (c - 1.0) - idx, (c - 1.0) - jdx
    diff = idx - jdx
    dmask_all = jnp.concatenate(
        [jnp.where(diff >= 0, jnp.exp(jnp.maximum(diff, 0.0) * lg[:, dh * h:dh * h + 1]), 0.0)
         for h in range(RET_HEADS)], axis=1)
    xi = jnp.exp((idx + 1.0) * lg)
    zeta = jnp.exp((c - 1.0 - idx) * lg)
    chunk_decay = jnp.exp(float(c) * lg)
    lane_head = lax.broadcasted_iota(I32, (1, w), 1) // dh
    keep_head = [jnp.where(lane_head == h, 1.0, 0.0).astype(BF16) for h in range(RET_HEADS)]
    same_head = (lax.broadcasted_iota(I32, (w, w), 0) // dh) == (lax.broadcasted_iota(I32, (w, w), 1) // dh)

    for b in range(nb):
        q = q_ref[b]
        k = k_ref[b] * jnp.asarray(dh ** -0.5, BF16)
        v = v_ref[b]
        state = r_scr[b]
        k_heads = jnp.concatenate([k * keep_head[h] for h in range(RET_HEADS)], axis=0)
        v_heads = jnp.concatenate([v * keep_head[h] for h in range(RET_HEADS)], axis=0)
        s = _dot_nt(q, k_heads) * dmask_all
        o = _dot(q, state.astype(BF16)) * xi + _dot(s.astype(BF16), v_heads)
        kv = _dot_tn(k, (v.astype(F32) * zeta).astype(BF16))
        r_scr[b] = state * chunk_decay + jnp.where(same_head, kv, 0.0)
        if not reverse:
            o_ref[b] = o
        else:
            tot = o + of_ref[b]
            mu = jnp.zeros_like(tot)
            for h in range(RET_HEADS):
                mh = lane_head == h
                mu = jnp.where(mh, jnp.sum(jnp.where(mh, tot, 0.0), axis=-1, keepdims=True) / dh, mu)
            cen = tot - mu
            var = jnp.zeros_like(tot)
            for h in range(RET_HEADS):
                mh = lane_head == h
                var = jnp.where(mh, jnp.sum(jnp.where(mh, cen * cen, 0.0), axis=-1, keepdims=True) / dh, var)
            gate = g_ref[b].astype(F32)
            o_ref[b] = (gate * jax.nn.sigmoid(gate) * (cen * lax.rsqrt(var + LN_EPS))).astype(BF16)


def _retention(dec, rq, rk, rv, rg, *, B, N, L):
    S = N + L
    c = RET_CHUNK
    w = MIX_GROUP_W
    nc_lat, nc_ctx = N // c, L // c
    nc = nc_lat + nc_ctx

    def fwd_blk(j):
        return jnp.where(j < nc_ctx, nc_lat + j, j - nc_ctx)

    def bwd_blk(j):
        return nc - 1 - j

    def call(blk, reverse, extra):
        spec = pl.BlockSpec((B, c, w), lambda j: (0, blk(j), 0))
        dspec = pl.BlockSpec((1, w), lambda j: (0, 0))
        return pl.pallas_call(
            functools.partial(_ret_kernel, reverse=reverse, nb=B),
            grid=(nc,),
            in_specs=[dspec] + [spec] * (3 + len(extra)),
            out_specs=spec,
            out_shape=jax.ShapeDtypeStruct((B, S, w), BF16 if reverse else F32),
            scratch_shapes=[pltpu.VMEM((B, w, w), F32)],
            compiler_params=_cparams(sem=("arbitrary",), vmem=VMEM_LIMIT),
            name="retention_bwd" if reverse else "retention_fwd",
        )(dec[1:2] if reverse else dec[0:1], rq, rk, rv, *extra)

    o_f = call(fwd_blk, False, ())
    return call(bwd_blk, True, (o_f, rg))


def _merge_kernel(*refs, nb, split_input, n_lat_tiles, alpha):
    for b in range(nb):
        x, rest = _tile_inputs(refs, b, split_input, n_lat_tiles)
        (yf_ref, yp_ref, ya_ref, yr_ref, g1_ref, sc_ref, sh_ref, lg_ref, lb_ref, wo_ref, rw_ref, rb_ref,
         x1_ref, h2_ref) = rest[:14]
        row = _mod_row(b, n_lat_tiles, nb)
        ycat = jnp.concatenate([yf_ref[b], yp_ref[b], ya_ref[b], yr_ref[b]], axis=-1)
        y = _dot(ycat, wo_ref[...])
        x1 = _normalize(alpha * x + g1_ref[row] * y) * lg_ref[...] + lb_ref[...]
        x1_ref[b] = x1
        h2 = _normalize(x1) * (1.0 + sc_ref[row]) + sh_ref[row]
        _to_token_tiles(h2_ref.at[b], h2)
        logits = _dot_nt(rw_ref[...], h2.astype(BF16)) + rb_ref[...]
        _route(logits, rest[14 + b], rest[14 + nb + b])


def _route(logits, idx_ref, gate_ref):
    eid = lax.broadcasted_iota(I32, logits.shape, 0)
    orow = lax.broadcasted_iota(I32, (8, logits.shape[1]), 0)
    idx_out = jnp.zeros((8, logits.shape[1]), I32)
    val_out = jnp.zeros((8, logits.shape[1]), F32)
    top = None
    den = None
    for kk in range(TOP_K):
        m = jnp.max(logits, axis=0, keepdims=True)
        sel = jnp.min(jnp.where(logits == m, eid, N_EXPERTS), axis=0, keepdims=True)
        logits = jnp.where(eid == sel, -jnp.inf, logits)
        if kk == 0:
            top = m
        e = jnp.exp(m - top)
        den = e if kk == 0 else den + e
        idx_out = jnp.where(orow == kk, sel, idx_out)
        val_out = jnp.where(orow == kk, e, val_out)
    idx_ref[...] = idx_out
    gate_ref[...] = val_out / den


def _merge(xs, ys, g1, sc2, sh2, ln_g, ln_b, w_out, rw_t, rb, *, B, S, n_lat_tiles, alpha):
    split_input = len(xs) == 2
    nt = S // TILE
    g = MIX_GROUP_W
    r = ROWS_PER_TOKEN
    yspec = pl.BlockSpec((B, TILE, g), lambda i: (0, i, 0))
    vec = pl.BlockSpec((1, D_MODEL), lambda i: (0, 0))
    route_spec = pl.BlockSpec((8, TILE), lambda i: (0, i))
    outs = pl.pallas_call(
        functools.partial(_merge_kernel, nb=B, split_input=split_input, n_lat_tiles=n_lat_tiles, alpha=alpha),
        grid=(nt,),
        in_specs=_x_specs(B, split_input, n_lat_tiles) + [yspec] * 4 + [
            _mod_spec(), _mod_spec(), _mod_spec(), vec, vec,
            pl.BlockSpec((D_MODEL, D_MODEL), lambda i: (0, 0)),
            pl.BlockSpec((N_EXPERTS, D_MODEL), lambda i: (0, 0)),
            pl.BlockSpec((N_EXPERTS, 1), lambda i: (0, 0)),
        ],
        out_specs=[
            pl.BlockSpec((B, TILE, D_MODEL), lambda i: (0, i, 0)),
            pl.BlockSpec((B, TILE * r, 128), lambda i: (0, i, 0)),
        ] + [route_spec] * (2 * B),
        out_shape=[
            jax.ShapeDtypeStruct((B, S, D_MODEL), F32),
            jax.ShapeDtypeStruct((B, S * r, 128), F32),
        ] + [jax.ShapeDtypeStruct((8, S), I32)] * B + [jax.ShapeDtypeStruct((8, S), F32)] * B,
        compiler_params=_cparams(vmem=VMEM_LIMIT),
        name="out_proj_norm_router",
    )(*xs, *ys, g1, sc2, sh2, ln_g, ln_b, w_out, rw_t, rb)
    x1, h2t = outs[:2]
    idx = jnp.concatenate(outs[2:2 + B], axis=1)
    gates = jnp.concatenate(outs[2 + B:], axis=1)
    return x1, h2t.reshape(B * S * r, 128), idx, gates


def _rank_kernel(idx_ref, before_ref, lpos_ref, ntab_ref, cbtab_ref, cnt_ref, carry_scr):
    tt = idx_ref.shape[1]

    @pl.when(pl.program_id(0) == 0)
    def _():
        carry_scr[...] = jnp.zeros(carry_scr.shape, F32)

    eid = lax.broadcasted_iota(I32, (N_EXPERTS, tt), 0)
    before = before_ref[...]
    hits, cums, pres = [], [], []
    n_tile = jnp.zeros((N_EXPERTS, 1), F32)
    for kk in range(TOP_K):
        hit = eid == idx_ref[kk:kk + 1, :]
        hits.append(hit)
        cums.append(_dot(jnp.where(hit, 1.0, 0.0).astype(BF16), before))
        pres.append(n_tile)
        n_tile = n_tile + jnp.sum(jnp.where(hit, 1.0, 0.0), axis=1, keepdims=True)
    n16 = ((n_tile.astype(I32) + (RUN_ALIGN - 1)) >> RUN_SHIFT) << RUN_SHIFT
    lower = jnp.where(lax.broadcasted_iota(I32, (N_EXPERTS, N_EXPERTS), 1)
                      < lax.broadcasted_iota(I32, (N_EXPERTS, N_EXPERTS), 0), 1.0, 0.0).astype(BF16)
    hi = jnp.broadcast_to((n16 >> 8).astype(F32), (N_EXPERTS, 128)).astype(BF16)
    lo = jnp.broadcast_to((n16 & 255).astype(F32), (N_EXPERTS, 128)).astype(BF16)
    lstart = (256.0 * _dot(lower, hi) + _dot(lower, lo))[:, 0:1]
    orow = lax.broadcasted_iota(I32, (8, tt), 0)
    out = jnp.zeros((8, tt), I32)
    for kk in range(TOP_K):
        slot = jnp.sum(jnp.where(hits[kk], lstart + pres[kk] + cums[kk], 0.0), axis=0, keepdims=True)
        out = jnp.where(orow == kk, slot.astype(I32) * ROWS_PER_TOKEN, out)
    lpos_ref[...] = out
    carry = carry_scr[...]
    ntab_ref[...] = jnp.broadcast_to(n_tile, ntab_ref.shape).astype(I32)
    cbtab_ref[...] = carry.astype(I32)
    carry = carry + n_tile
    carry_scr[...] = carry
    cnt_ref[...] = carry


def _rank(idx, *, T):
    tt = MOE_TOK_TILE
    nt = T // tt
    tab = pl.BlockSpec((None, N_EXPERTS, 128), lambda i: (i, 0, 0))
    return pl.pallas_call(
        _rank_kernel,
        grid=(nt,),
        in_specs=[pl.BlockSpec((8, tt), lambda i: (0, i)), pl.BlockSpec((tt, tt), lambda i: (0, 0))],
        out_specs=[pl.BlockSpec((8, tt), lambda i: (0, i)), tab, tab,
                   pl.BlockSpec((N_EXPERTS, 128), lambda i: (0, 0))],
        out_shape=[jax.ShapeDtypeStruct((8, T), I32), jax.ShapeDtypeStruct((nt, N_EXPERTS, 128), I32),
                   jax.ShapeDtypeStruct((nt, N_EXPERTS, 128), I32), jax.ShapeDtypeStruct((N_EXPERTS, 128), F32)],
        scratch_shapes=[pltpu.VMEM((N_EXPERTS, 128), F32)],
        compiler_params=_cparams(sem=("arbitrary",)),
        name="moe_rank",
    )(idx, jnp.asarray(np.triu(np.ones((tt, tt), np.float32), 1), dtype=BF16))


def _run_chunk(src_ref, src_tok, dst_ref, dst_tok, sem):
    return pltpu.make_async_copy(src_ref.at[_token_rows(src_tok, RUN_ALIGN), :],
                                 dst_ref.at[_token_rows(dst_tok, RUN_ALIGN), :], sem)


def _wait_chunks(src_ref, dst_ref, sem, count):
    def body(_, carry):
        _run_chunk(src_ref, 0, dst_ref, 0, sem).wait()
        return carry
    lax.fori_loop(0, count, body, 0)


def _copy_runs(step, pstart_ref, ntab_ref, cbtab_ref, chunk):
    def per_expert(e, carry):
        src0, total = carry
        nch = (ntab_ref[step * N_EXPERTS + e] + (RUN_ALIGN - 1)) >> RUN_SHIFT
        dst0 = pstart_ref[e] + cbtab_ref[step * N_EXPERTS + e]

        def one(c, cc):
            chunk(src0 + c * RUN_ALIGN, dst0 + c * RUN_ALIGN).start()
            return cc
        lax.fori_loop(0, nch, one, 0)
        return src0 + nch * RUN_ALIGN, total + nch
    return lax.fori_loop(0, N_EXPERTS, per_expert, (0, 0))[1]


def _dispatch_kernel(pstart_ref, cnt_ref, pad_ref, ntab_ref, cbtab_ref, lpos_ref, h_ref, xs_ref,
                     stage, zero_scr, sems, sent):
    step = pl.program_id(0)

    @pl.when(step == 0)
    def _():
        sem = sems.at[0]
        zero_scr[...] = jnp.zeros(zero_scr.shape, F32)

        def per_expert(e, total):
            first = pstart_ref[e] + ((cnt_ref[e] >> RUN_SHIFT) << RUN_SHIFT)
            n_fill = (pstart_ref[e] + pad_ref[e] - first) >> RUN_SHIFT

            def fill(c, carry):
                _run_chunk(zero_scr, 0, xs_ref, first + c * RUN_ALIGN, sem).start()
                return carry
            lax.fori_loop(0, n_fill, fill, 0)
            return total + n_fill
        n_total = lax.fori_loop(0, N_EXPERTS, per_expert, 0)
        used = pstart_ref[N_EXPERTS - 1] + pad_ref[N_EXPERTS - 1]
        n_tail = (xs_ref.shape[0] // ROWS_PER_TOKEN - used) >> RUN_SHIFT

        def fill_tail(c, carry):
            _run_chunk(zero_scr, 0, xs_ref, used + c * RUN_ALIGN, sem).start()
            return carry
        lax.fori_loop(0, n_tail, fill_tail, 0)
        _wait_chunks(zero_scr, xs_ref, sem, n_total + n_tail)
        sent[0] = 0
        sent[1] = 0

    slot = step % 2
    buf = stage.at[slot]
    sem = sems.at[slot]

    def zero_last_group(e, src0):
        n_up = ((ntab_ref[step * N_EXPERTS + e] + (RUN_ALIGN - 1)) >> RUN_SHIFT) << RUN_SHIFT
        buf[_token_rows(jnp.maximum(src0 + n_up - RUN_ALIGN, 0), RUN_ALIGN), :] = jnp.zeros((CHUNK_ROWS, 128), F32)
        return src0 + n_up
    lax.fori_loop(0, N_EXPERTS, zero_last_group, 0)
    tt = MOE_TOK_TILE
    for kk in range(TOP_K):
        def place(t, carry, kk=kk):
            row = pl.multiple_of(lpos_ref[kk * tt + t], ROWS_PER_TOKEN)
            buf[pl.ds(row, ROWS_PER_TOKEN), :] = h_ref[_token_rows(t), :]
            return carry
        lax.fori_loop(0, tt, place, 0, unroll=16)
    _wait_chunks(stage.at[1 - slot], xs_ref, sems.at[1 - slot], sent[1 - slot])
    sent[1 - slot] = 0
    sent[slot] = _copy_runs(step, pstart_ref, ntab_ref, cbtab_ref,
                            lambda s, d: _run_chunk(buf, s, xs_ref, d, sem))

    @pl.when(step == pl.num_programs(0) - 1)
    def _():
        _wait_chunks(buf, xs_ref, sem, sent[slot])


def _smem_tile_spec(tt):
    return pl.BlockSpec((8 * tt,), lambda i, *_: (i,), memory_space=pltpu.SMEM)


def _tile_major(table):
    tt = MOE_TOK_TILE
    return table.reshape(table.shape[0], -1, tt).transpose(1, 0, 2).reshape(-1)


def _dispatch(pstart, cnt, padded, ntab, cbtab, lpos, h2t, *, T, P):
    tt = MOE_TOK_TILE
    r = ROWS_PER_TOKEN
    return pl.pallas_call(
        _dispatch_kernel,
        grid_spec=pltpu.PrefetchScalarGridSpec(
            num_scalar_prefetch=5,
            grid=(T // tt,),
            in_specs=[_smem_tile_spec(tt), pl.BlockSpec((tt * r, 128), lambda i, *_: (i, 0))],
            out_specs=pl.BlockSpec(memory_space=pl.ANY),
            scratch_shapes=[pltpu.VMEM((2, STAGE_TOKENS * r, 128), F32), pltpu.VMEM((CHUNK_ROWS, 128), F32),
                            pltpu.SemaphoreType.DMA((2,)), pltpu.SMEM((2,), I32)],
        ),
        out_shape=jax.ShapeDtypeStruct((P * r, 128), F32),
        compiler_params=_cparams(sem=("arbitrary",), vmem=EXPERT_VMEM_LIMIT),
        name="moe_dispatch",
    )(pstart, cnt, padded, ntab, cbtab, _tile_major(lpos), h2t)


def _collect_kernel(pstart_ref, ntab_ref, cbtab_ref, lpos_ref, gate_ref, ys_ref, f_ref, stage, sems, sent):
    step = pl.program_id(0)
    slot = step % 2

    def fetch(tile, b):
        sent[b] = _copy_runs(tile, pstart_ref, ntab_ref, cbtab_ref,
                             lambda s, d: _run_chunk(ys_ref, d, stage.at[b], s, sems.at[b]))

    @pl.when(step == 0)
    def _():
        fetch(0, 0)

    @pl.when(step + 1 < pl.num_programs(0))
    def _():
        fetch(step + 1, 1 - slot)

    buf = stage.at[slot]
    _wait_chunks(ys_ref, buf, sems.at[slot], sent[slot])

    tt = MOE_TOK_TILE

    def combine(t, carry):
        acc = None
        for kk in range(TOP_K):
            row = pl.multiple_of(lpos_ref[kk * tt + t], ROWS_PER_TOKEN)
            term = gate_ref[kk * tt + t] * buf[pl.ds(row, ROWS_PER_TOKEN), :]
            acc = term if acc is None else acc + term
        f_ref[_token_rows(t), :] = acc
        return carry
    lax.fori_loop(0, tt, combine, 0, unroll=8)


def _collect(pstart, ntab, cbtab, lpos, gates, ys, *, T):
    tt = MOE_TOK_TILE
    r = ROWS_PER_TOKEN
    return pl.pallas_call(
        _collect_kernel,
        grid_spec=pltpu.PrefetchScalarGridSpec(
            num_scalar_prefetch=3,
            grid=(T // tt,),
            in_specs=[_smem_tile_spec(tt), _smem_tile_spec(tt), pl.BlockSpec(memory_space=pl.ANY)],
            out_specs=pl.BlockSpec((tt * r, 128), lambda i, *_: (i, 0)),
            scratch_shapes=[pltpu.VMEM((2, STAGE_TOKENS * r, 128), F32), pltpu.SemaphoreType.DMA((2,)),
                            pltpu.SMEM((2,), I32)],
        ),
        out_shape=jax.ShapeDtypeStruct((T * r, 128), F32),
        compiler_params=_cparams(sem=("arbitrary",), vmem=EXPERT_VMEM_LIMIT),
        name="moe_collect",
    )(pstart, ntab, cbtab, _tile_major(lpos), _tile_major(gates), ys)


def _deinterleave_matrix():
    m = np.zeros((256, 256), np.float32)
    j = np.arange(128)
    m[2 * j, j] = 1.0
    m[2 * j + 1, 128 + j] = 1.0
    return jnp.asarray(m, dtype=BF16)


def _expert_kernel(be_ref, nu_ref, x_ref, w1_hbm, b1_ref, w2_hbm, b2_ref, perm_ref, y_ref,
                   w1_buf, w2_buf, w1_scr, w2_scr, sem1, sem2, *, layer):
    i = pl.program_id(0)

    def weight_copies(e):
        slot = e % 2
        return (pltpu.make_async_copy(w1_hbm.at[layer, e], w1_buf.at[slot], sem1.at[slot]),
                pltpu.make_async_copy(w2_hbm.at[layer, e], w2_buf.at[slot], sem2.at[slot]))

    @pl.when(i >= nu_ref[0])
    def _():
        y_ref[...] = jnp.zeros(y_ref.shape, F32)

    @pl.when(i < nu_ref[0])
    def _():
        e = be_ref[i]

        @pl.when((i == 0) | (e != be_ref[jnp.maximum(i - 1, 0)]))
        def _():
            @pl.when(i == 0)
            def _():
                for cp in weight_copies(e):
                    cp.start()
            for cp in weight_copies(e):
                cp.wait()

            @pl.when(e + 1 < N_EXPERTS)
            def _():
                for cp in weight_copies(e + 1):
                    cp.start()
            w1_ref = w1_buf.at[e % 2]
            half = w1_scr.shape[1] // 2
            for m in range(w1_scr.shape[1] // 256):
                r = _dot(w1_ref[:, 256 * m:256 * (m + 1)].astype(BF16), perm_ref[...])
                w1_scr[:, 128 * m:128 * (m + 1)] = r[:, :128].astype(BF16)
                w1_scr[:, half + 128 * m:half + 128 * (m + 1)] = r[:, 128:].astype(BF16)
            w2_scr[...] = w2_buf[e % 2].astype(BF16)

        x = _from_token_tiles(x_ref, MOE_BLK).astype(BF16)
        u = _dot(x, w1_scr[...]) + b1_ref[...]
        half = u.shape[1] // 2
        glu = jnp.minimum(u[:, :half], SWIGLU_LIMIT)
        lin = jnp.clip(u[:, half:], -SWIGLU_LIMIT, SWIGLU_LIMIT)
        act = glu * jax.nn.sigmoid(SWIGLU_ALPHA * glu) * (lin + 1.0)
        _to_token_tiles(y_ref, _dot(act.astype(BF16), w2_scr[...]) + b2_ref[...])


def _experts(blk_e, n_used, xs, w1, b1, w2, b2, *, P, layer):
    nblk = P // MOE_BLK
    r = ROWS_PER_TOKEN
    d, two_f = w1.shape[2:]
    f = w2.shape[2]
    rows = lambda i, be, nu: (jnp.minimum(i, nu[0] - 1), 0)
    return pl.pallas_call(
        functools.partial(_expert_kernel, layer=layer),
        grid_spec=pltpu.PrefetchScalarGridSpec(
            num_scalar_prefetch=2,
            grid=(nblk,),
            in_specs=[
                pl.BlockSpec((MOE_BLK * r, 128), rows),
                pl.BlockSpec(memory_space=pl.ANY),
                pl.BlockSpec((None, 1, two_f), lambda i, be, nu: (be[i], 0, 0)),
                pl.BlockSpec(memory_space=pl.ANY),
                pl.BlockSpec((None, 1, d), lambda i, be, nu: (be[i], 0, 0)),
                pl.BlockSpec((256, 256), lambda i, be, nu: (0, 0)),
            ],
            out_specs=pl.BlockSpec((MOE_BLK * r, 128), lambda i, be, nu: (i, 0)),
            scratch_shapes=[pltpu.VMEM((2, d, two_f), F32), pltpu.VMEM((2, f, d), F32),
                            pltpu.VMEM((d, two_f), BF16), pltpu.VMEM((f, d), BF16),
                            pltpu.SemaphoreType.DMA((2,)), pltpu.SemaphoreType.DMA((2,))],
        ),
        out_shape=jax.ShapeDtypeStruct((P * r, 128), F32),
        compiler_params=_cparams(sem=("arbitrary",), vmem=EXPERT_VMEM_LIMIT),
        name="moe_experts",
    )(blk_e, n_used, xs, w1, b1, w2, b2, _deinterleave_matrix())


def _moe(h2t, idx, gates, w1, b1, w2, b2, *, T, layer):
    lpos, ntab, cbtab, cnt = _rank(idx, T=T)
    counts = cnt[:, 0].astype(I32)
    padded = (counts + RUN_ALIGN + MOE_BLK - 1) // MOE_BLK * MOE_BLK
    pad_end = jnp.cumsum(padded)
    pstart = (pad_end - padded).astype(I32)
    nblk = TOP_K * T // MOE_BLK + 2 * N_EXPERTS
    P = nblk * MOE_BLK
    starts = jnp.arange(nblk, dtype=I32) * MOE_BLK
    blk_e = jnp.minimum(jnp.sum((pad_end[None, :] <= starts[:, None]).astype(I32), axis=1), N_EXPERTS - 1)
    n_used = (pad_end[-1:] // MOE_BLK).astype(I32)
    ntab = ntab[:, :, 0].reshape(-1)
    cbtab = cbtab[:, :, 0].reshape(-1)
    xs = _dispatch(pstart, counts, padded.astype(I32), ntab, cbtab, lpos, h2t, T=T, P=P)
    ys = _experts(blk_e.astype(I32), n_used, xs, w1, b1, w2, b2, P=P, layer=layer)
    return _collect(pstart, ntab, cbtab, lpos, gates, ys, T=T)


def _residual_kernel(f_ref, x_ref, g2_ref, lg_ref, lb_ref, o_ref, *, nb, n_lat_tiles, alpha):
    for b in range(nb):
        f = _from_token_tiles(f_ref.at[b], TILE)
        z = alpha * x_ref[b] + g2_ref[_mod_row(b, n_lat_tiles, nb)] * f
        o_ref[b] = _normalize(z) * lg_ref[...] + lb_ref[...]


def _residual(ft, x1, g2, ln_g, ln_b, *, B, S, n_out_tiles, n_lat_tiles, alpha):
    r = ROWS_PER_TOKEN
    vec = pl.BlockSpec((1, D_MODEL), lambda i: (0, 0))
    return pl.pallas_call(
        functools.partial(_residual_kernel, nb=B, n_lat_tiles=n_lat_tiles, alpha=alpha),
        grid=(n_out_tiles,),
        in_specs=[
            pl.BlockSpec((B, TILE * r, 128), lambda i: (0, i, 0)),
            pl.BlockSpec((B, TILE, D_MODEL), lambda i: (0, i, 0)),
            _mod_spec(), vec, vec,
        ],
        out_specs=pl.BlockSpec((B, TILE, D_MODEL), lambda i: (0, i, 0)),
        out_shape=jax.ShapeDtypeStruct((B, n_out_tiles * TILE, D_MODEL), F32),
        compiler_params=_cparams(vmem=VMEM_LIMIT),
        name="ffn_residual_norm",
    )(ft.reshape(B, S * r, 128), x1, g2, ln_g, ln_b)


def _rope_tables(n_lat, n_ctx):
    half = HEAD_DIM // 2
    pos = np.arange(n_lat)
    inv = 1.0 / (ROPE_BASE ** (np.arange(0, half, 2, dtype=np.float64) / half))
    ang_row = (pos // GRID_W)[:, None].astype(np.float64) * inv[None, :]
    ang_col = (pos % GRID_W)[:, None].astype(np.float64) * inv[None, :]
    ang = np.concatenate([ang_row, ang_row, ang_col, ang_col], axis=1)
    sign = np.concatenate([-np.ones(16), np.ones(16)] * 2)[None, :]
    cos = np.concatenate([np.cos(ang), np.ones((n_ctx, HEAD_DIM))], axis=0)
    sin = np.concatenate([np.sin(ang) * sign, np.zeros((n_ctx, HEAD_DIM))], axis=0)
    return (jnp.asarray(np.tile(cos, (1, 2)), dtype=F32), jnp.asarray(np.tile(sin, (1, 2)), dtype=F32))


def _extend_w_in(w):
    g = MIX_GROUP_W
    hd = HEAD_DIM
    a, bp, q = w[:, 0:g], w[:, g:2 * g], w[:, 2 * g:3 * g]
    k, v = w[:, 3 * g:3 * g + 2 * hd], w[:, 3 * g + 2 * hd:4 * g]
    rest = w[:, 4 * g:]
    dup = lambda t: jnp.concatenate([t[:, :hd], t[:, :hd], t[:, hd:], t[:, hd:]], axis=1)
    return jnp.concatenate([a, bp, q, dup(k), dup(v), rest], axis=1).astype(BF16)


def _block_diag(pw):
    n, c, _ = pw.shape
    out = jnp.zeros((n * c, n * c), pw.dtype)
    for gi in range(n):
        out = lax.dynamic_update_slice(out, pw[gi], (gi * c, gi * c))
    return out


def kernel(x, c, ctx, c_ctx, w_mod, b_mod, w_in, pool_w, pool_scale, attn_sink, ret_decay, w_out, ln_g, ln_b,
           router_w, router_b, exp_w1, exp_b1, exp_w2, exp_b2):
    B, N, D = x.shape
    L = ctx.shape[1]
    depth = w_mod.shape[0]
    assert D == D_MODEL and L == TILE and N % TILE == 0 and N % (FFT_RADIX2 * 16) == 0 and B < 8
    S = N + L
    n_lat_tiles = N // TILE
    assert (B * S) % MOE_TOK_TILE == 0 and (B * N) % MOE_TOK_TILE == 0
    alpha = float((2 * depth) ** 0.25)

    cvec = jnp.zeros((8, D), F32).at[:B].set(c).at[B].set(c_ctx)
    mods = _modulation(cvec, w_mod, b_mod)
    cos, sin = _rope_tables(N, L)

    xs = (x, ctx)
    out = None
    for l in range(depth):
        last = l == depth - 1
        m = mods[l].reshape(8, 6, 1, D)
        sh1, sc1, g1, sh2, sc2, g2 = [m[:, j] for j in range(6)]
        proj = _projection(xs, sc1, sh1, _extend_w_in(w_in[l]), cos, sin, B=B, S=S, n_lat_tiles=n_lat_tiles)
        a, bp, q, k2, v2, rq, rk, rv, rg = proj
        y_f = _fourier_mix(a, B=B, N=N, L=L, with_ctx=not last)
        y_p = _pool_mix(bp, _block_diag(pool_w[l]).astype(BF16), pool_scale[l].reshape(1, -1), B=B, N=N, L=L)
        y_a = _attention(attn_sink[l], q, k2, v2, B=B, N=N, L=L)
        dec = jnp.repeat(ret_decay[l], MIX_GROUP_W // RET_HEADS, axis=1)
        y_r = _retention(dec, rq, rk, rv, rg, B=B, N=N, L=L)
        rows = N if last else S
        x1, h2t, idx, gates = _merge(
            xs, (y_f, y_p, y_a, y_r), g1, sc2, sh2, ln_g[l, 0:1], ln_b[l, 0:1], w_out[l].astype(BF16),
            router_w[l].T.astype(BF16), router_b[l].reshape(-1, 1), B=B, S=rows, n_lat_tiles=n_lat_tiles, alpha=alpha)
        b1 = jnp.concatenate([exp_b1[l][:, 0::2], exp_b1[l][:, 1::2]], axis=-1)[:, None, :]
        ft = _moe(h2t, idx, gates, exp_w1, b1, exp_w2, exp_b2[l][:, None, :], T=B * rows, layer=l)
        out = _residual(ft, x1, g2, ln_g[l, 1:2], ln_b[l, 1:2], B=B, S=rows,
                        n_out_tiles=rows // TILE, n_lat_tiles=n_lat_tiles, alpha=alpha)
        xs = (out,)
    return out
```

```python
import functools

import numpy as np
import jax
import jax.numpy as jnp
from jax import lax
from jax.experimental import pallas as pl
from jax.experimental.pallas import tpu as pltpu

F32 = jnp.float32
BF16 = jnp.bfloat16
I32 = jnp.int32

D_MODEL = 1024
GRID_W = 64
FOURIER_GROUP_W = 64
POOL_WINDOWS = (2, 4, 8, 16)
HEAD_DIM = 64
WINDOW = 128
ROPE_BASE = 10000.0
RET_HEADS = 4
RET_CHUNK = 128
N_EXPERTS = 32
TOP_K = 4
SWIGLU_LIMIT = 7.0
SWIGLU_ALPHA = 1.702
LN_EPS = 1e-6
NEG_INF = -1e30
MIX_GROUP_W = 256
FFT_RADIX2 = 64

TILE = 256
MOE_BLK = 512
MOE_TOK_TILE = 1024
ROWS_PER_TOKEN = D_MODEL // 128
RUN_ALIGN = 16
RUN_SHIFT = 4
CHUNK_ROWS = RUN_ALIGN * ROWS_PER_TOKEN
STAGE_TOKENS = TOP_K * MOE_TOK_TILE + N_EXPERTS * RUN_ALIGN
VMEM_LIMIT = 48 * 1024 * 1024
EXPERT_VMEM_LIMIT = 56 * 1024 * 1024


def _dot(a, b):
    return jnp.dot(a, b, preferred_element_type=F32)


def _dot_nt(a, b):
    return lax.dot_general(a, b, (((1,), (1,)), ((), ())), preferred_element_type=F32)


def _dot_tn(a, b):
    return lax.dot_general(a, b, (((0,), (0,)), ((), ())), preferred_element_type=F32)


def _normalize(x):
    mu = jnp.mean(x, axis=-1, keepdims=True)
    xc = x - mu
    var = jnp.mean(xc * xc, axis=-1, keepdims=True)
    return xc * lax.rsqrt(var + LN_EPS)


def _cparams(sem=None, vmem=None):
    kw = {}
    if sem is not None:
        kw["dimension_semantics"] = sem
    if vmem is not None:
        kw["vmem_limit_bytes"] = vmem
    return pltpu.CompilerParams(**kw)


def _to_token_tiles(ref, val):
    n = val.shape[0]
    for c in range(ROWS_PER_TOKEN):
        ref[pl.ds(c, n, stride=ROWS_PER_TOKEN), :] = val[:, 128 * c:128 * (c + 1)]


def _from_token_tiles(ref, n):
    return jnp.concatenate([ref[pl.ds(c, n, stride=ROWS_PER_TOKEN), :] for c in range(ROWS_PER_TOKEN)], axis=-1)


def _token_rows(tok, count=1):
    return pl.ds(pl.multiple_of(tok * ROWS_PER_TOKEN, ROWS_PER_TOKEN), count * ROWS_PER_TOKEN)


def _mod_kernel(c_ref, w_ref, b_ref, o_ref):
    c = c_ref[...]
    a = c * jax.nn.sigmoid(c)
    w = w_ref[...]
    a_hi = a.astype(BF16)
    a_lo = (a - a_hi.astype(F32)).astype(BF16)
    w_hi = w.astype(BF16)
    w_lo = (w - w_hi.astype(F32)).astype(BF16)
    o_ref[...] = _dot(a_hi, w_hi) + _dot(a_lo, w_hi) + _dot(a_hi, w_lo) + b_ref[...]


def _modulation(cvec, w_mod, b_mod):
    depth, d, six_d = w_mod.shape
    nb = 1536
    return pl.pallas_call(
        _mod_kernel,
        grid=(depth, six_d // nb),
        in_specs=[
            pl.BlockSpec((8, d), lambda l, j: (0, 0)),
            pl.BlockSpec((None, d, nb), lambda l, j: (l, 0, j)),
            pl.BlockSpec((None, 1, nb), lambda l, j: (l, 0, j)),
        ],
        out_specs=pl.BlockSpec((None, 8, nb), lambda l, j: (l, 0, j)),
        out_shape=jax.ShapeDtypeStruct((depth, 8, six_d), F32),
        compiler_params=_cparams(vmem=VMEM_LIMIT),
        name="modulation",
    )(cvec, w_mod, b_mod.reshape(depth, 1, six_d))


def _tile_inputs(refs, b, split_input, n_lat_tiles):
    if split_input:
        x = jnp.where(pl.program_id(0) == n_lat_tiles, refs[1][b], refs[0][b])
        return x, refs[2:]
    return refs[0][b], refs[1:]


def _mod_row(b, n_lat_tiles, ctx_row):
    return jnp.where(pl.program_id(0) == n_lat_tiles, ctx_row, b)


def _proj_kernel(*refs, nb, split_input, n_lat_tiles):
    g = MIX_GROUP_W
    for b in range(nb):
        x, rest = _tile_inputs(refs, b, split_input, n_lat_tiles)
        sc_ref, sh_ref, w_ref, cos_ref, sin_ref = rest[:5]
        a_ref, bp_ref, q_ref, k_ref, v_ref, rq_ref, rk_ref, rv_ref, rg_ref = rest[5:]
        row = _mod_row(b, n_lat_tiles, nb)
        h = _normalize(x) * (1.0 + sc_ref[row]) + sh_ref[row]
        p = _dot(h.astype(BF16), w_ref[...])
        a_ref[b] = p[:, 0:g].astype(BF16)
        bp_ref[b] = p[:, g:2 * g].astype(BF16)
        qk = p[:, 2 * g:4 * g]
        lane = lax.broadcasted_iota(I32, qk.shape, 1)
        first = (lane & 16) == 0
        partner = jnp.where(first, pltpu.roll(qk, 2 * g - 16, 1), pltpu.roll(qk, 16, 1))
        cos = jnp.concatenate([cos_ref[...]] * 4, axis=-1)
        sin = jnp.concatenate([sin_ref[...]] * 4, axis=-1)
        qk = qk * cos + partner * sin
        q_ref[b] = (qk[:, :g] * (HEAD_DIM ** -0.5)).astype(BF16)
        k_ref[b] = qk[:, g:].astype(BF16)
        v_ref[b] = p[:, 4 * g:5 * g].astype(BF16)
        rq_ref[b] = p[:, 5 * g:6 * g].astype(BF16)
        rk_ref[b] = p[:, 6 * g:7 * g].astype(BF16)
        rv_ref[b] = p[:, 7 * g:8 * g].astype(BF16)
        rg_ref[b] = p[:, 8 * g:9 * g].astype(BF16)


def _mod_spec():
    return pl.BlockSpec((8, 1, D_MODEL), lambda i: (0, 0, 0))


def _x_specs(B, split_input, n_lat_tiles):
    if split_input:
        return [
            pl.BlockSpec((B, TILE, D_MODEL), lambda i: (0, jnp.minimum(i, n_lat_tiles - 1), 0)),
            pl.BlockSpec((B, TILE, D_MODEL), lambda i: (0, 0, 0)),
        ]
    return [pl.BlockSpec((B, TILE, D_MODEL), lambda i: (0, i, 0))]


def _projection(xs, sc, sh, w_ext, cos, sin, *, B, S, n_lat_tiles):
    split_input = len(xs) == 2
    nt = S // TILE
    g = MIX_GROUP_W
    out_spec = pl.BlockSpec((B, TILE, g), lambda i: (0, i, 0))
    out_shape = jax.ShapeDtypeStruct((B, S, g), BF16)
    return pl.pallas_call(
        functools.partial(_proj_kernel, nb=B, split_input=split_input, n_lat_tiles=n_lat_tiles),
        grid=(nt,),
        in_specs=_x_specs(B, split_input, n_lat_tiles) + [
            _mod_spec(),
            _mod_spec(),
            pl.BlockSpec(w_ext.shape, lambda i: (0, 0)),
            pl.BlockSpec((TILE, 128), lambda i: (i, 0)),
            pl.BlockSpec((TILE, 128), lambda i: (i, 0)),
        ],
        out_specs=[out_spec] * 9,
        out_shape=[out_shape] * 9,
        compiler_params=_cparams(vmem=VMEM_LIMIT),
        name="ln_mod_in_proj",
    )(*xs, sc, sh, w_ext, cos, sin)


def _channel_dft_matrix():
    c = np.arange(FOURIER_GROUP_W)
    ang = 2.0 * np.pi * np.outer(c, c) / FOURIER_GROUP_W
    eye = np.eye(MIX_GROUP_W // FOURIER_GROUP_W)
    cs = np.concatenate([np.kron(eye, np.cos(ang)), -np.kron(eye, np.sin(ang))], axis=1)
    return jnp.asarray(cs, dtype=BF16)


def _seq_dft_stage_a(n1):
    n2 = FFT_RADIX2
    n = n1 * n2
    k1 = np.arange(n1)[None, :, None]
    i1 = np.arange(n1)[None, None, :]
    i2 = np.arange(n2)[:, None, None]
    th = 2.0 * np.pi * ((k1 * (n2 * i1 + i2)) % n) / n
    wr, wi = np.cos(th), -np.sin(th)
    wa = np.concatenate([np.concatenate([wr, -wi], axis=2), np.concatenate([wi, wr], axis=2)], axis=1)
    return jnp.asarray(wa, dtype=BF16)


def _seq_dft_stage_b():
    n2 = FFT_RADIX2
    k2 = np.arange(n2)
    ph = 2.0 * np.pi * (np.outer(k2, k2) % n2) / n2
    return jnp.asarray(np.concatenate([np.cos(ph), np.sin(ph)], axis=1), dtype=BF16)


def _dense_dft(length):
    k = np.arange(length)
    th = 2.0 * np.pi * (np.outer(k, k) % length) / length
    return jnp.asarray(np.concatenate([np.cos(th), np.sin(th)], axis=1), dtype=BF16)


FFT_A_COLS = 8
FFT_B_COLS = 16


def _fft_a_kernel(a_ref, cs_ref, wa_ref, t_ref, *, nb, n1):
    g = MIX_GROUP_W
    for j in range(FFT_A_COLS):
        for b in range(nb):
            p = _dot(a_ref[b, :, g * j:g * (j + 1)], cs_ref[...])
            pp = jnp.concatenate([p[:, :g], p[:, g:]], axis=0).astype(BF16)
            t = _dot(wa_ref[j], pp)
            t_ref[b, :, 2 * g * j:2 * g * (j + 1)] = jnp.concatenate([t[:n1], t[n1:]], axis=-1).astype(BF16)


def _fft_b_kernel(t_ref, vb_ref, y_ref, *, nb, scale):
    g = MIX_GROUP_W
    n2 = FFT_RADIX2
    for j in range(FFT_B_COLS):
        for b in range(nb):
            t = t_ref[b, n2 * j:n2 * (j + 1), :]
            tt = jnp.concatenate([t[:, :g], t[:, g:]], axis=0)
            y = _dot(vb_ref[...], tt) * scale
            y_ref[b, 0:n2, g * j:g * (j + 1)] = y.astype(BF16)
            y_ref[b, n2:, g * j:g * (j + 1)] = jnp.zeros((y_ref.shape[1] - n2, g), BF16)


def _fft_ctx_kernel(a_ref, cs_ref, wc_ref, yin_ref, y_ref, *, scale):
    del yin_ref
    g = MIX_GROUP_W
    p = _dot(a_ref[...], cs_ref[...])
    pp = jnp.concatenate([p[:, :g], p[:, g:]], axis=0).astype(BF16)
    y_ref[...] = (_dot(wc_ref[...], pp) * scale).astype(BF16)


def _fourier_mix(a, *, B, N, L, with_ctx):
    S = N + L
    g = MIX_GROUP_W
    n2 = FFT_RADIX2
    n1 = N // n2
    cs = _channel_dft_matrix()
    a2 = a.reshape(B, S // n2, n2 * g)
    t = pl.pallas_call(
        functools.partial(_fft_a_kernel, nb=B, n1=n1),
        grid=(n2 // FFT_A_COLS,),
        in_specs=[
            pl.BlockSpec((B, n1, FFT_A_COLS * g), lambda j: (0, 0, j)),
            pl.BlockSpec(cs.shape, lambda j: (0, 0)),
            pl.BlockSpec((FFT_A_COLS, 2 * n1, 2 * n1), lambda j: (j, 0, 0)),
        ],
        out_specs=pl.BlockSpec((B, n1, FFT_A_COLS * 2 * g), lambda j: (0, 0, j)),
        out_shape=jax.ShapeDtypeStruct((B, n1, n2 * 2 * g), BF16),
        name="fourier_seq_stage_a",
    )(a2, cs, _seq_dft_stage_a(n1))
    t = t.reshape(B, n1 * n2, 2 * g)
    rows = S // n1
    y = pl.pallas_call(
        functools.partial(_fft_b_kernel, nb=B, scale=float((N * FOURIER_GROUP_W) ** -0.5)),
        grid=(n1 // FFT_B_COLS,),
        in_specs=[
            pl.BlockSpec((B, FFT_B_COLS * n2, 2 * g), lambda k: (0, k, 0)),
            pl.BlockSpec((n2, 2 * n2), lambda k: (0, 0)),
        ],
        out_specs=pl.BlockSpec((B, rows, FFT_B_COLS * g), lambda k: (0, 0, k)),
        out_shape=jax.ShapeDtypeStruct((B, rows, n1 * g), BF16),
        name="fourier_seq_stage_b",
    )(t, _seq_dft_stage_b())
    y = y.reshape(B, S, g)
    if not with_ctx:
        return y
    ctx_tile = N // TILE
    return pl.pallas_call(
        functools.partial(_fft_ctx_kernel, scale=float((L * FOURIER_GROUP_W) ** -0.5)),
        grid=(B,),
        in_specs=[
            pl.BlockSpec((None, TILE, g), lambda b: (b, ctx_tile, 0)),
            pl.BlockSpec(cs.shape, lambda b: (0, 0)),
            pl.BlockSpec((L, 2 * L), lambda b: (0, 0)),
            pl.BlockSpec(memory_space=pl.ANY),
        ],
        out_specs=pl.BlockSpec((None, TILE, g), lambda b: (b, ctx_tile, 0)),
        out_shape=jax.ShapeDtypeStruct((B, S, g), BF16),
        input_output_aliases={3: 0},
        name="fourier_ctx",
    )(a, cs, _dense_dft(L), y)


POOL_HALO = 16


def _pool_bands():
    t = np.arange(TILE)[:, None]
    s = np.arange(TILE + 2 * POOL_HALO)[None, :] - POOL_HALO
    bands = [((s - t >= -(w // 2)) & (s - t < w // 2)).astype(np.float32) for w in POOL_WINDOWS]
    return jnp.asarray(np.stack(bands), dtype=BF16)


def _pool_kernel(prev_ref, x_ref, next_ref, band_ref, pw_ref, ps_ref, o_ref, *, nb, n_lat, s_tot):
    i = pl.program_id(0)
    t0 = i * TILE
    is_ctx = t0 >= n_lat
    r0 = jnp.where(is_ctx, n_lat, 0)
    r1 = jnp.where(is_ctx, s_tot, n_lat)
    kw = TILE + 2 * POOL_HALO
    s_pos = t0 - POOL_HALO + lax.broadcasted_iota(I32, (kw, 1), 0)
    inside = jnp.where(s_pos >= r0, jnp.where(s_pos < r1, 1.0, 0.0), 0.0).astype(BF16)
    xc = jnp.concatenate([jnp.concatenate([prev_ref[b], x_ref[b], next_ref[b]], axis=0) * inside
                          for b in range(nb)], axis=-1)
    g = MIX_GROUP_W
    t_col = t0 + lax.broadcasted_iota(I32, (TILE, 1), 0)
    group = (lax.broadcasted_iota(I32, (TILE, nb * g), 1) // FOURIER_GROUP_W) % len(POOL_WINDOWS)
    acc = jnp.zeros((TILE, nb * g), F32)
    for gi, w in enumerate(POOL_WINDOWS):
        lo = jnp.maximum(t_col - w // 2, r0)
        hi = jnp.minimum(t_col + w // 2, r1)
        mean = _dot(band_ref[gi], xc) / (hi - lo).astype(F32)
        acc = jnp.where(group == gi, mean, acc)
    for b in range(nb):
        pooled = acc[:, g * b:g * (b + 1)] - x_ref[b].astype(F32)
        o_ref[b] = (_dot(pooled.astype(BF16), pw_ref[...]) * ps_ref[...]).astype(BF16)


def _pool_mix(bp, pw_bd, pscale, *, B, N, L):
    S = N + L
    nt = S // TILE
    g = MIX_GROUP_W
    hb = TILE // POOL_HALO
    last = S // POOL_HALO - 1
    return pl.pallas_call(
        functools.partial(_pool_kernel, nb=B, n_lat=N, s_tot=S),
        grid=(nt,),
        in_specs=[
            pl.BlockSpec((B, POOL_HALO, g), lambda i: (0, jnp.maximum(i * hb - 1, 0), 0)),
            pl.BlockSpec((B, TILE, g), lambda i: (0, i, 0)),
            pl.BlockSpec((B, POOL_HALO, g), lambda i: (0, jnp.minimum((i + 1) * hb, last), 0)),
            pl.BlockSpec((len(POOL_WINDOWS), TILE, TILE + 2 * POOL_HALO), lambda i: (0, 0, 0)),
            pl.BlockSpec((g, g), lambda i: (0, 0)),
            pl.BlockSpec((1, g), lambda i: (0, 0)),
        ],
        out_specs=pl.BlockSpec((B, TILE, g), lambda i: (0, i, 0)),
        out_shape=jax.ShapeDtypeStruct((B, S, g), BF16),
        name="pool_mix",
    )(bp, bp, bp, _pool_bands(), pw_bd, pscale)


def _attn_window_bias(n_ctx):
    r = (np.arange(2 * TILE) % TILE)[:, None]
    c = np.arange(TILE + 2 * WINDOW)[None, :] - WINDOW
    band = np.where(np.abs(r - c) <= WINDOW, 0.0, NEG_INF)
    return jnp.asarray(np.concatenate([band, np.zeros((2 * TILE, n_ctx))], axis=1), dtype=F32)


def _attn_kernel(sink_ref, bias_ref, q_ref, kp_ref, kc_ref, kn_ref, kx_ref, vp_ref, vc_ref, vn_ref, vx_ref, o_ref,
                 *, nb, n_lat):
    i = pl.program_id(0)
    t0 = i * TILE
    nband = TILE + 2 * WINDOW
    nkeys = nband + kx_ref.shape[1]
    col = lax.broadcasted_iota(I32, (1, nkeys), 1)
    kpos = t0 - WINDOW + col
    valid = (col >= nband) | ((kpos >= 0) & (kpos < n_lat) & (t0 < n_lat))
    bias = bias_ref[...]
    lane = lax.broadcasted_iota(I32, (TILE, 2 * HEAD_DIM), 1)
    low = lane < HEAD_DIM
    keep_low = jnp.where(low, 1.0, 0.0).astype(BF16)
    keep_high = jnp.where(low, 0.0, 1.0).astype(BF16)
    row1 = lax.broadcasted_iota(I32, (2 * TILE, 1), 0)
    for b in range(nb):
        outs = []
        for h in range(2):
            sl = slice(2 * HEAD_DIM * h, 2 * HEAD_DIM * (h + 1))
            qh = q_ref[b, :, sl]
            qs = jnp.concatenate([qh * keep_low, qh * keep_high], axis=0)
            kh = jnp.concatenate([kp_ref[b, :, sl], kc_ref[b, :, sl], kn_ref[b, :, sl], kx_ref[b, :, sl]], axis=0)
            vh = jnp.concatenate([vp_ref[b, :, sl], vc_ref[b, :, sl], vn_ref[b, :, sl], vx_ref[b, :, sl]], axis=0)
            s = jnp.where(valid, _dot_nt(qs, kh) + bias, NEG_INF)
            sink = jnp.where(row1 < TILE, sink_ref[2 * h], sink_ref[2 * h + 1])
            m = jnp.maximum(jnp.max(s, axis=-1, keepdims=True), sink)
            p = jnp.exp(s - m)
            den = jnp.sum(p, axis=-1, keepdims=True) + jnp.exp(sink - m)
            o = _dot(p.astype(BF16), vh) / den
            outs.append(jnp.where(low, o[:TILE], o[TILE:]))
        o_ref[b] = jnp.concatenate(outs, axis=-1).astype(BF16)


def _attention(sinks, q, k2, v2, *, B, N, L):
    S = N + L
    nt = S // TILE
    g = MIX_GROUP_W
    hb = TILE // WINDOW
    last = N // WINDOW - 1
    ctx_tile = N // TILE
    prev = pl.BlockSpec((B, WINDOW, g), lambda i: (0, jnp.clip(i * hb - 1, 0, last), 0))
    cur = pl.BlockSpec((B, TILE, g), lambda i: (0, i, 0))
    nxt = pl.BlockSpec((B, WINDOW, g), lambda i: (0, jnp.minimum((i + 1) * hb, last), 0))
    ctx = pl.BlockSpec((B, L, g), lambda i: (0, ctx_tile, 0))
    return pl.pallas_call(
        functools.partial(_attn_kernel, nb=B, n_lat=N),
        grid=(nt,),
        in_specs=[pl.BlockSpec(memory_space=pltpu.SMEM),
                  pl.BlockSpec((2 * TILE, TILE + 2 * WINDOW + L), lambda i: (0, 0)),
                  cur, prev, cur, nxt, ctx, prev, cur, nxt, ctx],
        out_specs=pl.BlockSpec((B, TILE, g), lambda i: (0, i, 0)),
        out_shape=jax.ShapeDtypeStruct((B, S, g), BF16),
        compiler_params=_cparams(vmem=VMEM_LIMIT),
        name="banded_attention",
    )(sinks, _attn_window_bias(L), q, k2, k2, k2, k2, v2, v2, v2, v2)


def _ret_kernel(*refs, reverse, nb):
    if reverse:
        dec_ref, q_ref, k_ref, v_ref, of_ref, g_ref, o_ref, r_scr = refs
    else:
        dec_ref, q_ref, k_ref, v_ref, o_ref, r_scr = refs
    c = RET_CHUNK
    w = MIX_GROUP_W
    dh = w // RET_HEADS

    @pl.when(pl.program_id(0) == 0)
    def _():
        r_scr[...] = jnp.zeros(r_scr.shape, F32)

    lg = -jnp.exp(dec_ref[...])
    idx = lax.broadcasted_iota(I32, (c, 1), 0).astype(F32)
    jdx = lax.broadcasted_iota(I32, (1, c), 1).astype(F32)
    if reverse:
        idx, jdx = (c - 1.0) - idx, (c - 1.0) - jdx
    diff = idx - jdx
    dmask_all = jnp.concatenate(
        [jnp.where(diff >= 0, jnp.exp(jnp.maximum(diff, 0.0) * lg[:, dh * h:dh * h + 1]), 0.0)
         for h in range(RET_HEADS)], axis=1)
    xi = jnp.exp((idx + 1.0) * lg)
    zeta = jnp.exp((c - 1.0 - idx) * lg)
    chunk_decay = jnp.exp(float(c) * lg)
    lane_head = lax.broadcasted_iota(I32, (1, w), 1) // dh
    keep_head = [jnp.where(lane_head == h, 1.0, 0.0).astype(BF16) for h in range(RET_HEADS)]
    same_head = (lax.broadcasted_iota(I32, (w, w), 0) // dh) == (lax.broadcasted_iota(I32, (w, w), 1) // dh)

    for b in range(nb):
        q = q_ref[b]
        k = k_ref[b] * jnp.asarray(dh ** -0.5, BF16)
        v = v_ref[b]
        state = r_scr[b]
        k_heads = jnp.concatenate([k * keep_head[h] for h in range(RET_HEADS)], axis=0)
        v_heads = jnp.concatenate([v * keep_head[h] for h in range(RET_HEADS)], axis=0)
        s = _dot_nt(q, k_heads) * dmask_all
        o = _dot(q, state.astype(BF16)) * xi + _dot(s.astype(BF16), v_heads)
        kv = _dot_tn(k, (v.astype(F32) * zeta).astype(BF16))
        r_scr[b] = state * chunk_decay + jnp.where(same_head, kv, 0.0)
        if not reverse:
            o_ref[b] = o
        else:
            tot = o + of_ref[b]
            mu = jnp.zeros_like(tot)
            for h in range(RET_HEADS):
                mh = lane_head == h
                mu = jnp.where(mh, jnp.sum(jnp.where(mh, tot, 0.0), axis=-1, keepdims=True) / dh, mu)
            cen = tot - mu
            var = jnp.zeros_like(tot)
            for h in range(RET_HEADS):
                mh = lane_head == h
                var = jnp.where(mh, jnp.sum(jnp.where(mh, cen * cen, 0.0), axis=-1, keepdims=True) / dh, var)
            gate = g_ref[b].astype(F32)
            o_ref[b] = (gate * jax.nn.sigmoid(gate) * (cen * lax.rsqrt(var + LN_EPS))).astype(BF16)


def _retention(dec, rq, rk, rv, rg, *, B, N, L):
    S = N + L
    c = RET_CHUNK
    w = MIX_GROUP_W
    nc_lat, nc_ctx = N // c, L // c
    nc = nc_lat + nc_ctx

    def fwd_blk(j):
        return jnp.where(j < nc_ctx, nc_lat + j, j - nc_ctx)

    def bwd_blk(j):
        return nc - 1 - j

    def call(blk, reverse, extra):
        spec = pl.BlockSpec((B, c, w), lambda j: (0, blk(j), 0))
        dspec = pl.BlockSpec((1, w), lambda j: (0, 0))
        return pl.pallas_call(
            functools.partial(_ret_kernel, reverse=reverse, nb=B),
            grid=(nc,),
            in_specs=[dspec] + [spec] * (3 + len(extra)),
            out_specs=spec,
            out_shape=jax.ShapeDtypeStruct((B, S, w), BF16 if reverse else F32),
            scratch_shapes=[pltpu.VMEM((B, w, w), F32)],
            compiler_params=_cparams(sem=("arbitrary",), vmem=VMEM_LIMIT),
            name="retention_bwd" if reverse else "retention_fwd",
        )(dec[1:2] if reverse else dec[0:1], rq, rk, rv, *extra)

    o_f = call(fwd_blk, False, ())
    return call(bwd_blk, True, (o_f, rg))


def _merge_kernel(*refs, nb, split_input, n_lat_tiles, alpha):
    for b in range(nb):
        x, rest = _tile_inputs(refs, b, split_input, n_lat_tiles)
        (yf_ref, yp_ref, ya_ref, yr_ref, g1_ref, sc_ref, sh_ref, lg_ref, lb_ref, wo_ref, rw_ref, rb_ref,
         x1_ref, h2_ref) = rest[:14]
        row = _mod_row(b, n_lat_tiles, nb)
        ycat = jnp.concatenate([yf_ref[b], yp_ref[b], ya_ref[b], yr_ref[b]], axis=-1)
        y = _dot(ycat, wo_ref[...])
        x1 = _normalize(alpha * x + g1_ref[row] * y) * lg_ref[...] + lb_ref[...]
        x1_ref[b] = x1
        h2 = _normalize(x1) * (1.0 + sc_ref[row]) + sh_ref[row]
        _to_token_tiles(h2_ref.at[b], h2)
        logits = _dot_nt(rw_ref[...], h2.astype(BF16)) + rb_ref[...]
        _route(logits, rest[14 + b], rest[14 + nb + b])


def _route(logits, idx_ref, gate_ref):
    eid = lax.broadcasted_iota(I32, logits.shape, 0)
    orow = lax.broadcasted_iota(I32, (8, logits.shape[1]), 0)
    idx_out = jnp.zeros((8, logits.shape[1]), I32)
    val_out = jnp.zeros((8, logits.shape[1]), F32)
    top = None
    den = None
    for kk in range(TOP_K):
        m = jnp.max(logits, axis=0, keepdims=True)
        sel = jnp.min(jnp.where(logits == m, eid, N_EXPERTS), axis=0, keepdims=True)
        logits = jnp.where(eid == sel, -jnp.inf, logits)
        if kk == 0:
            top = m
        e = jnp.exp(m - top)
        den = e if kk == 0 else den + e
        idx_out = jnp.where(orow == kk, sel, idx_out)
        val_out = jnp.where(orow == kk, e, val_out)
    idx_ref[...] = idx_out
    gate_ref[...] = val_out / den


def _merge(xs, ys, g1, sc2, sh2, ln_g, ln_b, w_out, rw_t, rb, *, B, S, n_lat_tiles, alpha):
    split_input = len(xs) == 2
    nt = S // TILE
    g = MIX_GROUP_W
    r = ROWS_PER_TOKEN
    yspec = pl.BlockSpec((B, TILE, g), lambda i: (0, i, 0))
    vec = pl.BlockSpec((1, D_MODEL), lambda i: (0, 0))
    route_spec = pl.BlockSpec((8, TILE), lambda i: (0, i))
    outs = pl.pallas_call(
        functools.partial(_merge_kernel, nb=B, split_input=split_input, n_lat_tiles=n_lat_tiles, alpha=alpha),
        grid=(nt,),
        in_specs=_x_specs(B, split_input, n_lat_tiles) + [yspec] * 4 + [
            _mod_spec(), _mod_spec(), _mod_spec(), vec, vec,
            pl.BlockSpec((D_MODEL, D_MODEL), lambda i: (0, 0)),
            pl.BlockSpec((N_EXPERTS, D_MODEL), lambda i: (0, 0)),
            pl.BlockSpec((N_EXPERTS, 1), lambda i: (0, 0)),
        ],
        out_specs=[
            pl.BlockSpec((B, TILE, D_MODEL), lambda i: (0, i, 0)),
            pl.BlockSpec((B, TILE * r, 128), lambda i: (0, i, 0)),
        ] + [route_spec] * (2 * B),
        out_shape=[
            jax.ShapeDtypeStruct((B, S, D_MODEL), F32),
            jax.ShapeDtypeStruct((B, S * r, 128), F32),
        ] + [jax.ShapeDtypeStruct((8, S), I32)] * B + [jax.ShapeDtypeStruct((8, S), F32)] * B,
        compiler_params=_cparams(vmem=VMEM_LIMIT),
        name="out_proj_norm_router",
    )(*xs, *ys, g1, sc2, sh2, ln_g, ln_b, w_out, rw_t, rb)
    x1, h2t = outs[:2]
    idx = jnp.concatenate(outs[2:2 + B], axis=1)
    gates = jnp.concatenate(outs[2 + B:], axis=1)
    return x1, h2t.reshape(B * S * r, 128), idx, gates


def _rank_kernel(idx_ref, lpos_ref, ntab_ref, cbtab_ref, cnt_ref, carry_scr):
    tt = idx_ref.shape[1]

    @pl.when(pl.program_id(0) == 0)
    def _():
        carry_scr[...] = jnp.zeros(carry_scr.shape, F32)

    eid = lax.broadcasted_iota(I32, (N_EXPERTS, tt), 0)
    before = jnp.where(lax.broadcasted_iota(I32, (tt, tt), 0) < lax.broadcasted_iota(I32, (tt, tt), 1),
                       1.0, 0.0).astype(BF16)
    hits, cums, pres = [], [], []
    n_tile = jnp.zeros((N_EXPERTS, 1), F32)
    for kk in range(TOP_K):
        hit = eid == idx_ref[kk:kk + 1, :]
        hits.append(hit)
        cums.append(_dot(jnp.where(hit, 1.0, 0.0).astype(BF16), before))
        pres.append(n_tile)
        n_tile = n_tile + jnp.sum(jnp.where(hit, 1.0, 0.0), axis=1, keepdims=True)
    n16 = ((n_tile.astype(I32) + (RUN_ALIGN - 1)) >> RUN_SHIFT) << RUN_SHIFT
    lower = jnp.where(lax.broadcasted_iota(I32, (N_EXPERTS, N_EXPERTS), 1)
                      < lax.broadcasted_iota(I32, (N_EXPERTS, N_EXPERTS), 0), 1.0, 0.0).astype(BF16)
    hi = jnp.broadcast_to((n16 >> 8).astype(F32), (N_EXPERTS, 128)).astype(BF16)
    lo = jnp.broadcast_to((n16 & 255).astype(F32), (N_EXPERTS, 128)).astype(BF16)
    lstart = (256.0 * _dot(lower, hi) + _dot(lower, lo))[:, 0:1]
    orow = lax.broadcasted_iota(I32, (8, tt), 0)
    out = jnp.zeros((8, tt), I32)
    for kk in range(TOP_K):
        slot = jnp.sum(jnp.where(hits[kk], lstart + pres[kk] + cums[kk], 0.0), axis=0, keepdims=True)
        out = jnp.where(orow == kk, slot.astype(I32) * ROWS_PER_TOKEN, out)
    lpos_ref[...] = out
    carry = carry_scr[...]
    ntab_ref[...] = jnp.broadcast_to(n_tile, ntab_ref.shape).astype(I32)
    cbtab_ref[...] = carry.astype(I32)
    carry = carry + n_tile
    carry_scr[...] = carry
    cnt_ref[...] = carry


def _rank(idx, *, T):
    tt = MOE_TOK_TILE
    nt = T // tt
    tab = pl.BlockSpec((None, N_EXPERTS, 128), lambda i: (i, 0, 0))
    return pl.pallas_call(
        _rank_kernel,
        grid=(nt,),
        in_specs=[pl.BlockSpec((8, tt), lambda i: (0, i))],
        out_specs=[pl.BlockSpec((8, tt), lambda i: (0, i)), tab, tab,
                   pl.BlockSpec((N_EXPERTS, 128), lambda i: (0, 0))],
        out_shape=[jax.ShapeDtypeStruct((8, T), I32), jax.ShapeDtypeStruct((nt, N_EXPERTS, 128), I32),
                   jax.ShapeDtypeStruct((nt, N_EXPERTS, 128), I32), jax.ShapeDtypeStruct((N_EXPERTS, 128), F32)],
        scratch_shapes=[pltpu.VMEM((N_EXPERTS, 128), F32)],
        compiler_params=_cparams(sem=("arbitrary",)),
        name="moe_rank",
    )(idx)


def _run_chunk(src_ref, src_tok, dst_ref, dst_tok, sem):
    return pltpu.make_async_copy(src_ref.at[_token_rows(src_tok, RUN_ALIGN), :],
                                 dst_ref.at[_token_rows(dst_tok, RUN_ALIGN), :], sem)


def _wait_chunks(src_ref, dst_ref, sem, count):
    def body(_, carry):
        _run_chunk(src_ref, 0, dst_ref, 0, sem).wait()
        return carry
    lax.fori_loop(0, count, body, 0)


def _copy_runs(step, pstart_ref, ntab_ref, cbtab_ref, chunk):
    def per_expert(e, carry):
        src0, total = carry
        nch = (ntab_ref[step * N_EXPERTS + e] + (RUN_ALIGN - 1)) >> RUN_SHIFT
        dst0 = pstart_ref[e] + cbtab_ref[step * N_EXPERTS + e]

        def one(c, cc):
            chunk(src0 + c * RUN_ALIGN, dst0 + c * RUN_ALIGN).start()
            return cc
        lax.fori_loop(0, nch, one, 0)
        return src0 + nch * RUN_ALIGN, total + nch
    return lax.fori_loop(0, N_EXPERTS, per_expert, (0, 0))[1]


def _dispatch_kernel(pstart_ref, cnt_ref, pad_ref, ntab_ref, cbtab_ref, lpos_ref, h_ref, xs_ref,
                     stage, zero_scr, sems, sent):
    step = pl.program_id(0)

    @pl.when(step == 0)
    def _():
        sem = sems.at[0]
        zero_scr[...] = jnp.zeros(zero_scr.shape, F32)

        def per_expert(e, total):
            first = pstart_ref[e] + ((cnt_ref[e] >> RUN_SHIFT) << RUN_SHIFT)
            n_fill = (pstart_ref[e] + pad_ref[e] - first) >> RUN_SHIFT

            def fill(c, carry):
                _run_chunk(zero_scr, 0, xs_ref, first + c * RUN_ALIGN, sem).start()
                return carry
            lax.fori_loop(0, n_fill, fill, 0)
            return total + n_fill
        n_total = lax.fori_loop(0, N_EXPERTS, per_expert, 0)
        used = pstart_ref[N_EXPERTS - 1] + pad_ref[N_EXPERTS - 1]
        n_tail = (xs_ref.shape[0] // ROWS_PER_TOKEN - used) >> RUN_SHIFT

        def fill_tail(c, carry):
            _run_chunk(zero_scr, 0, xs_ref, used + c * RUN_ALIGN, sem).start()
            return carry
        lax.fori_loop(0, n_tail, fill_tail, 0)
        _wait_chunks(zero_scr, xs_ref, sem, n_total + n_tail)
        sent[0] = 0
        sent[1] = 0

    slot = step % 2
    buf = stage.at[slot]
    sem = sems.at[slot]

    def zero_last_group(e, src0):
        n_up = ((ntab_ref[step * N_EXPERTS + e] + (RUN_ALIGN - 1)) >> RUN_SHIFT) << RUN_SHIFT
        buf[_token_rows(jnp.maximum(src0 + n_up - RUN_ALIGN, 0), RUN_ALIGN), :] = jnp.zeros((CHUNK_ROWS, 128), F32)
        return src0 + n_up
    lax.fori_loop(0, N_EXPERTS, zero_last_group, 0)
    tt = MOE_TOK_TILE
    for kk in range(TOP_K):
        def place(t, carry, kk=kk):
            row = pl.multiple_of(lpos_ref[kk * tt + t], ROWS_PER_TOKEN)
            buf[pl.ds(row, ROWS_PER_TOKEN), :] = h_ref[_token_rows(t), :]
            return carry
        lax.fori_loop(0, tt, place, 0, unroll=16)
    _wait_chunks(stage.at[1 - slot], xs_ref, sems.at[1 - slot], sent[1 - slot])
    sent[1 - slot] = 0
    sent[slot] = _copy_runs(step, pstart_ref, ntab_ref, cbtab_ref,
                            lambda s, d: _run_chunk(buf, s, xs_ref, d, sem))

    @pl.when(step == pl.num_programs(0) - 1)
    def _():
        _wait_chunks(buf, xs_ref, sem, sent[slot])


def _smem_tile_spec(tt):
    return pl.BlockSpec((8 * tt,), lambda i, *_: (i,), memory_space=pltpu.SMEM)


def _tile_major(table):
    tt = MOE_TOK_TILE
    return table.reshape(table.shape[0], -1, tt).transpose(1, 0, 2).reshape(-1)


def _dispatch(pstart, cnt, padded, ntab, cbtab, lpos, h2t, *, T, P):
    tt = MOE_TOK_TILE
    r = ROWS_PER_TOKEN
    return pl.pallas_call(
        _dispatch_kernel,
        grid_spec=pltpu.PrefetchScalarGridSpec(
            num_scalar_prefetch=5,
            grid=(T // tt,),
            in_specs=[_smem_tile_spec(tt), pl.BlockSpec((tt * r, 128), lambda i, *_: (i, 0))],
            out_specs=pl.BlockSpec(memory_space=pl.ANY),
            scratch_shapes=[pltpu.VMEM((2, STAGE_TOKENS * r, 128), F32), pltpu.VMEM((CHUNK_ROWS, 128), F32),
                            pltpu.SemaphoreType.DMA((2,)), pltpu.SMEM((2,), I32)],
        ),
        out_shape=jax.ShapeDtypeStruct((P * r, 128), F32),
        compiler_params=_cparams(sem=("arbitrary",), vmem=EXPERT_VMEM_LIMIT),
        name="moe_dispatch",
    )(pstart, cnt, padded, ntab, cbtab, _tile_major(lpos), h2t)


def _collect_kernel(pstart_ref, ntab_ref, cbtab_ref, lpos_ref, gate_ref, ys_ref, f_ref, stage, sems, sent):
    step = pl.program_id(0)
    slot = step % 2

    def fetch(tile, b):
        sent[b] = _copy_runs(tile, pstart_ref, ntab_ref, cbtab_ref,
                             lambda s, d: _run_chunk(ys_ref, d, stage.at[b], s, sems.at[b]))

    @pl.when(step == 0)
    def _():
        fetch(0, 0)

    @pl.when(step + 1 < pl.num_programs(0))
    def _():
        fetch(step + 1, 1 - slot)

    buf = stage.at[slot]
    _wait_chunks(ys_ref, buf, sems.at[slot], sent[slot])

    tt = MOE_TOK_TILE

    def combine(t, carry):
        acc = None
        for kk in range(TOP_K):
            row = pl.multiple_of(lpos_ref[kk * tt + t], ROWS_PER_TOKEN)
            term = gate_ref[kk * tt + t] * buf[pl.ds(row, ROWS_PER_TOKEN), :]
            acc = term if acc is None else acc + term
        f_ref[_token_rows(t), :] = acc
        return carry
    lax.fori_loop(0, tt, combine, 0, unroll=8)


def _collect(pstart, ntab, cbtab, lpos, gates, ys, *, T):
    tt = MOE_TOK_TILE
    r = ROWS_PER_TOKEN
    return pl.pallas_call(
        _collect_kernel,
        grid_spec=pltpu.PrefetchScalarGridSpec(
            num_scalar_prefetch=3,
            grid=(T // tt,),
            in_specs=[_smem_tile_spec(tt), _smem_tile_spec(tt), pl.BlockSpec(memory_space=pl.ANY)],
            out_specs=pl.BlockSpec((tt * r, 128), lambda i, *_: (i, 0)),
            scratch_shapes=[pltpu.VMEM((2, STAGE_TOKENS * r, 128), F32), pltpu.SemaphoreType.DMA((2,)),
                            pltpu.SMEM((2,), I32)],
        ),
        out_shape=jax.ShapeDtypeStruct((T * r, 128), F32),
        compiler_params=_cparams(sem=("arbitrary",), vmem=EXPERT_VMEM_LIMIT),
        name="moe_collect",
    )(pstart, ntab, cbtab, _tile_major(lpos), _tile_major(gates), ys)


def _deinterleave_matrix():
    m = np.zeros((256, 256), np.float32)
    j = np.arange(128)
    m[2 * j, j] = 1.0
    m[2 * j + 1, 128 + j] = 1.0
    return jnp.asarray(m, dtype=BF16)


def _expert_kernel(be_ref, nu_ref, x_ref, w1_hbm, b1_ref, w2_hbm, b2_ref, perm_ref, y_ref,
                   w1_buf, w2_buf, w1_scr, w2_scr, sem1, sem2, *, layer):
    i = pl.program_id(0)

    def weight_copies(e):
        slot = e % 2
        return (pltpu.make_async_copy(w1_hbm.at[layer, e], w1_buf.at[slot], sem1.at[slot]),
                pltpu.make_async_copy(w2_hbm.at[layer, e], w2_buf.at[slot], sem2.at[slot]))

    @pl.when(i >= nu_ref[0])
    def _():
        y_ref[...] = jnp.zeros(y_ref.shape, F32)

    @pl.when(i < nu_ref[0])
    def _():
        e = be_ref[i]

        @pl.when((i == 0) | (e != be_ref[jnp.maximum(i - 1, 0)]))
        def _():
            @pl.when(i == 0)
            def _():
                for cp in weight_copies(e):
                    cp.start()
            for cp in weight_copies(e):
                cp.wait()

            @pl.when(e + 1 < N_EXPERTS)
            def _():
                for cp in weight_copies(e + 1):
                    cp.start()
            w1_ref = w1_buf.at[e % 2]
            half = w1_scr.shape[1] // 2
            for m in range(w1_scr.shape[1] // 256):
                r = _dot(w1_ref[:, 256 * m:256 * (m + 1)].astype(BF16), perm_ref[...])
                w1_scr[:, 128 * m:128 * (m + 1)] = r[:, :128].astype(BF16)
                w1_scr[:, half + 128 * m:half + 128 * (m + 1)] = r[:, 128:].astype(BF16)
            w2_scr[...] = w2_buf[e % 2].astype(BF16)

        x = _from_token_tiles(x_ref, MOE_BLK).astype(BF16)
        u = _dot(x, w1_scr[...]) + b1_ref[...]
        half = u.shape[1] // 2
        glu = jnp.minimum(u[:, :half], SWIGLU_LIMIT)
        lin = jnp.clip(u[:, half:], -SWIGLU_LIMIT, SWIGLU_LIMIT)
        act = glu * jax.nn.sigmoid(SWIGLU_ALPHA * glu) * (lin + 1.0)
        _to_token_tiles(y_ref, _dot(act.astype(BF16), w2_scr[...]) + b2_ref[...])


def _experts(blk_e, n_used, xs, w1, b1, w2, b2, *, P, layer):
    nblk = P // MOE_BLK
    r = ROWS_PER_TOKEN
    d, two_f = w1.shape[2:]
    f = w2.shape[2]
    rows = lambda i, be, nu: (jnp.minimum(i, nu[0] - 1), 0)
    return pl.pallas_call(
        functools.partial(_expert_kernel, layer=layer),
        grid_spec=pltpu.PrefetchScalarGridSpec(
            num_scalar_prefetch=2,
            grid=(nblk,),
            in_specs=[
                pl.BlockSpec((MOE_BLK * r, 128), rows),
                pl.BlockSpec(memory_space=pl.ANY),
                pl.BlockSpec((None, 1, two_f), lambda i, be, nu: (be[i], 0, 0)),
                pl.BlockSpec(memory_space=pl.ANY),
                pl.BlockSpec((None, 1, d), lambda i, be, nu: (be[i], 0, 0)),
                pl.BlockSpec((256, 256), lambda i, be, nu: (0, 0)),
            ],
            out_specs=pl.BlockSpec((MOE_BLK * r, 128), lambda i, be, nu: (i, 0)),
            scratch_shapes=[pltpu.VMEM((2, d, two_f), F32), pltpu.VMEM((2, f, d), F32),
                            pltpu.VMEM((d, two_f), BF16), pltpu.VMEM((f, d), BF16),
                            pltpu.SemaphoreType.DMA((2,)), pltpu.SemaphoreType.DMA((2,))],
        ),
        out_shape=jax.ShapeDtypeStruct((P * r, 128), F32),
        compiler_params=_cparams(sem=("arbitrary",), vmem=EXPERT_VMEM_LIMIT),
        name="moe_experts",
    )(blk_e, n_used, xs, w1, b1, w2, b2, _deinterleave_matrix())


def _moe(h2t, idx, gates, w1, b1, w2, b2, *, T, layer):
    lpos, ntab, cbtab, cnt = _rank(idx, T=T)
    counts = cnt[:, 0].astype(I32)
    padded = (counts + RUN_ALIGN + MOE_BLK - 1) // MOE_BLK * MOE_BLK
    pad_end = jnp.cumsum(padded)
    pstart = (pad_end - padded).astype(I32)
    nblk = TOP_K * T // MOE_BLK + 2 * N_EXPERTS
    P = nblk * MOE_BLK
    starts = jnp.arange(nblk, dtype=I32) * MOE_BLK
    blk_e = jnp.minimum(jnp.sum((pad_end[None, :] <= starts[:, None]).astype(I32), axis=1), N_EXPERTS - 1)
    n_used = (pad_end[-1:] // MOE_BLK).astype(I32)
    ntab = ntab[:, :, 0].reshape(-1)
    cbtab = cbtab[:, :, 0].reshape(-1)
    xs = _dispatch(pstart, counts, padded.astype(I32), ntab, cbtab, lpos, h2t, T=T, P=P)
    ys = _experts(blk_e.astype(I32), n_used, xs, w1, b1, w2, b2, P=P, layer=layer)
    return _collect(pstart, ntab, cbtab, lpos, gates, ys, T=T)


def _residual_kernel(f_ref, x_ref, g2_ref, lg_ref, lb_ref, o_ref, *, nb, n_lat_tiles, alpha):
    for b in range(nb):
        f = _from_token_tiles(f_ref.at[b], TILE)
        z = alpha * x_ref[b] + g2_ref[_mod_row(b, n_lat_tiles, nb)] * f
        o_ref[b] = _normalize(z) * lg_ref[...] + lb_ref[...]


def _residual(ft, x1, g2, ln_g, ln_b, *, B, S, n_out_tiles, n_lat_tiles, alpha):
    r = ROWS_PER_TOKEN
    vec = pl.BlockSpec((1, D_MODEL), lambda i: (0, 0))
    return pl.pallas_call(
        functools.partial(_residual_kernel, nb=B, n_lat_tiles=n_lat_tiles, alpha=alpha),
        grid=(n_out_tiles,),
        in_specs=[
            pl.BlockSpec((B, TILE * r, 128), lambda i: (0, i, 0)),
            pl.BlockSpec((B, TILE, D_MODEL), lambda i: (0, i, 0)),
            _mod_spec(), vec, vec,
        ],
        out_specs=pl.BlockSpec((B, TILE, D_MODEL), lambda i: (0, i, 0)),
        out_shape=jax.ShapeDtypeStruct((B, n_out_tiles * TILE, D_MODEL), F32),
        compiler_params=_cparams(vmem=VMEM_LIMIT),
        name="ffn_residual_norm",
    )(ft.reshape(B, S * r, 128), x1, g2, ln_g, ln_b)


def _rope_tables(n_lat, n_ctx):
    half = HEAD_DIM // 2
    pos = np.arange(n_lat)
    inv = 1.0 / (ROPE_BASE ** (np.arange(0, half, 2, dtype=np.float64) / half))
    ang_row = (pos // GRID_W)[:, None].astype(np.float64) * inv[None, :]
    ang_col = (pos % GRID_W)[:, None].astype(np.float64) * inv[None, :]
    ang = np.concatenate([ang_row, ang_row, ang_col, ang_col], axis=1)
    sign = np.concatenate([-np.ones(16), np.ones(16)] * 2)[None, :]
    cos = np.concatenate([np.cos(ang), np.ones((n_ctx, HEAD_DIM))], axis=0)
    sin = np.concatenate([np.sin(ang) * sign, np.zeros((n_ctx, HEAD_DIM))], axis=0)
    return (jnp.asarray(np.tile(cos, (1, 2)), dtype=F32), jnp.asarray(np.tile(sin, (1, 2)), dtype=F32))


def _extend_w_in(w):
    g = MIX_GROUP_W
    hd = HEAD_DIM
    a, bp, q = w[:, 0:g], w[:, g:2 * g], w[:, 2 * g:3 * g]
    k, v = w[:, 3 * g:3 * g + 2 * hd], w[:, 3 * g + 2 * hd:4 * g]
    rest = w[:, 4 * g:]
    dup = lambda t: jnp.concatenate([t[:, :hd], t[:, :hd], t[:, hd:], t[:, hd:]], axis=1)
    return jnp.concatenate([a, bp, q, dup(k), dup(v), rest], axis=1).astype(BF16)


def _block_diag(pw):
    n, c, _ = pw.shape
    out = jnp.zeros((n * c, n * c), pw.dtype)
    for gi in range(n):
        out = lax.dynamic_update_slice(out, pw[gi], (gi * c, gi * c))
    return out


def kernel(x, c, ctx, c_ctx, w_mod, b_mod, w_in, pool_w, pool_scale, attn_sink, ret_decay, w_out, ln_g, ln_b,
           router_w, router_b, exp_w1, exp_b1, exp_w2, exp_b2):
    B, N, D = x.shape
    L = ctx.shape[1]
    depth = w_mod.shape[0]
    assert D == D_MODEL and L == TILE and N % TILE == 0 and N % (FFT_RADIX2 * 16) == 0 and B < 8
    S = N + L
    n_lat_tiles = N // TILE
    assert (B * S) % MOE_TOK_TILE == 0 and (B * N) % MOE_TOK_TILE == 0
    alpha = float((2 * depth) ** 0.25)

    cvec = jnp.zeros((8, D), F32).at[:B].set(c).at[B].set(c_ctx)
    mods = _modulation(cvec, w_mod, b_mod)
    cos, sin = _rope_tables(N, L)

    xs = (x, ctx)
    out = None
    for l in range(depth):
        last = l == depth - 1
        m = mods[l].reshape(8, 6, 1, D)
        sh1, sc1, g1, sh2, sc2, g2 = [m[:, j] for j in range(6)]
        proj = _projection(xs, sc1, sh1, _extend_w_in(w_in[l]), cos, sin, B=B, S=S, n_lat_tiles=n_lat_tiles)
        a, bp, q, k2, v2, rq, rk, rv, rg = proj
        y_f = _fourier_mix(a, B=B, N=N, L=L, with_ctx=not last)
        y_p = _pool_mix(bp, _block_diag(pool_w[l]).astype(BF16), pool_scale[l].reshape(1, -1), B=B, N=N, L=L)
        y_a = _attention(attn_sink[l], q, k2, v2, B=B, N=N, L=L)
        dec = jnp.repeat(ret_decay[l], MIX_GROUP_W // RET_HEADS, axis=1)
        y_r = _retention(dec, rq, rk, rv, rg, B=B, N=N, L=L)
        rows = N if last else S
        x1, h2t, idx, gates = _merge(
            xs, (y_f, y_p, y_a, y_r), g1, sc2, sh2, ln_g[l, 0:1], ln_b[l, 0:1], w_out[l].astype(BF16),
            router_w[l].T.astype(BF16), router_b[l].reshape(-1, 1), B=B, S=rows, n_lat_tiles=n_lat_tiles, alpha=alpha)
        b1 = jnp.concatenate([exp_b1[l][:, 0::2], exp_b1[l][:, 1::2]], axis=-1)[:, None, :]
        ft = _moe(h2t, idx, gates, exp_w1, b1, exp_w2, exp_b2[l][:, None, :], T=B * rows, layer=l)
        out = _residual(ft, x1, g2, ln_g[l, 1:2], ln_b[l, 1:2], B=B, S=rows,
                        n_out_tiles=rows // TILE, n_lat_tiles=n_lat_tiles, alpha=alpha)
        xs = (out,)
    return out
```

```python
import functools

import numpy as np
import jax
import jax.numpy as jnp
from jax import lax
from jax.experimental import pallas as pl
from jax.experimental.pallas import tpu as pltpu

F32 = jnp.float32
BF16 = jnp.bfloat16
I32 = jnp.int32

D_MODEL = 1024
GRID_W = 64
FOURIER_GROUP_W = 64
POOL_WINDOWS = (2, 4, 8, 16)
HEAD_DIM = 64
WINDOW = 128
ROPE_BASE = 10000.0
RET_HEADS = 4
RET_CHUNK = 128
N_EXPERTS = 32
TOP_K = 4
SWIGLU_LIMIT = 7.0
SWIGLU_ALPHA = 1.702
LN_EPS = 1e-6
NEG_INF = -1e30
MIX_GROUP_W = 256
FFT_RADIX2 = 64

TILE = 256
MOE_BLK = 512
MOE_TOK_TILE = 1024
ROWS_PER_TOKEN = D_MODEL // 128
RUN_ALIGN = 16
RUN_SHIFT = 4
CHUNK_ROWS = RUN_ALIGN * ROWS_PER_TOKEN
STAGE_TOKENS = TOP_K * MOE_TOK_TILE + N_EXPERTS * RUN_ALIGN
VMEM_LIMIT = 48 * 1024 * 1024
EXPERT_VMEM_LIMIT = 56 * 1024 * 1024


def _dot(a, b):
    return jnp.dot(a, b, preferred_element_type=F32)


def _dot_nt(a, b):
    return lax.dot_general(a, b, (((1,), (1,)), ((), ())), preferred_element_type=F32)


def _dot_tn(a, b):
    return lax.dot_general(a, b, (((0,), (0,)), ((), ())), preferred_element_type=F32)


def _normalize(x):
    mu = jnp.mean(x, axis=-1, keepdims=True)
    xc = x - mu
    var = jnp.mean(xc * xc, axis=-1, keepdims=True)
    return xc * lax.rsqrt(var + LN_EPS)


def _cparams(sem=None, vmem=None):
    kw = {}
    if sem is not None:
        kw["dimension_semantics"] = sem
    if vmem is not None:
        kw["vmem_limit_bytes"] = vmem
    return pltpu.CompilerParams(**kw)


def _to_token_tiles(ref, val):
    n = val.shape[0]
    for c in range(ROWS_PER_TOKEN):
        ref[pl.ds(c, n, stride=ROWS_PER_TOKEN), :] = val[:, 128 * c:128 * (c + 1)]


def _from_token_tiles(ref, n):
    return jnp.concatenate([ref[pl.ds(c, n, stride=ROWS_PER_TOKEN), :] for c in range(ROWS_PER_TOKEN)], axis=-1)


def _token_rows(tok, count=1):
    return pl.ds(pl.multiple_of(tok * ROWS_PER_TOKEN, ROWS_PER_TOKEN), count * ROWS_PER_TOKEN)


def _mod_kernel(c_ref, w_ref, b_ref, o_ref):
    c = c_ref[...]
    a = c * jax.nn.sigmoid(c)
    w = w_ref[...]
    a_hi = a.astype(BF16)
    a_lo = (a - a_hi.astype(F32)).astype(BF16)
    w_hi = w.astype(BF16)
    w_lo = (w - w_hi.astype(F32)).astype(BF16)
    o_ref[...] = _dot(a_hi, w_hi) + _dot(a_lo, w_hi) + _dot(a_hi, w_lo) + b_ref[...]


def _modulation(cvec, w_mod, b_mod):
    depth, d, six_d = w_mod.shape
    nb = 1536
    return pl.pallas_call(
        _mod_kernel,
        grid=(depth, six_d // nb),
        in_specs=[
            pl.BlockSpec((8, d), lambda l, j: (0, 0)),
            pl.BlockSpec((None, d, nb), lambda l, j: (l, 0, j)),
            pl.BlockSpec((None, 1, nb), lambda l, j: (l, 0, j)),
        ],
        out_specs=pl.BlockSpec((None, 8, nb), lambda l, j: (l, 0, j)),
        out_shape=jax.ShapeDtypeStruct((depth, 8, six_d), F32),
        compiler_params=_cparams(vmem=VMEM_LIMIT),
        name="modulation",
    )(cvec, w_mod, b_mod.reshape(depth, 1, six_d))


def _tile_inputs(refs, b, split_input, n_lat_tiles):
    if split_input:
        x = jnp.where(pl.program_id(0) == n_lat_tiles, refs[1][b], refs[0][b])
        return x, refs[2:]
    return refs[0][b], refs[1:]


def _mod_row(b, n_lat_tiles, ctx_row):
    return jnp.where(pl.program_id(0) == n_lat_tiles, ctx_row, b)


def _proj_kernel(*refs, nb, split_input, n_lat_tiles):
    g = MIX_GROUP_W
    for b in range(nb):
        x, rest = _tile_inputs(refs, b, split_input, n_lat_tiles)
        sc_ref, sh_ref, w_ref, cos_ref, sin_ref = rest[:5]
        a_ref, bp_ref, q_ref, k_ref, v_ref, rq_ref, rk_ref, rv_ref, rg_ref = rest[5:]
        row = _mod_row(b, n_lat_tiles, nb)
        h = _normalize(x) * (1.0 + sc_ref[row]) + sh_ref[row]
        p = _dot(h.astype(BF16), w_ref[...])
        a_ref[b] = p[:, 0:g].astype(BF16)
        bp_ref[b] = p[:, g:2 * g].astype(BF16)
        qk = p[:, 2 * g:4 * g]
        lane = lax.broadcasted_iota(I32, qk.shape, 1)
        first = (lane & 16) == 0
        partner = jnp.where(first, pltpu.roll(qk, 2 * g - 16, 1), pltpu.roll(qk, 16, 1))
        cos = jnp.concatenate([cos_ref[...]] * 4, axis=-1)
        sin = jnp.concatenate([sin_ref[...]] * 4, axis=-1)
        qk = qk * cos + partner * sin
        q_ref[b] = (qk[:, :g] * (HEAD_DIM ** -0.5)).astype(BF16)
        k_ref[b] = qk[:, g:].astype(BF16)
        v_ref[b] = p[:, 4 * g:5 * g].astype(BF16)
        rq_ref[b] = p[:, 5 * g:6 * g].astype(BF16)
        rk_ref[b] = p[:, 6 * g:7 * g].astype(BF16)
        rv_ref[b] = p[:, 7 * g:8 * g].astype(BF16)
        rg_ref[b] = p[:, 8 * g:9 * g].astype(BF16)


def _mod_spec():
    return pl.BlockSpec((8, 1, D_MODEL), lambda i: (0, 0, 0))


def _x_specs(B, split_input, n_lat_tiles):
    if split_input:
        return [
            pl.BlockSpec((B, TILE, D_MODEL), lambda i: (0, jnp.minimum(i, n_lat_tiles - 1), 0)),
            pl.BlockSpec((B, TILE, D_MODEL), lambda i: (0, 0, 0)),
        ]
    return [pl.BlockSpec((B, TILE, D_MODEL), lambda i: (0, i, 0))]


def _projection(xs, sc, sh, w_ext, cos, sin, *, B, S, n_lat_tiles):
    split_input = len(xs) == 2
    nt = S // TILE
    g = MIX_GROUP_W
    out_spec = pl.BlockSpec((B, TILE, g), lambda i: (0, i, 0))
    out_shape = jax.ShapeDtypeStruct((B, S, g), BF16)
    return pl.pallas_call(
        functools.partial(_proj_kernel, nb=B, split_input=split_input, n_lat_tiles=n_lat_tiles),
        grid=(nt,),
        in_specs=_x_specs(B, split_input, n_lat_tiles) + [
            _mod_spec(),
            _mod_spec(),
            pl.BlockSpec(w_ext.shape, lambda i: (0, 0)),
            pl.BlockSpec((TILE, 128), lambda i: (i, 0)),
            pl.BlockSpec((TILE, 128), lambda i: (i, 0)),
        ],
        out_specs=[out_spec] * 9,
        out_shape=[out_shape] * 9,
        compiler_params=_cparams(vmem=VMEM_LIMIT),
        name="ln_mod_in_proj",
    )(*xs, sc, sh, w_ext, cos, sin)


def _channel_dft_matrix():
    c = np.arange(FOURIER_GROUP_W)
    ang = 2.0 * np.pi * np.outer(c, c) / FOURIER_GROUP_W
    eye = np.eye(MIX_GROUP_W // FOURIER_GROUP_W)
    cs = np.concatenate([np.kron(eye, np.cos(ang)), -np.kron(eye, np.sin(ang))], axis=1)
    return jnp.asarray(cs, dtype=BF16)


def _seq_dft_stage_a(n1):
    n2 = FFT_RADIX2
    n = n1 * n2
    k1 = np.arange(n1)[None, :, None]
    i1 = np.arange(n1)[None, None, :]
    i2 = np.arange(n2)[:, None, None]
    th = 2.0 * np.pi * ((k1 * (n2 * i1 + i2)) % n) / n
    wr, wi = np.cos(th), -np.sin(th)
    wa = np.concatenate([np.concatenate([wr, -wi], axis=2), np.concatenate([wi, wr], axis=2)], axis=1)
    return jnp.asarray(wa, dtype=BF16)


def _seq_dft_stage_b():
    n2 = FFT_RADIX2
    k2 = np.arange(n2)
    ph = 2.0 * np.pi * (np.outer(k2, k2) % n2) / n2
    return jnp.asarray(np.concatenate([np.cos(ph), np.sin(ph)], axis=1), dtype=BF16)


def _dense_dft(length):
    k = np.arange(length)
    th = 2.0 * np.pi * (np.outer(k, k) % length) / length
    return jnp.asarray(np.concatenate([np.cos(th), np.sin(th)], axis=1), dtype=BF16)


FFT_A_COLS = 8
FFT_B_COLS = 16


def _fft_a_kernel(a_ref, cs_ref, wa_ref, t_ref, *, nb, n1):
    g = MIX_GROUP_W
    for j in range(FFT_A_COLS):
        for b in range(nb):
            p = _dot(a_ref[b, :, g * j:g * (j + 1)], cs_ref[...])
            pp = jnp.concatenate([p[:, :g], p[:, g:]], axis=0).astype(BF16)
            t = _dot(wa_ref[j], pp)
            t_ref[b, :, 2 * g * j:2 * g * (j + 1)] = jnp.concatenate([t[:n1], t[n1:]], axis=-1).astype(BF16)


def _fft_b_kernel(t_ref, vb_ref, y_ref, *, nb, scale):
    g = MIX_GROUP_W
    n2 = FFT_RADIX2
    for j in range(FFT_B_COLS):
        for b in range(nb):
            t = t_ref[b, n2 * j:n2 * (j + 1), :]
            tt = jnp.concatenate([t[:, :g], t[:, g:]], axis=0)
            y = _dot(vb_ref[...], tt) * scale
            y_ref[b, 0:n2, g * j:g * (j + 1)] = y.astype(BF16)
            y_ref[b, n2:, g * j:g * (j + 1)] = jnp.zeros((y_ref.shape[1] - n2, g), BF16)


def _fft_ctx_kernel(a_ref, cs_ref, wc_ref, yin_ref, y_ref, *, scale):
    del yin_ref
    g = MIX_GROUP_W
    p = _dot(a_ref[...], cs_ref[...])
    pp = jnp.concatenate([p[:, :g], p[:, g:]], axis=0).astype(BF16)
    y_ref[...] = (_dot(wc_ref[...], pp) * scale).astype(BF16)


def _fourier_mix(a, *, B, N, L, with_ctx):
    S = N + L
    g = MIX_GROUP_W
    n2 = FFT_RADIX2
    n1 = N // n2
    cs = _channel_dft_matrix()
    a2 = a.reshape(B, S // n2, n2 * g)
    t = pl.pallas_call(
        functools.partial(_fft_a_kernel, nb=B, n1=n1),
        grid=(n2 // FFT_A_COLS,),
        in_specs=[
            pl.BlockSpec((B, n1, FFT_A_COLS * g), lambda j: (0, 0, j)),
            pl.BlockSpec(cs.shape, lambda j: (0, 0)),
            pl.BlockSpec((FFT_A_COLS, 2 * n1, 2 * n1), lambda j: (j, 0, 0)),
        ],
        out_specs=pl.BlockSpec((B, n1, FFT_A_COLS * 2 * g), lambda j: (0, 0, j)),
        out_shape=jax.ShapeDtypeStruct((B, n1, n2 * 2 * g), BF16),
        name="fourier_seq_stage_a",
    )(a2, cs, _seq_dft_stage_a(n1))
    t = t.reshape(B, n1 * n2, 2 * g)
    rows = S // n1
    y = pl.pallas_call(
        functools.partial(_fft_b_kernel, nb=B, scale=float((N * FOURIER_GROUP_W) ** -0.5)),
        grid=(n1 // FFT_B_COLS,),
        in_specs=[
            pl.BlockSpec((B, FFT_B_COLS * n2, 2 * g), lambda k: (0, k, 0)),
            pl.BlockSpec((n2, 2 * n2), lambda k: (0, 0)),
        ],
        out_specs=pl.BlockSpec((B, rows, FFT_B_COLS * g), lambda k: (0, 0, k)),
        out_shape=jax.ShapeDtypeStruct((B, rows, n1 * g), BF16),
        name="fourier_seq_stage_b",
    )(t, _seq_dft_stage_b())
    y = y.reshape(B, S, g)
    if not with_ctx:
        return y
    ctx_tile = N // TILE
    return pl.pallas_call(
        functools.partial(_fft_ctx_kernel, scale=float((L * FOURIER_GROUP_W) ** -0.5)),
        grid=(B,),
        in_specs=[
            pl.BlockSpec((None, TILE, g), lambda b: (b, ctx_tile, 0)),
            pl.BlockSpec(cs.shape, lambda b: (0, 0)),
            pl.BlockSpec((L, 2 * L), lambda b: (0, 0)),
            pl.BlockSpec(memory_space=pl.ANY),
        ],
        out_specs=pl.BlockSpec((None, TILE, g), lambda b: (b, ctx_tile, 0)),
        out_shape=jax.ShapeDtypeStruct((B, S, g), BF16),
        input_output_aliases={3: 0},
        name="fourier_ctx",
    )(a, cs, _dense_dft(L), y)


POOL_HALO = 16


def _pool_bands():
    t = np.arange(TILE)[:, None]
    s = np.arange(TILE + 2 * POOL_HALO)[None, :] - POOL_HALO
    bands = [((s - t >= -(w // 2)) & (s - t < w // 2)).astype(np.float32) for w in POOL_WINDOWS]
    return jnp.asarray(np.stack(bands), dtype=BF16)


def _pool_kernel(prev_ref, x_ref, next_ref, band_ref, pw_ref, ps_ref, o_ref, *, nb, n_lat, s_tot):
    i = pl.program_id(0)
    t0 = i * TILE
    is_ctx = t0 >= n_lat
    r0 = jnp.where(is_ctx, n_lat, 0)
    r1 = jnp.where(is_ctx, s_tot, n_lat)
    kw = TILE + 2 * POOL_HALO
    s_pos = t0 - POOL_HALO + lax.broadcasted_iota(I32, (kw, 1), 0)
    inside = jnp.where(s_pos >= r0, jnp.where(s_pos < r1, 1.0, 0.0), 0.0).astype(BF16)
    xc = jnp.concatenate([jnp.concatenate([prev_ref[b], x_ref[b], next_ref[b]], axis=0) * inside
                          for b in range(nb)], axis=-1)
    g = MIX_GROUP_W
    t_col = t0 + lax.broadcasted_iota(I32, (TILE, 1), 0)
    group = (lax.broadcasted_iota(I32, (TILE, nb * g), 1) // FOURIER_GROUP_W) % len(POOL_WINDOWS)
    acc = jnp.zeros((TILE, nb * g), F32)
    for gi, w in enumerate(POOL_WINDOWS):
        lo = jnp.maximum(t_col - w // 2, r0)
        hi = jnp.minimum(t_col + w // 2, r1)
        mean = _dot(band_ref[gi], xc) / (hi - lo).astype(F32)
        acc = jnp.where(group == gi, mean, acc)
    for b in range(nb):
        pooled = acc[:, g * b:g * (b + 1)] - x_ref[b].astype(F32)
        o_ref[b] = (_dot(pooled.astype(BF16), pw_ref[...]) * ps_ref[...]).astype(BF16)


def _pool_mix(bp, pw_bd, pscale, *, B, N, L):
    S = N + L
    nt = S // TILE
    g = MIX_GROUP_W
    hb = TILE // POOL_HALO
    last = S // POOL_HALO - 1
    return pl.pallas_call(
        functools.partial(_pool_kernel, nb=B, n_lat=N, s_tot=S),
        grid=(nt,),
        in_specs=[
            pl.BlockSpec((B, POOL_HALO, g), lambda i: (0, jnp.maximum(i * hb - 1, 0), 0)),
            pl.BlockSpec((B, TILE, g), lambda i: (0, i, 0)),
            pl.BlockSpec((B, POOL_HALO, g), lambda i: (0, jnp.minimum((i + 1) * hb, last), 0)),
            pl.BlockSpec((len(POOL_WINDOWS), TILE, TILE + 2 * POOL_HALO), lambda i: (0, 0, 0)),
            pl.BlockSpec((g, g), lambda i: (0, 0)),
            pl.BlockSpec((1, g), lambda i: (0, 0)),
        ],
        out_specs=pl.BlockSpec((B, TILE, g), lambda i: (0, i, 0)),
        out_shape=jax.ShapeDtypeStruct((B, S, g), BF16),
        name="pool_mix",
    )(bp, bp, bp, _pool_bands(), pw_bd, pscale)


def _attn_window_bias(n_ctx):
    r = (np.arange(2 * TILE) % TILE)[:, None]
    c = np.arange(TILE + 2 * WINDOW)[None, :] - WINDOW
    band = np.where(np.abs(r - c) <= WINDOW, 0.0, NEG_INF)
    return jnp.asarray(np.concatenate([band, np.zeros((2 * TILE, n_ctx))], axis=1), dtype=F32)


def _attn_kernel(sink_ref, bias_ref, q_ref, kp_ref, kc_ref, kn_ref, kx_ref, vp_ref, vc_ref, vn_ref, vx_ref, o_ref,
                 *, nb, n_lat):
    i = pl.program_id(0)
    t0 = i * TILE
    nband = TILE + 2 * WINDOW
    nkeys = nband + kx_ref.shape[1]
    col = lax.broadcasted_iota(I32, (1, nkeys), 1)
    kpos = t0 - WINDOW + col
    valid = (col >= nband) | ((kpos >= 0) & (kpos < n_lat) & (t0 < n_lat))
    bias = bias_ref[...]
    lane = lax.broadcasted_iota(I32, (TILE, 2 * HEAD_DIM), 1)
    low = lane < HEAD_DIM
    keep_low = jnp.where(low, 1.0, 0.0).astype(BF16)
    keep_high = jnp.where(low, 0.0, 1.0).astype(BF16)
    row1 = lax.broadcasted_iota(I32, (2 * TILE, 1), 0)
    for b in range(nb):
        outs = []
        for h in range(2):
            sl = slice(2 * HEAD_DIM * h, 2 * HEAD_DIM * (h + 1))
            qh = q_ref[b, :, sl]
            qs = jnp.concatenate([qh * keep_low, qh * keep_high], axis=0)
            kh = jnp.concatenate([kp_ref[b, :, sl], kc_ref[b, :, sl], kn_ref[b, :, sl], kx_ref[b, :, sl]], axis=0)
            vh = jnp.concatenate([vp_ref[b, :, sl], vc_ref[b, :, sl], vn_ref[b, :, sl], vx_ref[b, :, sl]], axis=0)
            s = jnp.where(valid, _dot_nt(qs, kh) + bias, NEG_INF)
            sink = jnp.where(row1 < TILE, sink_ref[2 * h], sink_ref[2 * h + 1])
            m = jnp.maximum(jnp.max(s, axis=-1, keepdims=True), sink)
            p = jnp.exp(s - m)
            den = jnp.sum(p, axis=-1, keepdims=True) + jnp.exp(sink - m)
            o = _dot(p.astype(BF16), vh) / den
            outs.append(jnp.where(low, o[:TILE], o[TILE:]))
        o_ref[b] = jnp.concatenate(outs, axis=-1).astype(BF16)


def _attention(sinks, q, k2, v2, *, B, N, L):
    S = N + L
    nt = S // TILE
    g = MIX_GROUP_W
    hb = TILE // WINDOW
    last = N // WINDOW - 1
    ctx_tile = N // TILE
    prev = pl.BlockSpec((B, WINDOW, g), lambda i: (0, jnp.clip(i * hb - 1, 0, last), 0))
    cur = pl.BlockSpec((B, TILE, g), lambda i: (0, i, 0))
    nxt = pl.BlockSpec((B, WINDOW, g), lambda i: (0, jnp.minimum((i + 1) * hb, last), 0))
    ctx = pl.BlockSpec((B, L, g), lambda i: (0, ctx_tile, 0))
    return pl.pallas_call(
        functools.partial(_attn_kernel, nb=B, n_lat=N),
        grid=(nt,),
        in_specs=[pl.BlockSpec(memory_space=pltpu.SMEM),
                  pl.BlockSpec((2 * TILE, TILE + 2 * WINDOW + L), lambda i: (0, 0)),
                  cur, prev, cur, nxt, ctx, prev, cur, nxt, ctx],
        out_specs=pl.BlockSpec((B, TILE, g), lambda i: (0, i, 0)),
        out_shape=jax.ShapeDtypeStruct((B, S, g), BF16),
        compiler_params=_cparams(vmem=VMEM_LIMIT),
        name="banded_attention",
    )(sinks, _attn_window_bias(L), q, k2, k2, k2, k2, v2, v2, v2, v2)


def _ret_kernel(*refs, reverse, nb):
    if reverse:
        dec_ref, q_ref, k_ref, v_ref, of_ref, g_ref, o_ref, r_scr = refs
    else:
        dec_ref, q_ref, k_ref, v_ref, o_ref, r_scr = refs
    c = RET_CHUNK
    w = MIX_GROUP_W
    dh = w // RET_HEADS

    @pl.when(pl.program_id(0) == 0)
    def _():
        r_scr[...] = jnp.zeros(r_scr.shape, F32)

    lg = -jnp.exp(dec_ref[...])
    idx = lax.broadcasted_iota(I32, (c, 1), 0).astype(F32)
    jdx = lax.broadcasted_iota(I32, (1, c), 1).astype(F32)
    if reverse:
        idx, jdx = (c - 1.0) - idx, (c - 1.0) - jdx
    diff = idx - jdx
    dmask_all = jnp.concatenate(
        [jnp.where(diff >= 0, jnp.exp(jnp.maximum(diff, 0.0) * lg[:, dh * h:dh * h + 1]), 0.0)
         for h in range(RET_HEADS)], axis=1)
    xi = jnp.exp((idx + 1.0) * lg)
    zeta = jnp.exp((c - 1.0 - idx) * lg)
    chunk_decay = jnp.exp(float(c) * lg)
    lane_head = lax.broadcasted_iota(I32, (1, w), 1) // dh
    keep_head = [jnp.where(lane_head == h, 1.0, 0.0).astype(BF16) for h in range(RET_HEADS)]
    same_head = (lax.broadcasted_iota(I32, (w, w), 0) // dh) == (lax.broadcasted_iota(I32, (w, w), 1) // dh)

    for b in range(nb):
        q = q_ref[b]
        k = k_ref[b] * jnp.asarray(dh ** -0.5, BF16)
        v = v_ref[b]
        state = r_scr[b]
        k_heads = jnp.concatenate([k * keep_head[h] for h in range(RET_HEADS)], axis=0)
        v_heads = jnp.concatenate([v * keep_head[h] for h in range(RET_HEADS)], axis=0)
        s = _dot_nt(q, k_heads) * dmask_all
        o = _dot(q, state.astype(BF16)) * xi + _dot(s.astype(BF16), v_heads)
        kv = _dot_tn(k, (v.astype(F32) * zeta).astype(BF16))
        r_scr[b] = state * chunk_decay + jnp.where(same_head, kv, 0.0)
        if not reverse:
            o_ref[b] = o
        else:
            tot = o + of_ref[b]
            mu = jnp.zeros_like(tot)
            for h in range(RET_HEADS):
                mh = lane_head == h
                mu = jnp.where(mh, jnp.sum(jnp.where(mh, tot, 0.0), axis=-1, keepdims=True) / dh, mu)
            cen = tot - mu
            var = jnp.zeros_like(tot)
            for h in range(RET_HEADS):
                mh = lane_head == h
                var = jnp.where(mh, jnp.sum(jnp.where(mh, cen * cen, 0.0), axis=-1, keepdims=True) / dh, var)
            gate = g_ref[b].astype(F32)
            o_ref[b] = (gate * jax.nn.sigmoid(gate) * (cen * lax.rsqrt(var + LN_EPS))).astype(BF16)


def _retention(dec, rq, rk, rv, rg, *, B, N, L):
    S = N + L
    c = RET_CHUNK
    w = MIX_GROUP_W
    nc_lat, nc_ctx = N // c, L // c
    nc = nc_lat + nc_ctx

    def fwd_blk(j):
        return jnp.where(j < nc_ctx, nc_lat + j, j - nc_ctx)

    def bwd_blk(j):
        return nc - 1 - j

    def call(blk, reverse, extra):
        spec = pl.BlockSpec((B, c, w), lambda j: (0, blk(j), 0))
        dspec = pl.BlockSpec((1, w), lambda j: (0, 0))
        return pl.pallas_call(
            functools.partial(_ret_kernel, reverse=reverse, nb=B),
            grid=(nc,),
            in_specs=[dspec] + [spec] * (3 + len(extra)),
            out_specs=spec,
            out_shape=jax.ShapeDtypeStruct((B, S, w), BF16 if reverse else F32),
            scratch_shapes=[pltpu.VMEM((B, w, w), F32)],
            compiler_params=_cparams(sem=("arbitrary",), vmem=VMEM_LIMIT),
            name="retention_bwd" if reverse else "retention_fwd",
        )(dec[1:2] if reverse else dec[0:1], rq, rk, rv, *extra)

    o_f = call(fwd_blk, False, ())
    return call(bwd_blk, True, (o_f, rg))


def _merge_kernel(*refs, nb, split_input, n_lat_tiles, alpha):
    for b in range(nb):
        x, rest = _tile_inputs(refs, b, split_input, n_lat_tiles)
        (yf_ref, yp_ref, ya_ref, yr_ref, g1_ref, sc_ref, sh_ref, lg_ref, lb_ref, wo_ref, rw_ref, rb_ref,
         x1_ref, h2_ref) = rest[:14]
        row = _mod_row(b, n_lat_tiles, nb)
        ycat = jnp.concatenate([yf_ref[b], yp_ref[b], ya_ref[b], yr_ref[b]], axis=-1)
        y = _dot(ycat, wo_ref[...])
        x1 = _normalize(alpha * x + g1_ref[row] * y) * lg_ref[...] + lb_ref[...]
        x1_ref[b] = x1
        h2 = _normalize(x1) * (1.0 + sc_ref[row]) + sh_ref[row]
        _to_token_tiles(h2_ref.at[b], h2)
        logits = _dot_nt(rw_ref[...], h2.astype(BF16)) + rb_ref[...]
        _route(logits, rest[14 + b], rest[14 + nb + b])


def _route(logits, idx_ref, gate_ref):
    eid = lax.broadcasted_iota(I32, logits.shape, 0)
    orow = lax.broadcasted_iota(I32, (8, logits.shape[1]), 0)
    idx_out = jnp.zeros((8, logits.shape[1]), I32)
    val_out = jnp.zeros((8, logits.shape[1]), F32)
    top = None
    den = None
    for kk in range(TOP_K):
        m = jnp.max(logits, axis=0, keepdims=True)
        sel = jnp.min(jnp.where(logits == m, eid, N_EXPERTS), axis=0, keepdims=True)
        logits = jnp.where(eid == sel, -jnp.inf, logits)
        if kk == 0:
            top = m
        e = jnp.exp(m - top)
        den = e if kk == 0 else den + e
        idx_out = jnp.where(orow == kk, sel, idx_out)
        val_out = jnp.where(orow == kk, e, val_out)
    idx_ref[...] = idx_out
    gate_ref[...] = val_out / den


def _merge(xs, ys, g1, sc2, sh2, ln_g, ln_b, w_out, rw_t, rb, *, B, S, n_lat_tiles, alpha):
    split_input = len(xs) == 2
    nt = S // TILE
    g = MIX_GROUP_W
    r = ROWS_PER_TOKEN
    yspec = pl.BlockSpec((B, TILE, g), lambda i: (0, i, 0))
    vec = pl.BlockSpec((1, D_MODEL), lambda i: (0, 0))
    route_spec = pl.BlockSpec((8, TILE), lambda i: (0, i))
    outs = pl.pallas_call(
        functools.partial(_merge_kernel, nb=B, split_input=split_input, n_lat_tiles=n_lat_tiles, alpha=alpha),
        grid=(nt,),
        in_specs=_x_specs(B, split_input, n_lat_tiles) + [yspec] * 4 + [
            _mod_spec(), _mod_spec(), _mod_spec(), vec, vec,
            pl.BlockSpec((D_MODEL, D_MODEL), lambda i: (0, 0)),
            pl.BlockSpec((N_EXPERTS, D_MODEL), lambda i: (0, 0)),
            pl.BlockSpec((N_EXPERTS, 1), lambda i: (0, 0)),
        ],
        out_specs=[
            pl.BlockSpec((B, TILE, D_MODEL), lambda i: (0, i, 0)),
            pl.BlockSpec((B, TILE * r, 128), lambda i: (0, i, 0)),
        ] + [route_spec] * (2 * B),
        out_shape=[
            jax.ShapeDtypeStruct((B, S, D_MODEL), F32),
            jax.ShapeDtypeStruct((B, S * r, 128), F32),
        ] + [jax.ShapeDtypeStruct((8, S), I32)] * B + [jax.ShapeDtypeStruct((8, S), F32)] * B,
        compiler_params=_cparams(vmem=VMEM_LIMIT),
        name="out_proj_norm_router",
    )(*xs, *ys, g1, sc2, sh2, ln_g, ln_b, w_out, rw_t, rb)
    x1, h2t = outs[:2]
    idx = jnp.concatenate(outs[2:2 + B], axis=1)
    gates = jnp.concatenate(outs[2 + B:], axis=1)
    return x1, h2t.reshape(B * S * r, 128), idx, gates


def _rank_kernel(idx_ref, lpos_ref, ntab_ref, cbtab_ref, cnt_ref, carry_scr):
    tt = idx_ref.shape[1]

    @pl.when(pl.program_id(0) == 0)
    def _():
        carry_scr[...] = jnp.zeros(carry_scr.shape, F32)

    eid = lax.broadcasted_iota(I32, (N_EXPERTS, tt), 0)
    before = jnp.where(lax.broadcasted_iota(I32, (tt, tt), 0) < lax.broadcasted_iota(I32, (tt, tt), 1),
                       1.0, 0.0).astype(BF16)
    hits, cums, pres = [], [], []
    n_tile = jnp.zeros((N_EXPERTS, 1), F32)
    for kk in range(TOP_K):
        hit = eid == idx_ref[kk:kk + 1, :]
        hits.append(hit)
        cums.append(_dot(jnp.where(hit, 1.0, 0.0).astype(BF16), before))
        pres.append(n_tile)
        n_tile = n_tile + jnp.sum(jnp.where(hit, 1.0, 0.0), axis=1, keepdims=True)
    n16 = ((n_tile.astype(I32) + (RUN_ALIGN - 1)) >> RUN_SHIFT) << RUN_SHIFT
    lower = jnp.where(lax.broadcasted_iota(I32, (N_EXPERTS, N_EXPERTS), 1)
                      < lax.broadcasted_iota(I32, (N_EXPERTS, N_EXPERTS), 0), 1.0, 0.0).astype(BF16)
    hi = jnp.broadcast_to((n16 >> 8).astype(F32), (N_EXPERTS, 128)).astype(BF16)
    lo = jnp.broadcast_to((n16 & 255).astype(F32), (N_EXPERTS, 128)).astype(BF16)
    lstart = (256.0 * _dot(lower, hi) + _dot(lower, lo))[:, 0:1]
    orow = lax.broadcasted_iota(I32, (8, tt), 0)
    out = jnp.zeros((8, tt), I32)
    for kk in range(TOP_K):
        slot = jnp.sum(jnp.where(hits[kk], lstart + pres[kk] + cums[kk], 0.0), axis=0, keepdims=True)
        out = jnp.where(orow == kk, slot.astype(I32) * ROWS_PER_TOKEN, out)
    lpos_ref[...] = out
    carry = carry_scr[...]
    ntab_ref[...] = jnp.broadcast_to(n_tile, ntab_ref.shape).astype(I32)
    cbtab_ref[...] = carry.astype(I32)
    carry = carry + n_tile
    carry_scr[...] = carry
    cnt_ref[...] = carry


def _rank(idx, *, T):
    tt = MOE_TOK_TILE
    nt = T // tt
    tab = pl.BlockSpec((None, N_EXPERTS, 128), lambda i: (i, 0, 0))
    return pl.pallas_call(
        _rank_kernel,
        grid=(nt,),
        in_specs=[pl.BlockSpec((8, tt), lambda i: (0, i))],
        out_specs=[pl.BlockSpec((8, tt), lambda i: (0, i)), tab, tab,
                   pl.BlockSpec((N_EXPERTS, 128), lambda i: (0, 0))],
        out_shape=[jax.ShapeDtypeStruct((8, T), I32), jax.ShapeDtypeStruct((nt, N_EXPERTS, 128), I32),
                   jax.ShapeDtypeStruct((nt, N_EXPERTS, 128), I32), jax.ShapeDtypeStruct((N_EXPERTS, 128), F32)],
        scratch_shapes=[pltpu.VMEM((N_EXPERTS, 128), F32)],
        compiler_params=_cparams(sem=("arbitrary",)),
        name="moe_rank",
    )(idx)


def _run_chunk(src_ref, src_tok, dst_ref, dst_tok, sem):
    return pltpu.make_async_copy(src_ref.at[_token_rows(src_tok, RUN_ALIGN), :],
                                 dst_ref.at[_token_rows(dst_tok, RUN_ALIGN), :], sem)


def _wait_chunks(src_ref, dst_ref, sem, count):
    def body(_, carry):
        _run_chunk(src_ref, 0, dst_ref, 0, sem).wait()
        return carry
    lax.fori_loop(0, count, body, 0)


def _copy_runs(step, pstart_ref, ntab_ref, cbtab_ref, chunk):
    def per_expert(e, carry):
        src0, total = carry
        nch = (ntab_ref[step * N_EXPERTS + e] + (RUN_ALIGN - 1)) >> RUN_SHIFT
        dst0 = pstart_ref[e] + cbtab_ref[step * N_EXPERTS + e]

        def one(c, cc):
            chunk(src0 + c * RUN_ALIGN, dst0 + c * RUN_ALIGN).start()
            return cc
        lax.fori_loop(0, nch, one, 0)
        return src0 + nch * RUN_ALIGN, total + nch
    return lax.fori_loop(0, N_EXPERTS, per_expert, (0, 0))[1]


def _dispatch_kernel(pstart_ref, cnt_ref, pad_ref, ntab_ref, cbtab_ref, lpos_ref, h_ref, xs_ref,
                     stage, zero_scr, sems, sent):
    step = pl.program_id(0)

    @pl.when(step == 0)
    def _():
        sem = sems.at[0]
        zero_scr[...] = jnp.zeros(zero_scr.shape, F32)

        def per_expert(e, total):
            first = pstart_ref[e] + ((cnt_ref[e] >> RUN_SHIFT) << RUN_SHIFT)
            n_fill = (pstart_ref[e] + pad_ref[e] - first) >> RUN_SHIFT

            def fill(c, carry):
                _run_chunk(zero_scr, 0, xs_ref, first + c * RUN_ALIGN, sem).start()
                return carry
            lax.fori_loop(0, n_fill, fill, 0)
            return total + n_fill
        n_total = lax.fori_loop(0, N_EXPERTS, per_expert, 0)
        used = pstart_ref[N_EXPERTS - 1] + pad_ref[N_EXPERTS - 1]
        n_tail = (xs_ref.shape[0] // ROWS_PER_TOKEN - used) >> RUN_SHIFT

        def fill_tail(c, carry):
            _run_chunk(zero_scr, 0, xs_ref, used + c * RUN_ALIGN, sem).start()
            return carry
        lax.fori_loop(0, n_tail, fill_tail, 0)
        _wait_chunks(zero_scr, xs_ref, sem, n_total + n_tail)
        sent[0] = 0
        sent[1] = 0

    slot = step % 2
    buf = stage.at[slot]
    sem = sems.at[slot]

    def zero_last_group(e, src0):
        n_up = ((ntab_ref[step * N_EXPERTS + e] + (RUN_ALIGN - 1)) >> RUN_SHIFT) << RUN_SHIFT
        buf[_token_rows(jnp.maximum(src0 + n_up - RUN_ALIGN, 0), RUN_ALIGN), :] = jnp.zeros((CHUNK_ROWS, 128), F32)
        return src0 + n_up
    lax.fori_loop(0, N_EXPERTS, zero_last_group, 0)
    tt = MOE_TOK_TILE
    for kk in range(TOP_K):
        def place(t, carry, kk=kk):
            row = pl.multiple_of(lpos_ref[kk * tt + t], ROWS_PER_TOKEN)
            buf[pl.ds(row, ROWS_PER_TOKEN), :] = h_ref[_token_rows(t), :]
            return carry
        lax.fori_loop(0, tt, place, 0, unroll=64)
    _wait_chunks(stage.at[1 - slot], xs_ref, sems.at[1 - slot], sent[1 - slot])
    sent[1 - slot] = 0
    sent[slot] = _copy_runs(step, pstart_ref, ntab_ref, cbtab_ref,
                            lambda s, d: _run_chunk(buf, s, xs_ref, d, sem))

    @pl.when(step == pl.num_programs(0) - 1)
    def _():
        _wait_chunks(buf, xs_ref, sem, sent[slot])


def _smem_tile_spec(tt):
    return pl.BlockSpec((8 * tt,), lambda i, *_: (i,), memory_space=pltpu.SMEM)


def _tile_major(table):
    tt = MOE_TOK_TILE
    return table.reshape(table.shape[0], -1, tt).transpose(1, 0, 2).reshape(-1)


def _dispatch(pstart, cnt, padded, ntab, cbtab, lpos, h2t, *, T, P):
    tt = MOE_TOK_TILE
    r = ROWS_PER_TOKEN
    return pl.pallas_call(
        _dispatch_kernel,
        grid_spec=pltpu.PrefetchScalarGridSpec(
            num_scalar_prefetch=5,
            grid=(T // tt,),
            in_specs=[_smem_tile_spec(tt), pl.BlockSpec((tt * r, 128), lambda i, *_: (i, 0))],
            out_specs=pl.BlockSpec(memory_space=pl.ANY),
            scratch_shapes=[pltpu.VMEM((2, STAGE_TOKENS * r, 128), F32), pltpu.VMEM((CHUNK_ROWS, 128), F32),
                            pltpu.SemaphoreType.DMA((2,)), pltpu.SMEM((2,), I32)],
        ),
        out_shape=jax.ShapeDtypeStruct((P * r, 128), F32),
        compiler_params=_cparams(sem=("arbitrary",), vmem=EXPERT_VMEM_LIMIT),
        name="moe_dispatch",
    )(pstart, cnt, padded, ntab, cbtab, _tile_major(lpos), h2t)


def _collect_kernel(pstart_ref, ntab_ref, cbtab_ref, lpos_ref, gate_ref, ys_ref, f_ref, stage, sems, sent):
    step = pl.program_id(0)
    slot = step % 2

    def fetch(tile, b):
        sent[b] = _copy_runs(tile, pstart_ref, ntab_ref, cbtab_ref,
                             lambda s, d: _run_chunk(ys_ref, d, stage.at[b], s, sems.at[b]))

    @pl.when(step == 0)
    def _():
        fetch(0, 0)

    @pl.when(step + 1 < pl.num_programs(0))
    def _():
        fetch(step + 1, 1 - slot)

    buf = stage.at[slot]
    _wait_chunks(ys_ref, buf, sems.at[slot], sent[slot])

    tt = MOE_TOK_TILE

    def combine(t, carry):
        acc = None
        for kk in range(TOP_K):
            row = pl.multiple_of(lpos_ref[kk * tt + t], ROWS_PER_TOKEN)
            term = gate_ref[kk * tt + t] * buf[pl.ds(row, ROWS_PER_TOKEN), :]
            acc = term if acc is None else acc + term
        f_ref[_token_rows(t), :] = acc
        return carry
    lax.fori_loop(0, tt, combine, 0, unroll=32)


def _collect(pstart, ntab, cbtab, lpos, gates, ys, *, T):
    tt = MOE_TOK_TILE
    r = ROWS_PER_TOKEN
    return pl.pallas_call(
        _collect_kernel,
        grid_spec=pltpu.PrefetchScalarGridSpec(
            num_scalar_prefetch=3,
            grid=(T // tt,),
            in_specs=[_smem_tile_spec(tt), _smem_tile_spec(tt), pl.BlockSpec(memory_space=pl.ANY)],
            out_specs=pl.BlockSpec((tt * r, 128), lambda i, *_: (i, 0)),
            scratch_shapes=[pltpu.VMEM((2, STAGE_TOKENS * r, 128), F32), pltpu.SemaphoreType.DMA((2,)),
                            pltpu.SMEM((2,), I32)],
        ),
        out_shape=jax.ShapeDtypeStruct((T * r, 128), F32),
        compiler_params=_cparams(sem=("arbitrary",), vmem=EXPERT_VMEM_LIMIT),
        name="moe_collect",
    )(pstart, ntab, cbtab, _tile_major(lpos), _tile_major(gates), ys)


def _deinterleave_matrix():
    m = np.zeros((256, 256), np.float32)
    j = np.arange(128)
    m[2 * j, j] = 1.0
    m[2 * j + 1, 128 + j] = 1.0
    return jnp.asarray(m, dtype=BF16)


def _expert_kernel(be_ref, nu_ref, x_ref, w1_hbm, b1_ref, w2_hbm, b2_ref, perm_ref, y_ref,
                   w1_buf, w2_buf, w1_scr, w2_scr, sem1, sem2, *, layer):
    i = pl.program_id(0)

    def weight_copies(e):
        slot = e % 2
        return (pltpu.make_async_copy(w1_hbm.at[layer, e], w1_buf.at[slot], sem1.at[slot]),
                pltpu.make_async_copy(w2_hbm.at[layer, e], w2_buf.at[slot], sem2.at[slot]))

    @pl.when(i >= nu_ref[0])
    def _():
        y_ref[...] = jnp.zeros(y_ref.shape, F32)

    @pl.when(i < nu_ref[0])
    def _():
        e = be_ref[i]

        @pl.when((i == 0) | (e != be_ref[jnp.maximum(i - 1, 0)]))
        def _():
            @pl.when(i == 0)
            def _():
                for cp in weight_copies(e):
                    cp.start()
            for cp in weight_copies(e):
                cp.wait()

            @pl.when(e + 1 < N_EXPERTS)
            def _():
                for cp in weight_copies(e + 1):
                    cp.start()
            w1_ref = w1_buf.at[e % 2]
            half = w1_scr.shape[1] // 2
            for m in range(w1_scr.shape[1] // 256):
                r = _dot(w1_ref[:, 256 * m:256 * (m + 1)].astype(BF16), perm_ref[...])
                w1_scr[:, 128 * m:128 * (m + 1)] = r[:, :128].astype(BF16)
                w1_scr[:, half + 128 * m:half + 128 * (m + 1)] = r[:, 128:].astype(BF16)
            w2_scr[...] = w2_buf[e % 2].astype(BF16)

        x = _from_token_tiles(x_ref, MOE_BLK).astype(BF16)
        u = _dot(x, w1_scr[...]) + b1_ref[...]
        half = u.shape[1] // 2
        glu = jnp.minimum(u[:, :half], SWIGLU_LIMIT)
        lin = jnp.clip(u[:, half:], -SWIGLU_LIMIT, SWIGLU_LIMIT)
        act = glu * jax.nn.sigmoid(SWIGLU_ALPHA * glu) * (lin + 1.0)
        _to_token_tiles(y_ref, _dot(act.astype(BF16), w2_scr[...]) + b2_ref[...])


def _experts(blk_e, n_used, xs, w1, b1, w2, b2, *, P, layer):
    nblk = P // MOE_BLK
    r = ROWS_PER_TOKEN
    d, two_f = w1.shape[2:]
    f = w2.shape[2]
    rows = lambda i, be, nu: (jnp.minimum(i, nu[0] - 1), 0)
    return pl.pallas_call(
        functools.partial(_expert_kernel, layer=layer),
        grid_spec=pltpu.PrefetchScalarGridSpec(
            num_scalar_prefetch=2,
            grid=(nblk,),
            in_specs=[
                pl.BlockSpec((MOE_BLK * r, 128), rows),
                pl.BlockSpec(memory_space=pl.ANY),
                pl.BlockSpec((None, 1, two_f), lambda i, be, nu: (be[i], 0, 0)),
                pl.BlockSpec(memory_space=pl.ANY),
                pl.BlockSpec((None, 1, d), lambda i, be, nu: (be[i], 0, 0)),
                pl.BlockSpec((256, 256), lambda i, be, nu: (0, 0)),
            ],
            out_specs=pl.BlockSpec((MOE_BLK * r, 128), lambda i, be, nu: (i, 0)),
            scratch_shapes=[pltpu.VMEM((2, d, two_f), F32), pltpu.VMEM((2, f, d), F32),
                            pltpu.VMEM((d, two_f), BF16), pltpu.VMEM((f, d), BF16),
                            pltpu.SemaphoreType.DMA((2,)), pltpu.SemaphoreType.DMA((2,))],
        ),
        out_shape=jax.ShapeDtypeStruct((P * r, 128), F32),
        compiler_params=_cparams(sem=("arbitrary",), vmem=EXPERT_VMEM_LIMIT),
        name="moe_experts",
    )(blk_e, n_used, xs, w1, b1, w2, b2, _deinterleave_matrix())


def _moe(h2t, idx, gates, w1, b1, w2, b2, *, T, layer):
    lpos, ntab, cbtab, cnt = _rank(idx, T=T)
    counts = cnt[:, 0].astype(I32)
    padded = (counts + RUN_ALIGN + MOE_BLK - 1) // MOE_BLK * MOE_BLK
    pad_end = jnp.cumsum(padded)
    pstart = (pad_end - padded).astype(I32)
    nblk = TOP_K * T // MOE_BLK + 2 * N_EXPERTS
    P = nblk * MOE_BLK
    starts = jnp.arange(nblk, dtype=I32) * MOE_BLK
    blk_e = jnp.minimum(jnp.sum((pad_end[None, :] <= starts[:, None]).astype(I32), axis=1), N_EXPERTS - 1)
    n_used = (pad_end[-1:] // MOE_BLK).astype(I32)
    ntab = ntab[:, :, 0].reshape(-1)
    cbtab = cbtab[:, :, 0].reshape(-1)
    xs = _dispatch(pstart, counts, padded.astype(I32), ntab, cbtab, lpos, h2t, T=T, P=P)
    ys = _experts(blk_e.astype(I32), n_used, xs, w1, b1, w2, b2, P=P, layer=layer)
    return _collect(pstart, ntab, cbtab, lpos, gates, ys, T=T)


def _residual_kernel(f_ref, x_ref, g2_ref, lg_ref, lb_ref, o_ref, *, nb, n_lat_tiles, alpha):
    for b in range(nb):
        f = _from_token_tiles(f_ref.at[b], TILE)
        z = alpha * x_ref[b] + g2_ref[_mod_row(b, n_lat_tiles, nb)] * f
        o_ref[b] = _normalize(z) * lg_ref[...] + lb_ref[...]


def _residual(ft, x1, g2, ln_g, ln_b, *, B, S, n_out_tiles, n_lat_tiles, alpha):
    r = ROWS_PER_TOKEN
    vec = pl.BlockSpec((1, D_MODEL), lambda i: (0, 0))
    return pl.pallas_call(
        functools.partial(_residual_kernel, nb=B, n_lat_tiles=n_lat_tiles, alpha=alpha),
        grid=(n_out_tiles,),
        in_specs=[
            pl.BlockSpec((B, TILE * r, 128), lambda i: (0, i, 0)),
            pl.BlockSpec((B, TILE, D_MODEL), lambda i: (0, i, 0)),
            _mod_spec(), vec, vec,
        ],
        out_specs=pl.BlockSpec((B, TILE, D_MODEL), lambda i: (0, i, 0)),
        out_shape=jax.ShapeDtypeStruct((B, n_out_tiles * TILE, D_MODEL), F32),
        compiler_params=_cparams(vmem=VMEM_LIMIT),
        name="ffn_residual_norm",
    )(ft.reshape(B, S * r, 128), x1, g2, ln_g, ln_b)


def _rope_tables(n_lat, n_ctx):
    half = HEAD_DIM // 2
    pos = np.arange(n_lat)
    inv = 1.0 / (ROPE_BASE ** (np.arange(0, half, 2, dtype=np.float64) / half))
    ang_row = (pos // GRID_W)[:, None].astype(np.float64) * inv[None, :]
    ang_col = (pos % GRID_W)[:, None].astype(np.float64) * inv[None, :]
    ang = np.concatenate([ang_row, ang_row, ang_col, ang_col], axis=1)
    sign = np.concatenate([-np.ones(16), np.ones(16)] * 2)[None, :]
    cos = np.concatenate([np.cos(ang), np.ones((n_ctx, HEAD_DIM))], axis=0)
    sin = np.concatenate([np.sin(ang) * sign, np.zeros((n_ctx, HEAD_DIM))], axis=0)
    return (jnp.asarray(np.tile(cos, (1, 2)), dtype=F32), jnp.asarray(np.tile(sin, (1, 2)), dtype=F32))


def _extend_w_in(w):
    g = MIX_GROUP_W
    hd = HEAD_DIM
    a, bp, q = w[:, 0:g], w[:, g:2 * g], w[:, 2 * g:3 * g]
    k, v = w[:, 3 * g:3 * g + 2 * hd], w[:, 3 * g + 2 * hd:4 * g]
    rest = w[:, 4 * g:]
    dup = lambda t: jnp.concatenate([t[:, :hd], t[:, :hd], t[:, hd:], t[:, hd:]], axis=1)
    return jnp.concatenate([a, bp, q, dup(k), dup(v), rest], axis=1).astype(BF16)


def _block_diag(pw):
    n, c, _ = pw.shape
    out = jnp.zeros((n * c, n * c), pw.dtype)
    for gi in range(n):
        out = lax.dynamic_update_slice(out, pw[gi], (gi * c, gi * c))
    return out


def kernel(x, c, ctx, c_ctx, w_mod, b_mod, w_in, pool_w, pool_scale, attn_sink, ret_decay, w_out, ln_g, ln_b,
           router_w, router_b, exp_w1, exp_b1, exp_w2, exp_b2):
    B, N, D = x.shape
    L = ctx.shape[1]
    depth = w_mod.shape[0]
    assert D == D_MODEL and L == TILE and N % TILE == 0 and N % (FFT_RADIX2 * 16) == 0 and B < 8
    S = N + L
    n_lat_tiles = N // TILE
    assert (B * S) % MOE_TOK_TILE == 0 and (B * N) % MOE_TOK_TILE == 0
    alpha = float((2 * depth) ** 0.25)

    cvec = jnp.zeros((8, D), F32).at[:B].set(c).at[B].set(c_ctx)
    mods = _modulation(cvec, w_mod, b_mod)
    cos, sin = _rope_tables(N, L)

    xs = (x, ctx)
    out = None
    for l in range(depth):
        last = l == depth - 1
        m = mods[l].reshape(8, 6, 1, D)
        sh1, sc1, g1, sh2, sc2, g2 = [m[:, j] for j in range(6)]
        proj = _projection(xs, sc1, sh1, _extend_w_in(w_in[l]), cos, sin, B=B, S=S, n_lat_tiles=n_lat_tiles)
        a, bp, q, k2, v2, rq, rk, rv, rg = proj
        y_f = _fourier_mix(a, B=B, N=N, L=L, with_ctx=not last)
        y_p = _pool_mix(bp, _block_diag(pool_w[l]).astype(BF16), pool_scale[l].reshape(1, -1), B=B, N=N, L=L)
        y_a = _attention(attn_sink[l], q, k2, v2, B=B, N=N, L=L)
        dec = jnp.repeat(ret_decay[l], MIX_GROUP_W // RET_HEADS, axis=1)
        y_r = _retention(dec, rq, rk, rv, rg, B=B, N=N, L=L)
        rows = N if last else S
        x1, h2t, idx, gates = _merge(
            xs, (y_f, y_p, y_a, y_r), g1, sc2, sh2, ln_g[l, 0:1], ln_b[l, 0:1], w_out[l].astype(BF16),
            router_w[l].T.astype(BF16), router_b[l].reshape(-1, 1), B=B, S=rows, n_lat_tiles=n_lat_tiles, alpha=alpha)
        b1 = jnp.concatenate([exp_b1[l][:, 0::2], exp_b1[l][:, 1::2]], axis=-1)[:, None, :]
        ft = _moe(h2t, idx, gates, exp_w1, b1, exp_w2, exp_b2[l][:, None, :], T=B * rows, layer=l)
        out = _residual(ft, x1, g2, ln_g[l, 1:2], ln_b[l, 1:2], B=B, S=rows,
                        n_out_tiles=rows // TILE, n_lat_tiles=n_lat_tiles, alpha=alpha)
        xs = (out,)
    return out
```
